```python
import math, functools
import jax, jax.numpy as jnp
from jax import lax
import numpy as np

D_MODEL = 2048
BATCH = 1
SEQ = 8192
DEPTH = 2
DEC_BATCH = 32
DEC_SEQ = 4
PAST_LEN = 8192
PAGE_SIZE = 128

D_MIX = D_MODEL
N_MIXERS = 4
GROUP_W = D_MIX // N_MIXERS
HEAD_DIM = 128
N_GROUP_HEADS = GROUP_W // HEAD_DIM
DIFF_QK = HEAD_DIM // 2
ROT_DIM = DIFF_QK // 4
ROPE_THETA = 500000.0
CONF_WIDTH = 31
SC_WIDTH = 3
Q_BLOCK = 128
N_IN_SLICES = 15
EPS = 1e-6
SB_SCALE = 1.0 / math.sqrt(HEAD_DIM)
DIFF_SCALE = 1.0 / math.sqrt(DIFF_QK)
POOL_NUM = 5
POOL_DEN = 4

kernel_name = 'hybrid_conformer_stickbreak_diffattn_shortconv_step'


def rmsnorm(x, g):
    xf = x.astype(jnp.float32)
    y = xf * lax.rsqrt(jnp.mean(xf * xf, axis=-1, keepdims=True) + EPS)
    return (y * g.astype(jnp.float32)).astype(x.dtype)


def layernorm(x, g, b):
    xf = x.astype(jnp.float32)
    mu = jnp.mean(xf, axis=-1, keepdims=True)
    xc = xf - mu
    y = xc * lax.rsqrt(jnp.mean(xc * xc, axis=-1, keepdims=True) + EPS)
    return (y * g.astype(jnp.float32) + b.astype(jnp.float32)).astype(x.dtype)


def rope(x, pos):
    half = ROT_DIM // 2
    inv = ROPE_THETA ** (-2.0 * jnp.arange(half, dtype=jnp.float32) / ROT_DIM)
    ang = pos.astype(jnp.float32)[:, None] * inv[None, :]
    cos = jnp.cos(ang)[None, :, None, None, :]
    sin = jnp.sin(ang)[None, :, None, None, :]
    xf = x.astype(jnp.float32)
    x1 = xf[..., :half]
    x2 = xf[..., half:ROT_DIM]
    out = jnp.concatenate([x1 * cos - x2 * sin, x2 * cos + x1 * sin, xf[..., ROT_DIM:]], axis=-1)
    return out.astype(x.dtype)


def causal_dwconv(xp, w):
    c = xp.shape[-1]
    return lax.conv_general_dilated(xp, w[:, None, :].astype(xp.dtype), window_strides=(1,),
                                    padding='VALID', dimension_numbers=('NWC', 'WIO', 'NWC'),
                                    feature_group_count=c)


def sb_attend(q, k, v, q_pos, k_pos):
    z = jnp.einsum('bqhd,bshd->bhqs', q, k).astype(jnp.float32) * SB_SCALE
    mask = (k_pos[None, :] < q_pos[:, None])[None, None]
    log_keep = jnp.where(mask, jax.nn.log_sigmoid(-z), 0.0)
    tail = lax.cumsum(log_keep, axis=3, reverse=True) - log_keep
    a = jnp.where(mask, jnp.exp(jax.nn.log_sigmoid(z) + tail), 0.0)
    return jnp.einsum('bhqs,bshd->bqhd', a.astype(v.dtype), v)


def diff_attend(q, k, v, q_pos, k_pos, lam):
    b, nq, h, _ = q.shape
    s_len = k.shape[1]
    q2 = q.reshape(b, nq, h, 2, DIFF_QK)
    k2 = k.reshape(b, s_len, h, 2, DIFF_QK)
    s = jnp.einsum('bqhcd,bshcd->bhcqs', q2, k2).astype(jnp.float32) * DIFF_SCALE
    mask = k_pos[None, :] <= q_pos[:, None]
    p = jax.nn.softmax(jnp.where(mask, s, -jnp.inf), axis=-1)
    w = p[:, :, 0] - lam * p[:, :, 1]
    return jnp.einsum('bhqs,bshd->bqhd', w.astype(v.dtype), v)


def sweep_blocks(fn, q, k, v, q_pos, k_pos):
    b, t = q.shape[:2]
    blk = Q_BLOCK if t % Q_BLOCK == 0 else t
    nb = t // blk
    qb = jnp.moveaxis(q.reshape((b, nb, blk) + q.shape[2:]), 1, 0)
    pb = q_pos.reshape(nb, blk)
    out = lax.map(lambda a: fn(a[0], k, v, a[1], k_pos), (qb, pb))
    return jnp.moveaxis(out, 0, 1).reshape((b, t) + out.shape[3:])


def mixer_layer(x, q_pos, conf_buf, sc_buf, past, layer_idx, norm_g, w_in, conf_dw_w, conf_dw_b,
                conf_ln_g, conf_ln_b, diff_lq1, diff_lk1, diff_lq2, diff_lk2, diff_subln_g,
                sc_conv_w, w_out):
    b, t, _ = x.shape
    xn = rmsnorm(x, norm_g)
    (conf_a, conf_b, conf_g, sb_q, sb_k, sb_v, sb_g, df_q, df_k, df_v, df_g,
     sc_h, sc_b, sc_c, sc_g) = jnp.split(xn @ w_in, N_IN_SLICES, axis=-1)

    h = conf_a * jax.nn.sigmoid(conf_b)
    hp = jnp.concatenate([conf_buf.astype(h.dtype), h], axis=1)
    ya = causal_dwconv(hp, conf_dw_w) + conf_dw_b
    ya = jax.nn.silu(layernorm(ya, conf_ln_g, conf_ln_b)) * jax.nn.silu(conf_g)
    new_conf = hp[:, -(CONF_WIDTH - 1):]

    heads = lambda a: a.reshape(b, t, N_GROUP_HEADS, HEAD_DIM)
    sq, sk, sv = heads(sb_q), heads(sb_k), heads(sb_v)
    comp = lambda a: rope(a.reshape(b, t, N_GROUP_HEADS, 2, DIFF_QK), q_pos).reshape(b, t, N_GROUP_HEADS, HEAD_DIM)
    dq, dk, dv = comp(df_q), comp(df_k), heads(df_v)

    if past is None:
        sk_all, sv_all, dk_all, dv_all = sk, sv, dk, dv
        k_pos = q_pos
    else:
        psk, psv, pdk, pdv = past
        sk_all = jnp.concatenate([psk.astype(sk.dtype), sk], axis=1)
        sv_all = jnp.concatenate([psv.astype(sv.dtype), sv], axis=1)
        dk_all = jnp.concatenate([pdk.astype(dk.dtype), dk], axis=1)
        dv_all = jnp.concatenate([pdv.astype(dv.dtype), dv], axis=1)
        k_pos = jnp.arange(sk_all.shape[1], dtype=jnp.int32)

    yb = sweep_blocks(sb_attend, sq, sk_all, sv_all, q_pos, k_pos)
    yb = yb.reshape(b, t, GROUP_W) * jax.nn.silu(sb_g)

    lam_init = 0.8 - 0.6 * math.exp(-0.3 * layer_idx)
    lam = (jnp.exp(jnp.sum(diff_lq1.astype(jnp.float32) * diff_lk1.astype(jnp.float32)))
           - jnp.exp(jnp.sum(diff_lq2.astype(jnp.float32) * diff_lk2.astype(jnp.float32))) + lam_init)
    yc = sweep_blocks(functools.partial(diff_attend, lam=lam), dq, dk_all, dv_all, q_pos, k_pos)
    yc = rmsnorm(yc, diff_subln_g) * (1.0 - lam_init)
    yc = yc.reshape(b, t, GROUP_W) * jax.nn.silu(df_g)

    u = sc_c * sc_h
    up = jnp.concatenate([sc_buf.astype(u.dtype), u], axis=1)
    yd = sc_b * causal_dwconv(up, sc_conv_w) * jax.nn.silu(sc_g)
    new_sc = up[:, -(SC_WIDTH - 1):]

    y = jnp.concatenate([ya, yb, yc, yd], axis=-1) @ w_out
    return x + y, (sk, sv, dk, dv, new_conf, new_sc)


def setup_inputs(seed: int = 0) -> dict:
    key = jax.random.key(seed)
    ks = jax.random.split(key, 24)
    f32 = jnp.float32
    n_pages = PAST_LEN // PAGE_SIZE
    n_pool = (POOL_NUM * DEC_BATCH * n_pages) // POOL_DEN
    nrm = lambda k, shape, s=1.0: s * jax.random.normal(k, shape, f32)
    kv_shape = (DEPTH, n_pool, PAGE_SIZE, N_GROUP_HEADS, HEAD_DIM)
    perm = jax.random.permutation(ks[8], n_pool)
    page_table = perm[: DEC_BATCH * n_pages].reshape(DEC_BATCH, n_pages).astype(jnp.int32)
    return {
        'x_prompt': nrm(ks[0], (BATCH, SEQ, D_MODEL)),
        'x_sample': nrm(ks[1], (DEC_BATCH, DEC_SEQ, D_MODEL)),
        'cache_sb_k': nrm(ks[2], kv_shape),
        'cache_sb_v': nrm(ks[3], kv_shape),
        'cache_diff_k': nrm(ks[4], kv_shape),
        'cache_diff_v': nrm(ks[5], kv_shape),
        'state_conf_conv': nrm(ks[6], (DEPTH, DEC_BATCH, CONF_WIDTH - 1, GROUP_W), 0.5),
        'state_short_conv': nrm(ks[7], (DEPTH, DEC_BATCH, SC_WIDTH - 1, GROUP_W), 0.5),
        'page_table': page_table,
        'norm_g': 1.0 + nrm(ks[9], (DEPTH, D_MODEL), 0.02),
        'w_in': nrm(ks[10], (DEPTH, D_MODEL, N_IN_SLICES * GROUP_W), D_MODEL ** -0.5),
        'conf_dw_w': nrm(ks[11], (DEPTH, CONF_WIDTH, GROUP_W), CONF_WIDTH ** -0.5),
        'conf_dw_b': nrm(ks[12], (DEPTH, GROUP_W), 0.02),
        'conf_ln_g': 1.0 + nrm(ks[13], (DEPTH, GROUP_W), 0.02),
        'conf_ln_b': nrm(ks[14], (DEPTH, GROUP_W), 0.02),
        'diff_lq1': nrm(ks[15], (DEPTH, DIFF_QK), 0.1),
        'diff_lk1': nrm(ks[16], (DEPTH, DIFF_QK), 0.1),
        'diff_lq2': nrm(ks[17], (DEPTH, DIFF_QK), 0.1),
        'diff_lk2': nrm(ks[18], (DEPTH, DIFF_QK), 0.1),
        'diff_subln_g': 1.0 + nrm(ks[19], (DEPTH, HEAD_DIM), 0.02),
        'sc_conv_w': nrm(ks[20], (DEPTH, SC_WIDTH, GROUP_W), SC_WIDTH ** -0.5),
        'w_out': nrm(ks[21], (DEPTH, D_MIX, D_MODEL), D_MIX ** -0.5),
        'final_norm_g': 1.0 + nrm(ks[22], (D_MODEL,), 0.02),
    }


def reference(x_prompt, x_sample, cache_sb_k, cache_sb_v, cache_diff_k, cache_diff_v,
              state_conf_conv, state_short_conv, page_table, norm_g, w_in, conf_dw_w, conf_dw_b,
              conf_ln_g, conf_ln_b, diff_lq1, diff_lk1, diff_lq2, diff_lk2, diff_subln_g,
              sc_conv_w, w_out, final_norm_g):
    bp, tp = x_prompt.shape[:2]
    bs, ts = x_sample.shape[:2]
    pos_p = jnp.arange(tp, dtype=jnp.int32)
    n_past = page_table.shape[1] * cache_sb_k.shape[2]
    pos_s = n_past + jnp.arange(ts, dtype=jnp.int32)

    hp_, hs_ = x_prompt, x_sample
    p_new = [[] for _ in range(6)]
    s_new = [[] for _ in range(6)]
    for l in range(DEPTH):
        w_l = (norm_g[l], w_in[l], conf_dw_w[l], conf_dw_b[l], conf_ln_g[l], conf_ln_b[l],
               diff_lq1[l], diff_lk1[l], diff_lq2[l], diff_lk2[l], diff_subln_g[l],
               sc_conv_w[l], w_out[l])
        zc = jnp.zeros((bp, CONF_WIDTH - 1, GROUP_W), x_prompt.dtype)
        zs = jnp.zeros((bp, SC_WIDTH - 1, GROUP_W), x_prompt.dtype)
        hp_, pst = mixer_layer(hp_, pos_p, zc, zs, None, l, *w_l)

        def gather(c):
            pages = c[l, page_table]
            return pages.reshape((bs, -1) + pages.shape[3:])
        past = (gather(cache_sb_k), gather(cache_sb_v), gather(cache_diff_k), gather(cache_diff_v))
        hs_, sst = mixer_layer(hs_, pos_s, state_conf_conv[l], state_short_conv[l], past, l, *w_l)
        for i in range(6):
            p_new[i].append(pst[i])
            s_new[i].append(sst[i])

    y_prompt = rmsnorm(hp_, final_norm_g)
    y_sample = rmsnorm(hs_, final_norm_g)
    p_sb_k, p_sb_v, p_diff_k, p_diff_v, p_conf_conv, p_short_conv = [jnp.stack(a, 0) for a in p_new]
    s_sb_k, s_sb_v, s_diff_k, s_diff_v, s_conf_conv, s_short_conv = [jnp.stack(a, 0) for a in s_new]
    return (y_prompt, y_sample, p_sb_k, p_sb_v, p_diff_k, p_diff_v, p_conf_conv, p_short_conv,
            s_sb_k, s_sb_v, s_diff_k, s_diff_v, s_conf_conv, s_short_conv)
```

```python
import functools
import math

import jax
import jax.numpy as jnp
from jax import lax
from jax.experimental import pallas as pl
from jax.experimental.pallas import tpu as pltpu

F32 = jnp.float32
BF16 = jnp.bfloat16

N_MIXERS = 4
N_HEADS = 4
HEAD_DIM = 128
GROUP_W = N_HEADS * HEAD_DIM
DIFF_QK = HEAD_DIM // 2
ROT_DIM = DIFF_QK // 4
ROPE_THETA = 500000.0
N_IN_SLICES = 15
EPS = 1e-6
SB_SCALE = 1.0 / math.sqrt(HEAD_DIM)
DIFF_SCALE = 1.0 / math.sqrt(DIFF_QK)
PAD_T = 8
SB_EXIT_LOG = -104.0
VMEM_LIMIT = 56 * 1024 * 1024

(S_CONF_A, S_CONF_B, S_CONF_G, S_SB_Q, S_SB_K, S_SB_V, S_SB_G, S_DF_Q, S_DF_K, S_DF_V, S_DF_G,
 S_SC_H, S_SC_B, S_SC_C, S_SC_G) = range(N_IN_SLICES)


def _silu(x):
    return x * (1.0 / (1.0 + jnp.exp(-x)))


def _cparams(sem):
    return pltpu.CompilerParams(dimension_semantics=sem, vmem_limit_bytes=VMEM_LIMIT)


def _inproj_kernel(x_ref, g_ref, w_ref, o_ref, xn_ref):
    @pl.when(pl.program_id(1) == 0)
    def _():
        x = x_ref[...]
        ms = jnp.mean(x * x, axis=-1, keepdims=True)
        xn_ref[...] = ((x * lax.rsqrt(ms + EPS)) * g_ref[...]).astype(BF16)

    o_ref[...] = jnp.dot(xn_ref[...], w_ref[...], preferred_element_type=F32)


def in_proj(x2d, g, w_bf16, tm):
    m, d = x2d.shape
    n = w_bf16.shape[1]
    ns = n // GROUP_W
    return pl.pallas_call(
        _inproj_kernel,
        grid=(m // tm, ns),
        in_specs=[pl.BlockSpec((tm, d), lambda i, j: (i, 0)),
                  pl.BlockSpec((1, d), lambda i, j: (0, 0)),
                  pl.BlockSpec((d, GROUP_W), lambda i, j: (0, j))],
        out_specs=pl.BlockSpec((None, tm, GROUP_W), lambda i, j: (j, i, 0)),
        out_shape=jax.ShapeDtypeStruct((ns, m, GROUP_W), F32),
        scratch_shapes=[pltpu.VMEM((tm, d), BF16)],
        compiler_params=_cparams(("parallel", "arbitrary")),
        name="in_proj",
    )(x2d, g.reshape(1, d), w_bf16)


def _outproj_kernel(ya_ref, yb_ref, yc_ref, yd_ref, w_ref, x_ref, fg_ref, o_ref, *, final):
    acc = x_ref[...]
    for m, y_ref in enumerate((ya_ref, yb_ref, yc_ref, yd_ref)):
        acc = acc + jnp.dot(y_ref[...], w_ref[m * GROUP_W:(m + 1) * GROUP_W, :],
                            preferred_element_type=F32)
    if final:
        ms = jnp.mean(acc * acc, axis=-1, keepdims=True)
        acc = (acc * lax.rsqrt(ms + EPS)) * fg_ref[...]
    o_ref[...] = acc


def out_proj(ys, w_bf16, x2d, final_g, final, tm):
    m, d = x2d.shape
    yspec = pl.BlockSpec((tm, GROUP_W), lambda i: (i, 0))
    return pl.pallas_call(
        functools.partial(_outproj_kernel, final=final),
        grid=(m // tm,),
        in_specs=[yspec, yspec, yspec, yspec,
                  pl.BlockSpec((N_MIXERS * GROUP_W, d), lambda i: (0, 0)),
                  pl.BlockSpec((tm, d), lambda i: (i, 0)),
                  pl.BlockSpec((1, d), lambda i: (0, 0))],
        out_specs=pl.BlockSpec((tm, d), lambda i: (i, 0)),
        out_shape=jax.ShapeDtypeStruct((m, d), F32),
        compiler_params=_cparams(("parallel",)),
        name="out_proj",
    )(*ys, w_bf16, x2d, final_g.reshape(1, d))


def _sb_terms(z):
    t = jnp.log1p(jnp.exp(-jnp.abs(z)))
    return jnp.minimum(z, 0.0) - t, -jnp.maximum(z, 0.0) - t


def _suffix_sums(lk, u):
    hi = lk.astype(BF16)
    lo = (lk - hi.astype(F32)).astype(BF16)
    return (jnp.dot(hi, u, preferred_element_type=F32) + jnp.dot(lo, u, preferred_element_type=F32))


def _strict_lower(n):
    row = lax.broadcasted_iota(jnp.int32, (n, n), 0)
    col = lax.broadcasted_iota(jnp.int32, (n, n), 1)
    return row, col


def _qk(q, k):
    return lax.dot_general(q, k.astype(BF16), (((1,), (1,)), ((), ())), preferred_element_type=F32)


def _sb_prompt_kernel(q_ref, k_ref, v_ref, g_ref, o_ref, acc_ref, carry_ref, *, tb):
    i = pl.program_id(1)
    q = q_ref[...].astype(BF16)
    row, col = _strict_lower(tb)
    u = jnp.where(row > col, 1.0, 0.0).astype(BF16)
    visible = col < row

    def scores(jb):
        start = pl.multiple_of(jb * tb, tb)
        z = _qk(q, k_ref[pl.ds(start, tb), :]) * SB_SCALE
        return z, v_ref[pl.ds(start, tb), :].astype(BF16)

    z, v = scores(i)
    ls, lk = _sb_terms(z)
    lk = jnp.where(visible, lk, 0.0)
    a = jnp.where(visible, jnp.exp(ls + _suffix_sums(lk, u)), 0.0)
    acc_ref[...] = jnp.dot(a.astype(BF16), v, preferred_element_type=F32)
    carry_ref[...] = jnp.sum(lk, axis=1, keepdims=True)

    def cond(state):
        jb, go = state
        return jnp.logical_and(jb >= 0, go)

    def body(state):
        jb, _ = state
        z, v = scores(jb)
        ls, lk = _sb_terms(z)
        carry = carry_ref[...]
        a = jnp.exp(ls + _suffix_sums(lk, u) + carry)
        acc_ref[...] += jnp.dot(a.astype(BF16), v, preferred_element_type=F32)
        carry = carry + jnp.sum(lk, axis=1, keepdims=True)
        carry_ref[...] = carry
        return jb - 1, jnp.max(carry) > SB_EXIT_LOG

    lax.while_loop(cond, body, (i - 1, jnp.max(carry_ref[...]) > SB_EXIT_LOG))
    o_ref[...] = (acc_ref[...] * _silu(g_ref[...])).astype(o_ref.dtype)


def sb_prompt(proj, tb):
    _, t, _ = proj.shape

    def spec(rows, s, per_q):
        if per_q:
            return pl.BlockSpec((None, rows, HEAD_DIM), lambda h, i: (s, i, h))
        return pl.BlockSpec((None, rows, HEAD_DIM), lambda h, i: (s, 0, h))

    return pl.pallas_call(
        functools.partial(_sb_prompt_kernel, tb=tb),
        grid=(N_HEADS, t // tb),
        in_specs=[spec(tb, S_SB_Q, True), spec(t, S_SB_K, False), spec(t, S_SB_V, False),
                  spec(tb, S_SB_G, True)],
        out_specs=pl.BlockSpec((tb, HEAD_DIM), lambda h, i: (i, h)),
        out_shape=jax.ShapeDtypeStruct((t, GROUP_W), BF16),
        scratch_shapes=[pltpu.VMEM((tb, HEAD_DIM), F32), pltpu.VMEM((tb, 1), F32)],
        compiler_params=_cparams(("parallel", "arbitrary")),
        name="sb_prompt",
    )(proj, proj, proj, proj)


def _token_of_row(shape):
    return lax.broadcasted_iota(jnp.int32, shape, 0) & (PAD_T - 1)


def _head_cols(h):
    return slice(h * HEAD_DIM, (h + 1) * HEAD_DIM)


def _sb_decode_kernel(pt_ref, q_ref, kn_ref, vn_ref, g_ref, ck_ref, cv_ref, o_ref,
                      kbuf, vbuf, newk, newv, sem, acc_ref, carry_ref, *, layer, n_pages, n_tok,
                      page):
    b = pl.program_id(0)
    rows = N_HEADS * PAD_T

    def page_copies(p, slot):
        pg = pt_ref[b, p]
        return (pltpu.make_async_copy(ck_ref.at[layer, pg], kbuf.at[slot], sem.at[0, slot]),
                pltpu.make_async_copy(cv_ref.at[layer, pg], vbuf.at[slot], sem.at[1, slot]))

    def start_fetch(p, slot):
        for c in page_copies(p, slot):
            c.start()

    def wait_fetch(p, slot):
        for c in page_copies(p, slot):
            c.wait()

    start_fetch(n_pages - 1, 0)

    q8 = q_ref[...]
    q_heads = [q8[:, _head_cols(h)].astype(BF16) for h in range(N_HEADS)]
    row, col = _strict_lower(page)
    u = jnp.where(row > col, 1.0, 0.0).astype(BF16)
    tok = _token_of_row((rows, page))
    key = lax.broadcasted_iota(jnp.int32, (rows, page), 1)
    real_row = _token_of_row((rows, 1)) < n_tok

    def attend(head_k, head_v, visible, carry):
        z = jnp.concatenate([_qk(q_heads[h], head_k(h)) for h in range(N_HEADS)], axis=0) * SB_SCALE
        ls, lk = _sb_terms(z)
        if visible is not None:
            lk = jnp.where(visible, lk, 0.0)
        e = ls + _suffix_sums(lk, u)
        if carry is not None:
            e = e + carry
        a = jnp.exp(e)
        if visible is not None:
            a = jnp.where(visible, a, 0.0)
        a = a.astype(BF16)
        pv = jnp.concatenate(
            [jnp.dot(a[h * PAD_T:(h + 1) * PAD_T], head_v(h).astype(BF16), preferred_element_type=F32)
             for h in range(N_HEADS)], axis=1)
        return pv, jnp.sum(lk, axis=1, keepdims=True)

    def live(carry):
        return jnp.max(jnp.where(real_row, carry, -jnp.inf)) > SB_EXIT_LOG

    newk[...] = jnp.zeros_like(newk)
    newv[...] = jnp.zeros_like(newv)
    newk[0:PAD_T, :] = kn_ref[...]
    newv[0:PAD_T, :] = vn_ref[...]
    pv, dc = attend(lambda h: newk[:, _head_cols(h)], lambda h: newv[:, _head_cols(h)],
                    key < tok, None)
    acc_ref[...] = pv
    carry_ref[...] = dc

    def cond(state):
        p, go = state
        return jnp.logical_and(p >= 0, go)

    def body(state):
        p, _ = state
        slot = (n_pages - 1 - p) % 2
        wait_fetch(p, slot)

        @pl.when(p >= 1)
        def _():
            start_fetch(p - 1, 1 - slot)

        carry = carry_ref[...]
        pv, dc = attend(lambda h: kbuf[slot, :, h, :], lambda h: vbuf[slot, :, h, :], None, carry)
        acc_ref[...] += pv
        carry = carry + dc
        carry_ref[...] = carry
        return p - 1, live(carry)

    p_end, _ = lax.while_loop(cond, body, (jnp.int32(n_pages - 1), live(carry_ref[...])))

    @pl.when(p_end >= 0)
    def _():
        wait_fetch(p_end, (n_pages - 1 - p_end) % 2)

    o_ref[...] = acc_ref[...] * _silu(g_ref[...])


def sb_decode(q8, kn8, vn8, g8, cache_k, cache_v, page_table, layer, n_tok):
    bsz = q8.shape[0]
    n_pages = page_table.shape[1]
    page = cache_k.shape[2]
    tok_spec = pl.BlockSpec((None, PAD_T, GROUP_W), lambda b, pt: (b, 0, 0))
    any_spec = pl.BlockSpec(memory_space=pl.ANY)
    rows = N_HEADS * PAD_T
    page_buf = pltpu.VMEM((2, page, N_HEADS, HEAD_DIM), F32)
    return pl.pallas_call(
        functools.partial(_sb_decode_kernel, layer=layer, n_pages=n_pages, n_tok=n_tok, page=page),
        grid_spec=pltpu.PrefetchScalarGridSpec(
            num_scalar_prefetch=1,
            grid=(bsz,),
            in_specs=[tok_spec, tok_spec, tok_spec, tok_spec, any_spec, any_spec],
            out_specs=tok_spec,
            scratch_shapes=[page_buf, page_buf,
                            pltpu.VMEM((page, GROUP_W), F32), pltpu.VMEM((page, GROUP_W), F32),
                            pltpu.SemaphoreType.DMA((2, 2)),
                            pltpu.VMEM((PAD_T, GROUP_W), F32), pltpu.VMEM((rows, 1), F32)]),
        out_shape=jax.ShapeDtypeStruct((bsz, PAD_T, GROUP_W), F32),
        compiler_params=_cparams(("arbitrary",)),
        name="sb_decode",
    )(page_table, q8, kn8, vn8, g8, cache_k, cache_v)


def _lambda(lq1_ref, lk1_ref, lq2_ref, lk2_ref, lam_init):
    s1 = jnp.sum(lq1_ref[...] * lk1_ref[...], axis=-1, keepdims=True)
    s2 = jnp.sum(lq2_ref[...] * lk2_ref[...], axis=-1, keepdims=True)
    return jnp.exp(s1) - jnp.exp(s2) + lam_init


def _diff_finish(o1, o2, lam, sg, gate, lam_init):
    w = o1 - lam * o2
    ms = jnp.mean(w * w, axis=-1, keepdims=True)
    y = (w * lax.rsqrt(ms + EPS)) * sg
    return (y * (1.0 - lam_init)) * _silu(gate)


def _split_components(q):
    lane = lax.broadcasted_iota(jnp.int32, q.shape, 1)
    zero = jnp.zeros_like(q)
    return jnp.concatenate([jnp.where(lane < DIFF_QK, q, zero), jnp.where(lane >= DIFF_QK, q, zero)],
                           axis=0)


def _diff_prompt_kernel(q_ref, k_ref, v_ref, g_ref, lq1_ref, lk1_ref, lq2_ref, lk2_ref, sg_ref,
                        o_ref, m_ref, l_ref, acc_ref, *, tb, lam_init):
    i = pl.program_id(1)
    qq = _split_components(q_ref[...])
    m_ref[...] = jnp.full_like(m_ref, -jnp.inf)
    l_ref[...] = jnp.zeros_like(l_ref)
    acc_ref[...] = jnp.zeros_like(acc_ref)

    def step(jb, masked):
        start = pl.multiple_of(jb * tb, tb)
        s = _qk(qq, k_ref[pl.ds(start, tb), :])
        v = v_ref[pl.ds(start, tb), :]
        if masked:
            row = lax.broadcasted_iota(jnp.int32, s.shape, 0) & (tb - 1)
            col = lax.broadcasted_iota(jnp.int32, s.shape, 1)
            s = jnp.where(col <= row, s, -jnp.inf)
        m_prev = m_ref[...]
        m_new = jnp.maximum(m_prev, jnp.max(s, axis=1, keepdims=True))
        alpha = jnp.exp(m_prev - m_new)
        p = jnp.exp(s - m_new[:, 0:1])
        l_ref[...] = alpha * l_ref[...] + jnp.sum(p, axis=1, keepdims=True)
        acc_ref[...] = alpha * acc_ref[...] + jnp.dot(p.astype(BF16), v, preferred_element_type=F32)
        m_ref[...] = m_new

    def body(jb, carry):
        step(jb, False)
        return carry

    lax.fori_loop(0, i, body, 0)
    step(i, True)

    o = acc_ref[...] / l_ref[...]
    lam = _lambda(lq1_ref, lk1_ref, lq2_ref, lk2_ref, lam_init)
    o_ref[...] = _diff_finish(o[:tb], o[tb:], lam, sg_ref[...], g_ref[...], lam_init).astype(o_ref.dtype)


def diff_prompt(qb, kb, vb, proj, lam_params, subln_g, lam_init, tb):
    t = qb.shape[0]
    small = pl.BlockSpec((1, DIFF_QK), lambda h, i: (0, 0))
    return pl.pallas_call(
        functools.partial(_diff_prompt_kernel, tb=tb, lam_init=lam_init),
        grid=(N_HEADS, t // tb),
        in_specs=[pl.BlockSpec((tb, HEAD_DIM), lambda h, i: (i, h)),
                  pl.BlockSpec((t, HEAD_DIM), lambda h, i: (0, h)),
                  pl.BlockSpec((t, HEAD_DIM), lambda h, i: (0, h)),
                  pl.BlockSpec((None, tb, HEAD_DIM), lambda h, i: (S_DF_G, i, h)),
                  small, small, small, small,
                  pl.BlockSpec((1, HEAD_DIM), lambda h, i: (0, 0))],
        out_specs=pl.BlockSpec((tb, HEAD_DIM), lambda h, i: (i, h)),
        out_shape=jax.ShapeDtypeStruct((t, GROUP_W), BF16),
        scratch_shapes=[pltpu.VMEM((2 * tb, HEAD_DIM), F32), pltpu.VMEM((2 * tb, HEAD_DIM), F32),
                        pltpu.VMEM((2 * tb, HEAD_DIM), F32)],
        compiler_params=_cparams(("parallel", "arbitrary")),
        name="diff_prompt",
    )(qb, kb, vb, proj, *[p.reshape(1, DIFF_QK) for p in lam_params], subln_g.reshape(1, HEAD_DIM))


def _diff_decode_kernel(pt_ref, q_ref, kn_ref, vn_ref, g_ref, lq1_ref, lk1_ref, lq2_ref, lk2_ref,
                        sg_ref, *rest, pages_per_step, lam_init, page):
    k_refs = rest[:pages_per_step]
    v_refs = rest[pages_per_step:2 * pages_per_step]
    o_ref, m_ref, l_ref, acc_ref, newk, newv = rest[2 * pages_per_step:]
    c = pl.program_id(1)
    hrows = 2 * PAD_T
    rows = N_HEADS * hrows

    q8 = q_ref[...] * DIFF_SCALE
    q_heads = [_split_components(q8[:, _head_cols(h)]).astype(BF16) for h in range(N_HEADS)]

    @pl.when(c == 0)
    def _():
        m_ref[...] = jnp.full_like(m_ref, -jnp.inf)
        l_ref[...] = jnp.zeros_like(l_ref)
        acc_ref[...] = jnp.zeros_like(acc_ref)

    def scores(head_ks):
        return jnp.concatenate(
            [jnp.concatenate([_qk(q_heads[h], hk(h)) for hk in head_ks], axis=1)
             for h in range(N_HEADS)], axis=0)

    def update(s, head_vs):
        m_prev = m_ref[...]
        m_new = jnp.maximum(m_prev, jnp.max(s, axis=1, keepdims=True))
        alpha = jnp.exp(m_prev - m_new)
        p = jnp.exp(s - m_new)
        l_new = alpha * l_ref[...] + jnp.sum(p, axis=1, keepdims=True)
        p = p.astype(BF16)
        pvs = []
        for h in range(N_HEADS):
            pv = None
            for n, hv in enumerate(head_vs):
                d = jnp.dot(p[h * hrows:(h + 1) * hrows, n * page:(n + 1) * page], hv(h).astype(BF16),
                            preferred_element_type=F32)
                pv = d if pv is None else pv + d
            pvs.append(pv)
        acc_new = alpha * acc_ref[...] + jnp.concatenate(pvs, axis=0)
        return m_new, l_new, acc_new

    m_new, l_new, acc_new = update(
        scores([(lambda h, r=r: r[:, h, :]) for r in k_refs]),
        [(lambda h, r=r: r[:, h, :]) for r in v_refs])
    m_ref[...] = m_new
    l_ref[...] = l_new
    acc_ref[...] = acc_new

    @pl.when(c == pl.num_programs(1) - 1)
    def _():
        newk[...] = jnp.zeros_like(newk)
        newv[...] = jnp.zeros_like(newv)
        newk[0:PAD_T, :] = kn_ref[...]
        newv[0:PAD_T, :] = vn_ref[...]
        tok = _token_of_row((rows, page))
        key = lax.broadcasted_iota(jnp.int32, (rows, page), 1)
        s = jnp.where(key <= tok, scores([lambda h: newk[:, _head_cols(h)]]), -jnp.inf)
        _, l, acc = update(s, [lambda h: newv[:, _head_cols(h)]])
        o = acc / l
        lam = _lambda(lq1_ref, lk1_ref, lq2_ref, lk2_ref, lam_init)
        sg = sg_ref[...]
        g = g_ref[...]
        outs = []
        for h in range(N_HEADS):
            o1 = o[h * hrows:h * hrows + PAD_T]
            o2 = o[h * hrows + PAD_T:(h + 1) * hrows]
            outs.append(_diff_finish(o1, o2, lam, sg, g[:, _head_cols(h)], lam_init))
        o_ref[...] = jnp.concatenate(outs, axis=1)


def diff_decode(q8, kn8, vn8, g8, cache_k, cache_v, page_table, lam_params, subln_g, lam_init,
                layer, pages_per_step):
    bsz = q8.shape[0]
    n_pages = page_table.shape[1]
    page = cache_k.shape[2]
    pps = pages_per_step
    tok_spec = pl.BlockSpec((None, PAD_T, GROUP_W), lambda b, c, pt: (b, 0, 0))
    small = pl.BlockSpec((1, DIFF_QK), lambda b, c, pt: (0, 0))

    def page_spec(n):
        return pl.BlockSpec((None, None, page, N_HEADS, HEAD_DIM),
                            lambda b, c, pt: (layer, pt[b, c * pps + n], 0, 0, 0))

    rows = N_HEADS * 2 * PAD_T
    return pl.pallas_call(
        functools.partial(_diff_decode_kernel, pages_per_step=pps, lam_init=lam_init, page=page),
        grid_spec=pltpu.PrefetchScalarGridSpec(
            num_scalar_prefetch=1,
            grid=(bsz, n_pages // pps),
            in_specs=[tok_spec, tok_spec, tok_spec, tok_spec, small, small, small, small,
                      pl.BlockSpec((1, HEAD_DIM), lambda b, c, pt: (0, 0))]
                     + [page_spec(n) for n in range(pps)] + [page_spec(n) for n in range(pps)],
            out_specs=tok_spec,
            scratch_shapes=[pltpu.VMEM((rows, 1), F32), pltpu.VMEM((rows, 1), F32),
                            pltpu.VMEM((rows, HEAD_DIM), F32),
                            pltpu.VMEM((page, GROUP_W), F32), pltpu.VMEM((page, GROUP_W), F32)]),
        out_shape=jax.ShapeDtypeStruct((bsz, PAD_T, GROUP_W), F32),
        compiler_params=_cparams(("parallel", "arbitrary")),
        name="diff_decode",
    )(page_table, q8, kn8, vn8, g8, *[p.reshape(1, DIFF_QK) for p in lam_params],
      subln_g.reshape(1, HEAD_DIM), *([cache_k] * pps), *([cache_v] * pps))


def _rope(x, pos):
    b, t, _ = x.shape
    half = ROT_DIM // 2
    inv = ROPE_THETA ** (-2.0 * jnp.arange(half, dtype=F32) / ROT_DIM)
    ang = pos.astype(F32)[:, None] * inv[None, :]
    cos = jnp.cos(ang)[None, :, None, :]
    sin = jnp.sin(ang)[None, :, None, :]
    xc = x.reshape(b, t, GROUP_W // DIFF_QK, DIFF_QK)
    x1 = xc[..., :half]
    x2 = xc[..., half:ROT_DIM]
    out = jnp.concatenate([x1 * cos - x2 * sin, x2 * cos + x1 * sin, xc[..., ROT_DIM:]], axis=-1)
    return out.reshape(b, t, GROUP_W)


def _dwconv(xp, w):
    return lax.conv_general_dilated(xp, w[:, None, :], window_strides=(1,), padding='VALID',
                                    dimension_numbers=('NWC', 'WIO', 'NWC'),
                                    feature_group_count=xp.shape[-1])


def _conf_mixer(proj, conf_buf, dw_w, dw_b, ln_g, ln_b):
    a, bb, g = proj[S_CONF_A], proj[S_CONF_B], proj[S_CONF_G]
    h = a * jax.nn.sigmoid(bb)
    hp = jnp.concatenate([conf_buf, h], axis=1)
    ya = _dwconv(hp, dw_w) + dw_b
    mu = jnp.mean(ya, axis=-1, keepdims=True)
    yc = ya - mu
    ya = yc * lax.rsqrt(jnp.mean(yc * yc, axis=-1, keepdims=True) + EPS) * ln_g + ln_b
    ya = jax.nn.silu(ya) * jax.nn.silu(g)
    return ya, hp[:, -(dw_w.shape[0] - 1):]


def _short_conv_mixer(proj, sc_buf, sc_w):
    u = proj[S_SC_C] * proj[S_SC_H]
    up = jnp.concatenate([sc_buf, u], axis=1)
    yd = proj[S_SC_B] * _dwconv(up, sc_w) * jax.nn.silu(proj[S_SC_G])
    return yd, up[:, -(sc_w.shape[0] - 1):]


def _pad_tokens(x):
    return jnp.pad(x, ((0, 0), (0, PAD_T - x.shape[1]), (0, 0)))


def _pick_tile(n, pref):
    t = min(n, pref)
    while n % t:
        t //= 2
    return t


def kernel(x_prompt, x_sample, cache_sb_k, cache_sb_v, cache_diff_k, cache_diff_v, state_conf_conv,
           state_short_conv, page_table, norm_g, w_in, conf_dw_w, conf_dw_b, conf_ln_g, conf_ln_b,
           diff_lq1, diff_lk1, diff_lq2, diff_lk2, diff_subln_g, sc_conv_w, w_out, final_norm_g):
    bp, tp, d = x_prompt.shape
    bs, ts, _ = x_sample.shape
    depth = w_in.shape[0]
    n_pages = page_table.shape[1]
    page = cache_sb_k.shape[2]
    n_past = n_pages * page
    pos_p = jnp.arange(tp, dtype=jnp.int32)
    pos_s = n_past + jnp.arange(ts, dtype=jnp.int32)
    assert bp == 1 and ts <= PAD_T

    w_in_b = w_in.astype(BF16)
    w_out_b = w_out.astype(BF16)

    hp = x_prompt.reshape(bp * tp, d)
    hs = x_sample.reshape(bs * ts, d)
    tm_p = _pick_tile(bp * tp, 1024)
    tm_s = _pick_tile(bs * ts, 128)
    tb_sb = _pick_tile(tp, 256)
    tb_df = _pick_tile(tp, 256)
    pps = _pick_tile(n_pages, 8)

    p_new = [[] for _ in range(6)]
    s_new = [[] for _ in range(6)]
    for l in range(depth):
        lam_init = 0.8 - 0.6 * math.exp(-0.3 * l)
        lam_params = (diff_lq1[l], diff_lk1[l], diff_lq2[l], diff_lk2[l])
        final = l == depth - 1

        proj = in_proj(hp, norm_g[l], w_in_b[l], tm_p)
        proj_b = proj.reshape(N_IN_SLICES, bp, tp, GROUP_W)
        ya, new_conf = _conf_mixer(proj_b, jnp.zeros((bp, conf_dw_w.shape[1] - 1, GROUP_W), F32),
                                   conf_dw_w[l], conf_dw_b[l], conf_ln_g[l], conf_ln_b[l])
        yd, new_sc = _short_conv_mixer(proj_b, jnp.zeros((bp, sc_conv_w.shape[1] - 1, GROUP_W), F32),
                                       sc_conv_w[l])
        dq = _rope(proj_b[S_DF_Q], pos_p)
        dk = _rope(proj_b[S_DF_K], pos_p)
        yb = sb_prompt(proj, tb_sb)
        yc = diff_prompt((dq[0] * DIFF_SCALE).astype(BF16), dk[0].astype(BF16),
                         proj[S_DF_V].astype(BF16), proj, lam_params, diff_subln_g[l], lam_init, tb_df)
        hp = out_proj((ya[0].astype(BF16), yb, yc, yd[0].astype(BF16)), w_out_b[l], hp,
                      final_norm_g, final, _pick_tile(bp * tp, 256))
        heads = lambda a, bb, tt: a.reshape(bb, tt, N_HEADS, HEAD_DIM)
        for n, a in enumerate((heads(proj[S_SB_K], bp, tp), heads(proj[S_SB_V], bp, tp),
                               heads(dk, bp, tp), heads(proj[S_DF_V], bp, tp), new_conf, new_sc)):
            p_new[n].append(a)

        proj = in_proj(hs, norm_g[l], w_in_b[l], tm_s)
        proj_b = proj.reshape(N_IN_SLICES, bs, ts, GROUP_W)
        ya, new_conf = _conf_mixer(proj_b, state_conf_conv[l], conf_dw_w[l], conf_dw_b[l],
                                   conf_ln_g[l], conf_ln_b[l])
        yd, new_sc = _short_conv_mixer(proj_b, state_short_conv[l], sc_conv_w[l])
        dq = _rope(proj_b[S_DF_Q], pos_s)
        dk = _rope(proj_b[S_DF_K], pos_s)
        yb = sb_decode(_pad_tokens(proj_b[S_SB_Q]), _pad_tokens(proj_b[S_SB_K]),
                       _pad_tokens(proj_b[S_SB_V]), _pad_tokens(proj_b[S_SB_G]),
                       cache_sb_k, cache_sb_v, page_table, l, ts)
        yc = diff_decode(_pad_tokens(dq), _pad_tokens(dk), _pad_tokens(proj_b[S_DF_V]),
                         _pad_tokens(proj_b[S_DF_G]), cache_diff_k, cache_diff_v, page_table,
                         lam_params, diff_subln_g[l], lam_init, l, pps)
        flat = lambda a: a.reshape(bs * ts, GROUP_W).astype(BF16)
        hs = out_proj((flat(ya), flat(yb[:, :ts]), flat(yc[:, :ts]), flat(yd)), w_out_b[l], hs,
                      final_norm_g, final, tm_s)
        for n, a in enumerate((heads(proj[S_SB_K], bs, ts), heads(proj[S_SB_V], bs, ts),
                               heads(dk, bs, ts), heads(proj[S_DF_V], bs, ts), new_conf, new_sc)):
            s_new[n].append(a)

    y_prompt = hp.reshape(bp, tp, d)
    y_sample = hs.reshape(bs, ts, d)
    return (y_prompt, y_sample, *[jnp.stack(a, 0) for a in p_new], *[jnp.stack(a, 0) for a in s_new])
```

```python
import functools
import math

import jax
import jax.numpy as jnp
from jax import lax
from jax.experimental import pallas as pl
from jax.experimental.pallas import tpu as pltpu

F32 = jnp.float32
BF16 = jnp.bfloat16

N_MIXERS = 4
N_HEADS = 4
HEAD_DIM = 128
GROUP_W = N_HEADS * HEAD_DIM
DIFF_QK = HEAD_DIM // 2
ROT_DIM = DIFF_QK // 4
ROPE_THETA = 500000.0
N_IN_SLICES = 15
EPS = 1e-6
LOG2E = 1.4426950408889634
SB_SCALE = 1.0 / math.sqrt(HEAD_DIM)
DIFF_SCALE = 1.0 / math.sqrt(DIFF_QK)
PAD_T = 8
SB_EXIT_LOG = -104.0
VMEM_LIMIT = 56 * 1024 * 1024

(S_CONF_A, S_CONF_B, S_CONF_G, S_SB_Q, S_SB_K, S_SB_V, S_SB_G, S_DF_Q, S_DF_K, S_DF_V, S_DF_G,
 S_SC_H, S_SC_B, S_SC_C, S_SC_G) = range(N_IN_SLICES)


def _silu(x):
    return x * (1.0 / (1.0 + jnp.exp(-x)))


def _cparams(sem):
    return pltpu.CompilerParams(dimension_semantics=sem, vmem_limit_bytes=VMEM_LIMIT)


def _log2(n):
    assert n & (n - 1) == 0
    return n.bit_length() - 1


def _head_cols(h):
    return slice(h * HEAD_DIM, (h + 1) * HEAD_DIM)


def _lane_chunks(x):
    return [x[:, c * HEAD_DIM:(c + 1) * HEAD_DIM] for c in range(x.shape[1] // HEAD_DIM)]


def _tree(op, xs):
    while len(xs) > 1:
        xs = [op(xs[n], xs[n + 1]) if n + 1 < len(xs) else xs[n] for n in range(0, len(xs), 2)]
    return xs[0]


def _qk(q, k):
    return lax.dot_general(q, k.astype(BF16), (((1,), (1,)), ((), ())), preferred_element_type=F32)


def _with_ones(v):
    v = v.astype(BF16)
    return jnp.concatenate([v, jnp.ones_like(v)], axis=1)


def _inproj_kernel(x_ref, g_ref, w_ref, o_ref, xn_ref):
    @pl.when(pl.program_id(1) == 0)
    def _():
        x = x_ref[...]
        ms = jnp.mean(x * x, axis=-1, keepdims=True)
        xn_ref[...] = ((x * lax.rsqrt(ms + EPS)) * g_ref[...]).astype(BF16)

    o_ref[...] = jnp.dot(xn_ref[...], w_ref[...], preferred_element_type=F32)


def in_proj(x2d, g, w_bf16, tm):
    m, d = x2d.shape
    n = w_bf16.shape[1]
    ns = n // GROUP_W
    return pl.pallas_call(
        _inproj_kernel,
        grid=(m // tm, ns),
        in_specs=[pl.BlockSpec((tm, d), lambda i, j: (i, 0)),
                  pl.BlockSpec((1, d), lambda i, j: (0, 0)),
                  pl.BlockSpec((d, GROUP_W), lambda i, j: (0, j))],
        out_specs=pl.BlockSpec((None, tm, GROUP_W), lambda i, j: (j, i, 0)),
        out_shape=jax.ShapeDtypeStruct((ns, m, GROUP_W), F32),
        scratch_shapes=[pltpu.VMEM((tm, d), BF16)],
        compiler_params=_cparams(("parallel", "arbitrary")),
        name="in_proj",
    )(x2d, g.reshape(1, d), w_bf16)


def _outproj_kernel(ya_ref, yb_ref, yc_ref, yd_ref, w_ref, x_ref, fg_ref, o_ref, *, final):
    acc = x_ref[...]
    for m, y_ref in enumerate((ya_ref, yb_ref, yc_ref, yd_ref)):
        acc = acc + jnp.dot(y_ref[...], w_ref[m * GROUP_W:(m + 1) * GROUP_W, :],
                            preferred_element_type=F32)
    if final:
        ms = jnp.mean(acc * acc, axis=-1, keepdims=True)
        acc = (acc * lax.rsqrt(ms + EPS)) * fg_ref[...]
    o_ref[...] = acc


def out_proj(ys, w_bf16, x2d, final_g, final, tm):
    m, d = x2d.shape
    yspec = pl.BlockSpec((tm, GROUP_W), lambda i: (i, 0))
    return pl.pallas_call(
        functools.partial(_outproj_kernel, final=final),
        grid=(m // tm,),
        in_specs=[yspec, yspec, yspec, yspec,
                  pl.BlockSpec((N_MIXERS * GROUP_W, d), lambda i: (0, 0)),
                  pl.BlockSpec((tm, d), lambda i: (i, 0)),
                  pl.BlockSpec((1, d), lambda i: (0, 0))],
        out_specs=pl.BlockSpec((tm, d), lambda i: (i, 0)),
        out_shape=jax.ShapeDtypeStruct((m, d), F32),
        compiler_params=_cparams(("parallel",)),
        name="out_proj",
    )(*ys, w_bf16, x2d, final_g.reshape(1, d))


def _sb_terms(z):
    t = jnp.log1p(jnp.exp(-jnp.abs(z)))
    return jnp.minimum(z, 0.0) - t, -jnp.maximum(z, 0.0) - t


def _suffix_sums(lk, u):
    hi = lk.astype(BF16)
    lo = (lk - hi.astype(F32)).astype(BF16)
    return (jnp.dot(hi, u, preferred_element_type=F32) + jnp.dot(lo, u, preferred_element_type=F32))


def _strict_lower(n):
    row = lax.broadcasted_iota(jnp.int32, (n, n), 0)
    col = lax.broadcasted_iota(jnp.int32, (n, n), 1)
    return row, col


def _sb_prompt_kernel(q_ref, k_ref, v_ref, g_ref, o_ref, acc_ref, carry_ref, *, tb):
    i = pl.program_id(1)
    q = q_ref[...].astype(BF16)
    row, col = _strict_lower(tb)
    u = jnp.where(row > col, 1.0, 0.0).astype(BF16)
    visible = col < row

    def scores(jb):
        start = pl.multiple_of(jb * tb, tb)
        z = _qk(q, k_ref[pl.ds(start, tb), :]) * SB_SCALE
        return z, v_ref[pl.ds(start, tb), :].astype(BF16)

    z, v = scores(i)
    ls, lk = _sb_terms(z)
    lk = jnp.where(visible, lk, 0.0)
    a = jnp.where(visible, jnp.exp(ls + _suffix_sums(lk, u)), 0.0)
    acc_ref[...] = jnp.dot(a.astype(BF16), v, preferred_element_type=F32)
    carry_ref[...] = jnp.sum(lk, axis=1, keepdims=True)

    def cond(state):
        jb, go = state
        return jnp.logical_and(jb >= 0, go)

    def body(state):
        jb, _ = state
        z, v = scores(jb)
        ls, lk = _sb_terms(z)
        carry = carry_ref[...]
        a = jnp.exp(ls + _suffix_sums(lk, u) + carry)
        acc_ref[...] += jnp.dot(a.astype(BF16), v, preferred_element_type=F32)
        carry = carry + jnp.sum(lk, axis=1, keepdims=True)
        carry_ref[...] = carry
        return jb - 1, jnp.max(carry) > SB_EXIT_LOG

    lax.while_loop(cond, body, (i - 1, jnp.max(carry_ref[...]) > SB_EXIT_LOG))
    o_ref[...] = (acc_ref[...] * _silu(g_ref[...])).astype(o_ref.dtype)


def sb_prompt(proj, tb):
    _, t, _ = proj.shape

    def spec(rows, s, per_q):
        if per_q:
            return pl.BlockSpec((None, rows, HEAD_DIM), lambda h, i: (s, i, h))
        return pl.BlockSpec((None, rows, HEAD_DIM), lambda h, i: (s, 0, h))

    return pl.pallas_call(
        functools.partial(_sb_prompt_kernel, tb=tb),
        grid=(N_HEADS, t // tb),
        in_specs=[spec(tb, S_SB_Q, True), spec(t, S_SB_K, False), spec(t, S_SB_V, False),
                  spec(tb, S_SB_G, True)],
        out_specs=pl.BlockSpec((tb, HEAD_DIM), lambda h, i: (i, h)),
        out_shape=jax.ShapeDtypeStruct((t, GROUP_W), BF16),
        scratch_shapes=[pltpu.VMEM((tb, HEAD_DIM), F32), pltpu.VMEM((tb, 1), F32)],
        compiler_params=_cparams(("parallel", "arbitrary")),
        name="sb_prompt",
    )(proj, proj, proj, proj)


def _page_masks(rows, rows_per_head, page_rows):
    row = lax.broadcasted_iota(jnp.int32, (rows, page_rows), 0)
    col = lax.broadcasted_iota(jnp.int32, (rows, page_rows), 1)
    own = (col & (N_HEADS - 1)) == (row >> _log2(rows_per_head))
    return own, col >> _log2(N_HEADS), row & (PAD_T - 1)


def _sb_decode_kernel(pt_ref, q_ref, kn_ref, vn_ref, g_ref, u_ref, ck_ref, cv_ref, o_ref,
                      kbuf, vbuf, newk, newv, sem, acc_ref, carry_ref, *, layer, n_pages, n_tok):
    b = pl.program_id(0)
    rows = N_HEADS * PAD_T
    page_rows = newk.shape[0]

    def page_copies(p, slot):
        pg = pt_ref[b, p]
        return (pltpu.make_async_copy(ck_ref.at[layer, pg], kbuf.at[slot], sem.at[0, slot]),
                pltpu.make_async_copy(cv_ref.at[layer, pg], vbuf.at[slot], sem.at[1, slot]))

    def start_fetch(p, slot):
        for c in page_copies(p, slot):
            c.start()

    def wait_fetch(p, slot):
        for c in page_copies(p, slot):
            c.wait()

    start_fetch(n_pages - 1, 0)

    q8 = q_ref[...]
    qall = jnp.concatenate([q8[:, _head_cols(h)] for h in range(N_HEADS)], axis=0).astype(BF16)
    own, key, tok = _page_masks(rows, PAD_T, page_rows)
    real_row = (lax.broadcasted_iota(jnp.int32, (rows, 1), 0) & (PAD_T - 1)) < n_tok
    u = u_ref[...]

    def attend(k, v, visible, carry):
        ls, lk = _sb_terms(_qk(qall, k) * SB_SCALE)
        lk = jnp.where(visible, lk, 0.0)
        e = ls + _suffix_sums(lk, u)
        if carry is not None:
            e = e + carry
        a = jnp.where(visible, jnp.exp(e), 0.0).astype(BF16)
        return (jnp.dot(a, v.astype(BF16), preferred_element_type=F32),
                jnp.sum(lk, axis=1, keepdims=True))

    def live(carry):
        return jnp.max(jnp.where(real_row, carry, -jnp.inf)) > SB_EXIT_LOG

    newk[...] = jnp.zeros_like(newk)
    newv[...] = jnp.zeros_like(newv)
    newk[0:PAD_T * N_HEADS, :] = kn_ref[...]
    newv[0:PAD_T * N_HEADS, :] = vn_ref[...]
    pv, dc = attend(newk[...], newv[...], jnp.logical_and(own, key < tok), None)
    acc_ref[...] = pv
    carry_ref[...] = dc

    def cond(state):
        p, go = state
        return jnp.logical_and(p >= 0, go)

    def body(state):
        p, _ = state
        slot = (n_pages - 1 - p) % 2
        wait_fetch(p, slot)

        @pl.when(p >= 1)
        def _():
            start_fetch(p - 1, 1 - slot)

        carry = carry_ref[...]
        pv, dc = attend(kbuf[slot], vbuf[slot], own, carry)
        acc_ref[...] += pv
        carry = carry + dc
        carry_ref[...] = carry
        return p - 1, live(carry)

    p_end, _ = lax.while_loop(cond, body, (jnp.int32(n_pages - 1), live(carry_ref[...])))

    @pl.when(p_end >= 0)
    def _():
        wait_fetch(p_end, (n_pages - 1 - p_end) % 2)

    acc = acc_ref[...]
    out = jnp.concatenate([acc[h * PAD_T:(h + 1) * PAD_T] for h in range(N_HEADS)], axis=1)
    o_ref[...] = out * _silu(g_ref[...])


def sb_decode(q8, kn, vn, g8, cache_k, cache_v, page_table, layer, n_tok):
    bsz = q8.shape[0]
    n_pages = page_table.shape[1]
    page_rows = cache_k.shape[2]
    tok_spec = pl.BlockSpec((None, PAD_T, GROUP_W), lambda b, pt: (b, 0, 0))
    new_spec = pl.BlockSpec((None, PAD_T * N_HEADS, HEAD_DIM), lambda b, pt: (b, 0, 0))
    any_spec = pl.BlockSpec(memory_space=pl.ANY)
    rows = N_HEADS * PAD_T
    row, col = _strict_lower(page_rows)
    u = jnp.where(row > col, 1.0, 0.0).astype(BF16)
    page_buf = pltpu.VMEM((2, page_rows, HEAD_DIM), F32)
    new_buf = pltpu.VMEM((page_rows, HEAD_DIM), F32)
    return pl.pallas_call(
        functools.partial(_sb_decode_kernel, layer=layer, n_pages=n_pages, n_tok=n_tok),
        grid_spec=pltpu.PrefetchScalarGridSpec(
            num_scalar_prefetch=1,
            grid=(bsz,),
            in_specs=[tok_spec, new_spec, new_spec, tok_spec,
                      pl.BlockSpec((page_rows, page_rows), lambda b, pt: (0, 0)), any_spec, any_spec],
            out_specs=tok_spec,
            scratch_shapes=[page_buf, page_buf, new_buf, new_buf,
                            pltpu.SemaphoreType.DMA((2, 2)),
                            pltpu.VMEM((rows, HEAD_DIM), F32), pltpu.VMEM((rows, 1), F32)]),
        out_shape=jax.ShapeDtypeStruct((bsz, PAD_T, GROUP_W), F32),
        compiler_params=_cparams(("arbitrary",)),
        name="sb_decode",
    )(page_table, q8, kn, vn, g8, u, cache_k, cache_v)


def _lambda(lq1_ref, lk1_ref, lq2_ref, lk2_ref, lam_init):
    s1 = jnp.sum(lq1_ref[...] * lk1_ref[...], axis=-1, keepdims=True)
    s2 = jnp.sum(lq2_ref[...] * lk2_ref[...], axis=-1, keepdims=True)
    return jnp.exp(s1) - jnp.exp(s2) + lam_init


def _diff_finish(o1, o2, lam, sg, gate, lam_init):
    w = o1 - lam * o2
    ms = jnp.mean(w * w, axis=-1, keepdims=True)
    y = (w * lax.rsqrt(ms + EPS)) * sg
    return (y * (1.0 - lam_init)) * _silu(gate)


def _split_components(q):
    lane = lax.broadcasted_iota(jnp.int32, q.shape, 1)
    zero = jnp.zeros_like(q)
    return jnp.concatenate([jnp.where(lane < DIFF_QK, q, zero), jnp.where(lane >= DIFF_QK, q, zero)],
                           axis=0)


def _loop_tiles(n, fn, unroll):
    def group(g, c):
        for k in range(unroll):
            fn(g * unroll + k)
        return c

    n_groups = n // unroll
    lax.fori_loop(0, n_groups, group, 0)
    lax.fori_loop(n_groups * unroll, n, lambda j, c: (fn(j), c)[1], 0)


def _diff_prompt_kernel(q_ref, k_ref, v_ref, g_ref, lq1_ref, lk1_ref, lq2_ref, lk2_ref, sg_ref,
                        o_ref, m_ref, acc_ref, *, tb, lam_init, unroll):
    i = pl.program_id(1)
    qq = _split_components(q_ref[...])
    m_ref[...] = jnp.full_like(m_ref, -jnp.inf)
    acc_ref[...] = jnp.zeros_like(acc_ref)

    def scores(jb, masked):
        start = pl.multiple_of(jb * tb, tb)
        s = _qk(qq, k_ref[pl.ds(start, tb), :])
        if masked:
            row = lax.broadcasted_iota(jnp.int32, s.shape, 0) & (tb - 1)
            col = lax.broadcasted_iota(jnp.int32, s.shape, 1)
            s = jnp.where(col <= row, s, -jnp.inf)
        return s

    def sweep_max(jb, masked):
        m_ref[...] = jnp.maximum(m_ref[...], _tree(jnp.maximum, _lane_chunks(scores(jb, masked))))

    def sweep_pv(jb, masked):
        m = m_ref[...]
        p = jnp.concatenate([jnp.exp2(c - m) for c in _lane_chunks(scores(jb, masked))], axis=1)
        start = pl.multiple_of(jb * tb, tb)
        acc_ref[...] += jnp.dot(p.astype(BF16), _with_ones(v_ref[pl.ds(start, tb), :]),
                                preferred_element_type=F32)

    _loop_tiles(i, lambda jb: sweep_max(jb, False), unroll)
    sweep_max(i, True)
    m_ref[...] = jnp.broadcast_to(jnp.max(m_ref[...], axis=1, keepdims=True), m_ref.shape)
    _loop_tiles(i, lambda jb: sweep_pv(jb, False), unroll)
    sweep_pv(i, True)

    acc = acc_ref[...]
    o = acc[:, :HEAD_DIM] / acc[:, HEAD_DIM:]
    lam = _lambda(lq1_ref, lk1_ref, lq2_ref, lk2_ref, lam_init)
    o_ref[...] = _diff_finish(o[:tb], o[tb:], lam, sg_ref[...], g_ref[...], lam_init).astype(o_ref.dtype)


def diff_prompt(qb, kb, vb, proj, lam_params, subln_g, lam_init, tb, unroll):
    t = qb.shape[0]
    small = pl.BlockSpec((1, DIFF_QK), lambda h, i: (0, 0))
    return pl.pallas_call(
        functools.partial(_diff_prompt_kernel, tb=tb, lam_init=lam_init, unroll=unroll),
        grid=(N_HEADS, t // tb),
        in_specs=[pl.BlockSpec((tb, HEAD_DIM), lambda h, i: (i, h)),
                  pl.BlockSpec((t, HEAD_DIM), lambda h, i: (0, h)),
                  pl.BlockSpec((t, HEAD_DIM), lambda h, i: (0, h)),
                  pl.BlockSpec((None, tb, HEAD_DIM), lambda h, i: (S_DF_G, i, h)),
                  small, small, small, small,
                  pl.BlockSpec((1, HEAD_DIM), lambda h, i: (0, 0))],
        out_specs=pl.BlockSpec((tb, HEAD_DIM), lambda h, i: (i, h)),
        out_shape=jax.ShapeDtypeStruct((t, GROUP_W), BF16),
        scratch_shapes=[pltpu.VMEM((2 * tb, HEAD_DIM), F32), pltpu.VMEM((2 * tb, 2 * HEAD_DIM), F32)],
        compiler_params=_cparams(("parallel", "arbitrary")),
        name="diff_prompt",
    )(qb, kb, vb, proj, *[p.reshape(1, DIFF_QK) for p in lam_params], subln_g.reshape(1, HEAD_DIM))


def _diff_decode_kernel(pt_ref, q_ref, kn_ref, vn_ref, g_ref, lq1_ref, lk1_ref, lq2_ref, lk2_ref,
                        sg_ref, *rest, pages_per_step, group, lam_init):
    k_refs = rest[:pages_per_step]
    v_refs = rest[pages_per_step:2 * pages_per_step]
    o_ref, m_ref, acc_ref, newk, newv = rest[2 * pages_per_step:]
    c = pl.program_id(1)
    hrows = 2 * PAD_T
    rows = N_HEADS * hrows
    page_rows = newk.shape[0]

    q8 = q_ref[...] * (DIFF_SCALE * LOG2E)
    qall = jnp.concatenate([_split_components(q8[:, _head_cols(h)]) for h in range(N_HEADS)],
                           axis=0).astype(BF16)
    zq = jnp.zeros_like(qall)
    qpair = jnp.concatenate([jnp.concatenate([qall, zq], axis=1),
                             jnp.concatenate([zq, qall], axis=1)], axis=0)
    own, key, tok = _page_masks(rows, hrows, page_rows)

    @pl.when(c == 0)
    def _():
        m_ref[...] = jnp.full_like(m_ref, -jnp.inf)
        acc_ref[...] = jnp.zeros_like(acc_ref)

    def page_scores(ks):
        out = []
        for n in range(0, len(ks) - 1, 2):
            kk = jnp.concatenate([ks[n].astype(BF16), ks[n + 1].astype(BF16)], axis=1)
            s2 = _qk(qpair, kk)
            out += [s2[:rows], s2[rows:]]
        if len(ks) % 2:
            out.append(_qk(qall, ks[-1]))
        return out

    def update(ks, vs, visible):
        s = jnp.concatenate([jnp.where(visible, z, -jnp.inf) for z in page_scores(ks)], axis=1)
        m_prev = m_ref[...]
        m_new = jnp.maximum(m_prev, jnp.max(s, axis=1, keepdims=True))
        p = jnp.exp2(s - m_new).astype(BF16)
        pv = _tree(jnp.add, [jnp.dot(p[:, n * page_rows:(n + 1) * page_rows], _with_ones(v),
                                     preferred_element_type=F32) for n, v in enumerate(vs)])
        return m_new, jnp.exp2(m_prev - m_new) * acc_ref[...] + pv

    for g0 in range(0, pages_per_step, group):
        m_new, acc_new = update([r[...] for r in k_refs[g0:g0 + group]],
                                [r[...] for r in v_refs[g0:g0 + group]], own)
        m_ref[...] = m_new
        acc_ref[...] = acc_new

    @pl.when(c == pl.num_programs(1) - 1)
    def _():
        newk[...] = jnp.zeros_like(newk)
        newv[...] = jnp.zeros_like(newv)
        newk[0:PAD_T * N_HEADS, :] = kn_ref[...]
        newv[0:PAD_T * N_HEADS, :] = vn_ref[...]
        _, acc = update([newk[...]], [newv[...]], jnp.logical_and(own, key <= tok))
        o = acc[:, :HEAD_DIM] / acc[:, HEAD_DIM:]
        lam = _lambda(lq1_ref, lk1_ref, lq2_ref, lk2_ref, lam_init)
        sg = sg_ref[...]
        g = g_ref[...]
        outs = []
        for h in range(N_HEADS):
            o1 = o[h * hrows:h * hrows + PAD_T]
            o2 = o[h * hrows + PAD_T:(h + 1) * hrows]
            outs.append(_diff_finish(o1, o2, lam, sg, g[:, _head_cols(h)], lam_init))
        o_ref[...] = jnp.concatenate(outs, axis=1)


def diff_decode(q8, kn, vn, g8, cache_k, cache_v, page_table, lam_params, subln_g, lam_init,
                layer, pages_per_step, group):
    bsz = q8.shape[0]
    n_pages = page_table.shape[1]
    page_rows = cache_k.shape[2]
    pps = pages_per_step
    tok_spec = pl.BlockSpec((None, PAD_T, GROUP_W), lambda b, c, pt: (b, 0, 0))
    new_spec = pl.BlockSpec((None, PAD_T * N_HEADS, HEAD_DIM), lambda b, c, pt: (b, 0, 0))
    small = pl.BlockSpec((1, DIFF_QK), lambda b, c, pt: (0, 0))

    def page_spec(n):
        return pl.BlockSpec((None, None, page_rows, HEAD_DIM),
                            lambda b, c, pt: (layer, pt[b, c * pps + n], 0, 0))

    rows = N_HEADS * 2 * PAD_T
    new_buf = pltpu.VMEM((page_rows, HEAD_DIM), F32)
    return pl.pallas_call(
        functools.partial(_diff_decode_kernel, pages_per_step=pps, group=group, lam_init=lam_init),
        grid_spec=pltpu.PrefetchScalarGridSpec(
            num_scalar_prefetch=1,
            grid=(bsz, n_pages // pps),
            in_specs=[tok_spec, new_spec, new_spec, tok_spec, small, small, small, small,
                      pl.BlockSpec((1, HEAD_DIM), lambda b, c, pt: (0, 0))]
                     + [page_spec(n) for n in range(pps)] + [page_spec(n) for n in range(pps)],
            out_specs=tok_spec,
            scratch_shapes=[pltpu.VMEM((rows, 1), F32), pltpu.VMEM((rows, 2 * HEAD_DIM), F32),
                            new_buf, new_buf]),
        out_shape=jax.ShapeDtypeStruct((bsz, PAD_T, GROUP_W), F32),
        compiler_params=_cparams(("parallel", "arbitrary")),
        name="diff_decode",
    )(page_table, q8, kn, vn, g8, *[p.reshape(1, DIFF_QK) for p in lam_params],
      subln_g.reshape(1, HEAD_DIM), *([cache_k] * pps), *([cache_v] * pps))


CONF_HALO = 32
SC_HALO = 8
SUBLANES = 8
MIX_ROWS = 32


def _rope_tables(pos):
    half = ROT_DIM // 2
    inv = ROPE_THETA ** (-2.0 * jnp.arange(half, dtype=F32) / ROT_DIM)
    ang = pos.astype(F32)[:, None] * inv[None, :]
    cos, sin = jnp.cos(ang), jnp.sin(ang)
    j = jnp.arange(HEAD_DIM) % DIFF_QK
    f = j % half
    c = jnp.where(j < ROT_DIM, cos[:, f], 1.0)
    sa = jnp.where((j >= half) & (j < ROT_DIM), sin[:, f], 0.0)
    sb = jnp.where(j < half, -sin[:, f], 0.0)
    return c, sa, sb


def _rope_rows(x, c, sa, sb):
    half = ROT_DIM // 2
    out = []
    for xh in _lane_chunks(x):
        out.append(xh * c + pltpu.roll(xh, half, 1) * sa + pltpu.roll(xh, HEAD_DIM - half, 1) * sb)
    return jnp.concatenate(out, axis=1)


def _prompt_mix_kernel(a_ref, b_ref, g_ref, dq_ref, dk_ref, dv_ref, sh_ref, sb_ref, sc_ref, sg_ref,
                       cw_ref, cb_ref, lg_ref, lb_ref, sw_ref, rc_ref, rsa_ref, rsb_ref,
                       ya_ref, yd_ref, dqb_ref, dk_out_ref, dkb_ref, dvb_ref, nconf_ref, nsc_ref,
                       hb, hs, ub, *, tb, conf_w, sc_w):
    i = pl.program_id(0)

    @pl.when(i == 0)
    def _():
        hb[0:CONF_HALO, :] = jnp.zeros((CONF_HALO, GROUP_W), F32)
        ub[0:SC_HALO, :] = jnp.zeros((SC_HALO, GROUP_W), F32)

    @pl.when(i > 0)
    def _():
        hb[0:CONF_HALO, :] = hb[tb:tb + CONF_HALO, :]
        ub[0:SC_HALO, :] = ub[tb:tb + SC_HALO, :]

    hb[CONF_HALO:CONF_HALO + tb, :] = a_ref[...] * (1.0 / (1.0 + jnp.exp(-b_ref[...])))
    ub[SC_HALO:SC_HALO + tb, :] = sc_ref[...] * sh_ref[...]

    off = CONF_HALO - (conf_w - 1)
    span = hs.shape[1]
    for k in range(1, SUBLANES):
        hs[k - 1, :, :] = hb[k:k + span, :]

    def tap(w_ref, w):
        return jnp.concatenate([w_ref[w]] * (MIX_ROWS // SUBLANES), axis=0)

    def window(r):
        k, base = r % SUBLANES, r - r % SUBLANES
        if k == 0:
            return hb[base:base + MIX_ROWS, :]
        return hs[k - 1, base:base + MIX_ROWS, :]

    for r0 in range(0, tb, MIX_ROWS):
        rows = slice(r0, r0 + MIX_ROWS)
        acc = jnp.zeros((MIX_ROWS, GROUP_W), F32) + cb_ref[...]
        for w in range(conf_w):
            acc = acc + tap(cw_ref, w) * window(r0 + off + w)
        mu = jnp.mean(acc, axis=-1, keepdims=True)
        xc = acc - mu
        y = (xc * lax.rsqrt(jnp.mean(xc * xc, axis=-1, keepdims=True) + EPS)) * lg_ref[...] + lb_ref[...]
        ya_ref[rows, :] = (_silu(y) * _silu(g_ref[rows, :])).astype(ya_ref.dtype)
        u0 = r0 + SC_HALO - (sc_w - 1)
        conv = jnp.zeros((MIX_ROWS, GROUP_W), F32)
        for w in range(sc_w):
            conv = conv + tap(sw_ref, w) * ub[u0 + w:u0 + w + MIX_ROWS, :]
        yd_ref[rows, :] = ((sb_ref[rows, :] * conv) * _silu(sg_ref[rows, :])).astype(yd_ref.dtype)

    c, sa, sb = rc_ref[...], rsa_ref[...], rsb_ref[...]
    dqb_ref[...] = (_rope_rows(dq_ref[...], c, sa, sb) * (DIFF_SCALE * LOG2E)).astype(dqb_ref.dtype)
    dk = _rope_rows(dk_ref[...], c, sa, sb)
    dk_out_ref[...] = dk
    dkb_ref[...] = dk.astype(dkb_ref.dtype)
    dvb_ref[...] = dv_ref[...].astype(dvb_ref.dtype)

    @pl.when(i == pl.num_programs(0) - 1)
    def _():
        nconf_ref[...] = hb[CONF_HALO + tb - (conf_w - 1):CONF_HALO + tb, :]
        nsc_ref[...] = ub[SC_HALO + tb - (sc_w - 1):SC_HALO + tb, :]


def prompt_mix(proj, conf_w, conf_b, ln_g, ln_b, sc_w, rope_tabs, tb):
    _, t, _ = proj.shape
    cw, scw = conf_w.shape[0], sc_w.shape[0]

    def slab(s):
        return pl.BlockSpec((None, tb, GROUP_W), lambda i: (s, i, 0))

    def full(r):
        return pl.BlockSpec((r, GROUP_W), lambda i: (0, 0))

    def taps(n):
        return pl.BlockSpec((n, SUBLANES, GROUP_W), lambda i: (0, 0, 0))

    def replicate(w):
        return jnp.broadcast_to(w[:, None, :], (w.shape[0], SUBLANES, w.shape[1]))

    row_blk = pl.BlockSpec((tb, GROUP_W), lambda i: (i, 0))
    tab_blk = pl.BlockSpec((tb, HEAD_DIM), lambda i: (i, 0))
    slabs = (S_CONF_A, S_CONF_B, S_CONF_G, S_DF_Q, S_DF_K, S_DF_V, S_SC_H, S_SC_B, S_SC_C, S_SC_G)
    act = jax.ShapeDtypeStruct((t, GROUP_W), BF16)
    return pl.pallas_call(
        functools.partial(_prompt_mix_kernel, tb=tb, conf_w=cw, sc_w=scw),
        grid=(t // tb,),
        in_specs=[slab(s) for s in slabs]
                 + [taps(cw), full(1), full(1), full(1), taps(scw), tab_blk, tab_blk, tab_blk],
        out_specs=[row_blk, row_blk, row_blk, row_blk, row_blk, row_blk, full(cw - 1), full(scw - 1)],
        out_shape=[act, act, act, jax.ShapeDtypeStruct((t, GROUP_W), F32), act, act,
                   jax.ShapeDtypeStruct((cw - 1, GROUP_W), F32),
                   jax.ShapeDtypeStruct((scw - 1, GROUP_W), F32)],
        scratch_shapes=[pltpu.VMEM((CONF_HALO + tb, GROUP_W), F32),
                        pltpu.VMEM((SUBLANES - 1, CONF_HALO + tb - SUBLANES, GROUP_W), F32),
                        pltpu.VMEM((SC_HALO + tb, GROUP_W), F32)],
        compiler_params=_cparams(("arbitrary",)),
        name="prompt_mix",
    )(*([proj] * len(slabs)), replicate(conf_w), conf_b.reshape(1, -1), ln_g.reshape(1, -1),
      ln_b.reshape(1, -1), replicate(sc_w), *rope_tabs)


def _rope(x, pos):
    b, t, _ = x.shape
    half = ROT_DIM // 2
    inv = ROPE_THETA ** (-2.0 * jnp.arange(half, dtype=F32) / ROT_DIM)
    ang = pos.astype(F32)[:, None] * inv[None, :]
    cos = jnp.cos(ang)[None, :, None, :]
    sin = jnp.sin(ang)[None, :, None, :]
    xc = x.reshape(b, t, GROUP_W // DIFF_QK, DIFF_QK)
    x1 = xc[..., :half]
    x2 = xc[..., half:ROT_DIM]
    out = jnp.concatenate([x1 * cos - x2 * sin, x2 * cos + x1 * sin, xc[..., ROT_DIM:]], axis=-1)
    return out.reshape(b, t, GROUP_W)


def _dwconv(xp, w):
    return lax.conv_general_dilated(xp, w[:, None, :], window_strides=(1,), padding='VALID',
                                    dimension_numbers=('NWC', 'WIO', 'NWC'),
                                    feature_group_count=xp.shape[-1])


def _conf_mixer(proj, conf_buf, dw_w, dw_b, ln_g, ln_b):
    a, bb, g = proj[S_CONF_A], proj[S_CONF_B], proj[S_CONF_G]
    h = a * jax.nn.sigmoid(bb)
    hp = jnp.concatenate([conf_buf, h], axis=1)
    ya = _dwconv(hp, dw_w) + dw_b
    mu = jnp.mean(ya, axis=-1, keepdims=True)
    yc = ya - mu
    ya = yc * lax.rsqrt(jnp.mean(yc * yc, axis=-1, keepdims=True) + EPS) * ln_g + ln_b
    ya = jax.nn.silu(ya) * jax.nn.silu(g)
    return ya, hp[:, -(dw_w.shape[0] - 1):]


def _short_conv_mixer(proj, sc_buf, sc_w):
    u = proj[S_SC_C] * proj[S_SC_H]
    up = jnp.concatenate([sc_buf, u], axis=1)
    yd = proj[S_SC_B] * _dwconv(up, sc_w) * jax.nn.silu(proj[S_SC_G])
    return yd, up[:, -(sc_w.shape[0] - 1):]


def _pad_tokens(x):
    return jnp.pad(x, ((0, 0), (0, PAD_T - x.shape[1]), (0, 0)))


def _token_head_rows(x):
    return _pad_tokens(x).reshape(x.shape[0], PAD_T * N_HEADS, HEAD_DIM)


def _pick_tile(n, pref):
    t = min(n, pref)
    while n % t:
        t //= 2
    return t


def kernel(x_prompt, x_sample, cache_sb_k, cache_sb_v, cache_diff_k, cache_diff_v, state_conf_conv,
           state_short_conv, page_table, norm_g, w_in, conf_dw_w, conf_dw_b, conf_ln_g, conf_ln_b,
           diff_lq1, diff_lk1, diff_lq2, diff_lk2, diff_subln_g, sc_conv_w, w_out, final_norm_g):
    bp, tp, d = x_prompt.shape
    bs, ts, _ = x_sample.shape
    depth = w_in.shape[0]
    n_pages = page_table.shape[1]
    page = cache_sb_k.shape[2]
    n_past = n_pages * page
    pos_p = jnp.arange(tp, dtype=jnp.int32)
    pos_s = n_past + jnp.arange(ts, dtype=jnp.int32)
    assert bp == 1 and ts <= PAD_T

    caches = [c.reshape(c.shape[0], c.shape[1], page * N_HEADS, HEAD_DIM)
              for c in (cache_sb_k, cache_sb_v, cache_diff_k, cache_diff_v)]
    w_in_b = w_in.astype(BF16)
    w_out_b = w_out.astype(BF16)

    hp = x_prompt.reshape(bp * tp, d)
    hs = x_sample.reshape(bs * ts, d)
    tm_p = _pick_tile(bp * tp, 1024)
    tm_s = _pick_tile(bs * ts, 128)
    tb_sb = _pick_tile(tp, 256)
    tb_df = _pick_tile(tp, 512)
    tb_mix = _pick_tile(tp, 256)
    assert tb_mix >= CONF_HALO >= conf_dw_w.shape[1] - 1 and SC_HALO >= sc_conv_w.shape[1] - 1
    rope_tabs_p = _rope_tables(pos_p)
    pps = _pick_tile(n_pages, 16)
    pgroup = _pick_tile(pps, 2)

    p_new = [[] for _ in range(6)]
    s_new = [[] for _ in range(6)]
    for l in range(depth):
        lam_init = 0.8 - 0.6 * math.exp(-0.3 * l)
        lam_params = (diff_lq1[l], diff_lk1[l], diff_lq2[l], diff_lk2[l])
        final = l == depth - 1

        proj = in_proj(hp, norm_g[l], w_in_b[l], tm_p)
        ya, yd, dqb, dk, dkb, dvb, new_conf, new_sc = prompt_mix(
            proj, conf_dw_w[l], conf_dw_b[l], conf_ln_g[l], conf_ln_b[l], sc_conv_w[l], rope_tabs_p,
            tb_mix)
        yb = sb_prompt(proj, tb_sb)
        yc = diff_prompt(dqb, dkb, dvb, proj, lam_params, diff_subln_g[l], lam_init, tb_df, 2)
        hp = out_proj((ya, yb, yc, yd), w_out_b[l], hp, final_norm_g, final,
                      _pick_tile(bp * tp, 256))
        heads = lambda a, bb, tt: a.reshape(bb, tt, N_HEADS, HEAD_DIM)
        for n, a in enumerate((heads(proj[S_SB_K], bp, tp), heads(proj[S_SB_V], bp, tp),
                               heads(dk, bp, tp), heads(proj[S_DF_V], bp, tp), new_conf[None],
                               new_sc[None])):
            p_new[n].append(a)

        proj = in_proj(hs, norm_g[l], w_in_b[l], tm_s)
        proj_b = proj.reshape(N_IN_SLICES, bs, ts, GROUP_W)
        ya, new_conf = _conf_mixer(proj_b, state_conf_conv[l], conf_dw_w[l], conf_dw_b[l],
                                   conf_ln_g[l], conf_ln_b[l])
        yd, new_sc = _short_conv_mixer(proj_b, state_short_conv[l], sc_conv_w[l])
        dq = _rope(proj_b[S_DF_Q], pos_s)
        dk = _rope(proj_b[S_DF_K], pos_s)
        yb = sb_decode(_pad_tokens(proj_b[S_SB_Q]), _token_head_rows(proj_b[S_SB_K]),
                       _token_head_rows(proj_b[S_SB_V]), _pad_tokens(proj_b[S_SB_G]),
                       caches[0], caches[1], page_table, l, ts)
        yc = diff_decode(_pad_tokens(dq), _token_head_rows(dk), _token_head_rows(proj_b[S_DF_V]),
                         _pad_tokens(proj_b[S_DF_G]), caches[2], caches[3], page_table,
                         lam_params, diff_subln_g[l], lam_init, l, pps, pgroup)
        flat = lambda a: a.reshape(bs * ts, GROUP_W).astype(BF16)
        hs = out_proj((flat(ya), flat(yb[:, :ts]), flat(yc[:, :ts]), flat(yd)), w_out_b[l], hs,
                      final_norm_g, final, tm_s)
        for n, a in enumerate((heads(proj[S_SB_K], bs, ts), heads(proj[S_SB_V], bs, ts),
                               heads(dk, bs, ts), heads(proj[S_DF_V], bs, ts), new_conf, new_sc)):
            s_new[n].append(a)

    y_prompt = hp.reshape(bp, tp, d)
    y_sample = hs.reshape(bs, ts, d)
    return (y_prompt, y_sample, *[jnp.stack(a, 0) for a in p_new], *[jnp.stack(a, 0) for a in s_new])
```

```python
import functools
import math

import jax
import jax.numpy as jnp
from jax import lax
from jax.experimental import pallas as pl
from jax.experimental.pallas import tpu as pltpu

F32 = jnp.float32
BF16 = jnp.bfloat16

N_MIXERS = 4
N_HEADS = 4
HEAD_DIM = 128
GROUP_W = N_HEADS * HEAD_DIM
DIFF_QK = HEAD_DIM // 2
ROT_DIM = DIFF_QK // 4
ROPE_THETA = 500000.0
N_IN_SLICES = 15
EPS = 1e-6
LOG2E = 1.4426950408889634
SB_SCALE = 1.0 / math.sqrt(HEAD_DIM)
DIFF_SCALE = 1.0 / math.sqrt(DIFF_QK)
PAD_T = 8
SB_EXIT_LOG = -104.0
VMEM_LIMIT = 56 * 1024 * 1024

(S_CONF_A, S_CONF_B, S_CONF_G, S_SB_Q, S_SB_K, S_SB_V, S_SB_G, S_DF_Q, S_DF_K, S_DF_V, S_DF_G,
 S_SC_H, S_SC_B, S_SC_C, S_SC_G) = range(N_IN_SLICES)


def _silu(x):
    return x * (1.0 / (1.0 + jnp.exp(-x)))


def _cparams(sem):
    return pltpu.CompilerParams(dimension_semantics=sem, vmem_limit_bytes=VMEM_LIMIT)


def _log2(n):
    assert n & (n - 1) == 0
    return n.bit_length() - 1


def _head_cols(h):
    return slice(h * HEAD_DIM, (h + 1) * HEAD_DIM)


def _lane_chunks(x):
    return [x[:, c * HEAD_DIM:(c + 1) * HEAD_DIM] for c in range(x.shape[1] // HEAD_DIM)]


def _tree(op, xs):
    while len(xs) > 1:
        xs = [op(xs[n], xs[n + 1]) if n + 1 < len(xs) else xs[n] for n in range(0, len(xs), 2)]
    return xs[0]


def _qk(q, k):
    return lax.dot_general(q, k.astype(BF16), (((1,), (1,)), ((), ())), preferred_element_type=F32)


def _with_ones(v):
    v = v.astype(BF16)
    return jnp.concatenate([v, jnp.ones_like(v)], axis=1)


def _store_head_major(dst_ref, x):
    rows = x.shape[0]
    for h in range(N_HEADS):
        dst_ref[pl.ds(h, rows, stride=N_HEADS), :] = x[:, _head_cols(h)]


def _store_layer_head_major(dst_ref, x, layer, whole):
    if not whole:
        _store_head_major(dst_ref, x)
        return
    for other in range(dst_ref.shape[0]):
        if other != layer:
            dst_ref[other] = jnp.zeros(dst_ref.shape[1:], dst_ref.dtype)
    _store_head_major(dst_ref.at[layer], x)


def _inproj_kernel(x_ref, g_ref, w_ref, *rest, slabs_per_step, direct, layer, whole):
    n_direct = len(direct)
    o_ref = rest[-n_direct - 2]
    direct_refs = rest[-n_direct - 1:-1]
    xn_ref = rest[-1]
    j = pl.program_id(1)

    @pl.when(j == 0)
    def _():
        x = x_ref[...]
        ms = jnp.mean(x * x, axis=-1, keepdims=True)
        xn_ref[...] = ((x * lax.rsqrt(ms + EPS)) * g_ref[...]).astype(BF16)

    res = jnp.dot(xn_ref[...], w_ref[...], preferred_element_type=F32)
    for k in range(slabs_per_step):
        o_ref[k] = res[:, k * GROUP_W:(k + 1) * GROUP_W]
    for slab, d_ref in zip(direct, direct_refs):
        @pl.when(j == slab // slabs_per_step)
        def _(slab=slab, d_ref=d_ref):
            k = slab % slabs_per_step
            _store_layer_head_major(d_ref, res[:, k * GROUP_W:(k + 1) * GROUP_W], layer, whole)


def _layer_rows_spec(rows, depth, layer, whole, grid_rank):
    if grid_rank == 1:
        if whole:
            return pl.BlockSpec((depth, rows, HEAD_DIM), lambda i: (0, i, 0))
        return pl.BlockSpec((None, rows, HEAD_DIM), lambda i: (layer, i, 0))
    if whole:
        return pl.BlockSpec((depth, rows, HEAD_DIM), lambda i, j: (0, i, 0))
    return pl.BlockSpec((None, rows, HEAD_DIM), lambda i, j: (layer, i, 0))


def in_proj(x2d, g, w_bf16, tm, slabs_per_step, direct=(), direct_bufs=None, layer=0, depth=1):
    m, d = x2d.shape
    n = w_bf16.shape[1]
    ns = n // GROUP_W
    sps = slabs_per_step
    n_direct = len(direct)
    alias_in = list(direct_bufs) if direct_bufs is not None else []
    whole = not alias_in
    direct_shape = jax.ShapeDtypeStruct((depth, m * N_HEADS, HEAD_DIM), F32)
    outs = pl.pallas_call(
        functools.partial(_inproj_kernel, slabs_per_step=sps, direct=tuple(direct), layer=layer,
                          whole=whole),
        grid=(m // tm, ns // sps),
        in_specs=[pl.BlockSpec((tm, d), lambda i, j: (i, 0)),
                  pl.BlockSpec((1, d), lambda i, j: (0, 0)),
                  pl.BlockSpec((d, sps * GROUP_W), lambda i, j: (0, j))]
                 + [pl.BlockSpec(memory_space=pl.ANY)] * len(alias_in),
        out_specs=[pl.BlockSpec((sps, tm, GROUP_W), lambda i, j: (j, i, 0))]
                  + [_layer_rows_spec(tm * N_HEADS, depth, layer, whole, 2)] * n_direct,
        out_shape=[jax.ShapeDtypeStruct((ns, m, GROUP_W), F32)] + [direct_shape] * n_direct,
        input_output_aliases={3 + n: 1 + n for n in range(len(alias_in))},
        scratch_shapes=[pltpu.VMEM((tm, d), BF16)],
        compiler_params=_cparams(("parallel", "arbitrary")),
        name="in_proj",
    )(x2d, g.reshape(1, d), w_bf16, *alias_in)
    return outs[0], tuple(outs[1:])


def _outproj_kernel(ya_ref, yb_ref, yc_ref, yd_ref, w_ref, x_ref, fg_ref, o_ref, *, final):
    acc = x_ref[...]
    for m, y_ref in enumerate((ya_ref, yb_ref, yc_ref, yd_ref)):
        acc = acc + jnp.dot(y_ref[...], w_ref[m * GROUP_W:(m + 1) * GROUP_W, :],
                            preferred_element_type=F32)
    if final:
        ms = jnp.mean(acc * acc, axis=-1, keepdims=True)
        acc = (acc * lax.rsqrt(ms + EPS)) * fg_ref[...]
    o_ref[...] = acc


def out_proj(ys, w_bf16, x2d, final_g, final, tm):
    m, d = x2d.shape
    yspec = pl.BlockSpec((tm, GROUP_W), lambda i: (i, 0))
    return pl.pallas_call(
        functools.partial(_outproj_kernel, final=final),
        grid=(m // tm,),
        in_specs=[yspec, yspec, yspec, yspec,
                  pl.BlockSpec((N_MIXERS * GROUP_W, d), lambda i: (0, 0)),
                  pl.BlockSpec((tm, d), lambda i: (i, 0)),
                  pl.BlockSpec((1, d), lambda i: (0, 0))],
        out_specs=pl.BlockSpec((tm, d), lambda i: (i, 0)),
        out_shape=jax.ShapeDtypeStruct((m, d), F32),
        compiler_params=_cparams(("parallel",)),
        name="out_proj",
    )(*ys, w_bf16, x2d, final_g.reshape(1, d))


def _sb_terms(z):
    t = jnp.log1p(jnp.exp(-jnp.abs(z)))
    return jnp.minimum(z, 0.0) - t, -jnp.maximum(z, 0.0) - t


def _suffix_sums(lk, u):
    hi = lk.astype(BF16)
    lo = (lk - hi.astype(F32)).astype(BF16)
    return (jnp.dot(hi, u, preferred_element_type=F32) + jnp.dot(lo, u, preferred_element_type=F32))


def _strict_lower(n):
    row = lax.broadcasted_iota(jnp.int32, (n, n), 0)
    col = lax.broadcasted_iota(jnp.int32, (n, n), 1)
    return row, col


def _sb_prompt_kernel(q_ref, k_ref, v_ref, g_ref, o_ref, acc_ref, carry_ref, *, tb):
    i = pl.program_id(1)
    q = q_ref[...].astype(BF16)
    row, col = _strict_lower(tb)
    u = jnp.where(row > col, 1.0, 0.0).astype(BF16)
    visible = col < row

    def scores(jb):
        start = pl.multiple_of(jb * tb, tb)
        z = _qk(q, k_ref[pl.ds(start, tb), :]) * SB_SCALE
        return z, v_ref[pl.ds(start, tb), :].astype(BF16)

    z, v = scores(i)
    ls, lk = _sb_terms(z)
    lk = jnp.where(visible, lk, 0.0)
    a = jnp.where(visible, jnp.exp(ls + _suffix_sums(lk, u)), 0.0)
    acc_ref[...] = jnp.dot(a.astype(BF16), v, preferred_element_type=F32)
    carry_ref[...] = jnp.sum(lk, axis=1, keepdims=True)

    def cond(state):
        jb, go = state
        return jnp.logical_and(jb >= 0, go)

    def body(state):
        jb, _ = state
        z, v = scores(jb)
        ls, lk = _sb_terms(z)
        carry = carry_ref[...]
        a = jnp.exp(ls + _suffix_sums(lk, u) + carry)
        acc_ref[...] += jnp.dot(a.astype(BF16), v, preferred_element_type=F32)
        carry = carry + jnp.sum(lk, axis=1, keepdims=True)
        carry_ref[...] = carry
        return jb - 1, jnp.max(carry) > SB_EXIT_LOG

    lax.while_loop(cond, body, (i - 1, jnp.max(carry_ref[...]) > SB_EXIT_LOG))
    o_ref[...] = (acc_ref[...] * _silu(g_ref[...])).astype(o_ref.dtype)


def sb_prompt(proj, tb):
    _, t, _ = proj.shape

    def spec(rows, s, per_q):
        if per_q:
            return pl.BlockSpec((None, rows, HEAD_DIM), lambda h, i: (s, i, h))
        return pl.BlockSpec((None, rows, HEAD_DIM), lambda h, i: (s, 0, h))

    return pl.pallas_call(
        functools.partial(_sb_prompt_kernel, tb=tb),
        grid=(N_HEADS, t // tb),
        in_specs=[spec(tb, S_SB_Q, True), spec(t, S_SB_K, False), spec(t, S_SB_V, False),
                  spec(tb, S_SB_G, True)],
        out_specs=pl.BlockSpec((tb, HEAD_DIM), lambda h, i: (i, h)),
        out_shape=jax.ShapeDtypeStruct((t, GROUP_W), BF16),
        scratch_shapes=[pltpu.VMEM((tb, HEAD_DIM), F32), pltpu.VMEM((tb, 1), F32)],
        compiler_params=_cparams(("parallel", "arbitrary")),
        name="sb_prompt",
    )(proj, proj, proj, proj)


def _page_masks(rows, rows_per_head, page_rows):
    row = lax.broadcasted_iota(jnp.int32, (rows, page_rows), 0)
    col = lax.broadcasted_iota(jnp.int32, (rows, page_rows), 1)
    own = (col & (N_HEADS - 1)) == (row >> _log2(rows_per_head))
    return own, col >> _log2(N_HEADS), row & (PAD_T - 1)


def _sb_decode_kernel(pt_ref, q_ref, kn_ref, vn_ref, g_ref, u_ref, ck_ref, cv_ref, o_ref,
                      kbuf, vbuf, newk, newv, sem, acc_ref, carry_ref, *, layer, n_pages, n_tok):
    b = pl.program_id(0)
    rows = N_HEADS * PAD_T
    page_rows = newk.shape[0]

    def page_copies(p, slot):
        pg = pt_ref[b, p]
        return (pltpu.make_async_copy(ck_ref.at[layer, pg], kbuf.at[slot], sem.at[0, slot]),
                pltpu.make_async_copy(cv_ref.at[layer, pg], vbuf.at[slot], sem.at[1, slot]))

    def start_fetch(p, slot):
        for c in page_copies(p, slot):
            c.start()

    def wait_fetch(p, slot):
        for c in page_copies(p, slot):
            c.wait()

    start_fetch(n_pages - 1, 0)

    qall = jnp.concatenate([q_ref[h] for h in range(N_HEADS)], axis=0).astype(BF16)
    own, key, tok = _page_masks(rows, PAD_T, page_rows)
    real_row = (lax.broadcasted_iota(jnp.int32, (rows, 1), 0) & (PAD_T - 1)) < n_tok
    u = u_ref[...]

    def attend(k, v, visible, carry):
        ls, lk = _sb_terms(_qk(qall, k) * SB_SCALE)
        lk = jnp.where(visible, lk, 0.0)
        e = ls + _suffix_sums(lk, u)
        if carry is not None:
            e = e + carry
        a = jnp.where(visible, jnp.exp(e), 0.0).astype(BF16)
        return (jnp.dot(a, v.astype(BF16), preferred_element_type=F32),
                jnp.sum(lk, axis=1, keepdims=True))

    def live(carry):
        return jnp.max(jnp.where(real_row, carry, -jnp.inf)) > SB_EXIT_LOG

    newk[...] = jnp.zeros_like(newk)
    newv[...] = jnp.zeros_like(newv)
    newk[0:kn_ref.shape[0], :] = kn_ref[...]
    newv[0:vn_ref.shape[0], :] = vn_ref[...]
    pv, dc = attend(newk[...], newv[...], jnp.logical_and(own, key < tok), None)
    acc_ref[...] = pv
    carry_ref[...] = dc

    def cond(state):
        p, go = state
        return jnp.logical_and(p >= 0, go)

    def body(state):
        p, _ = state
        slot = (n_pages - 1 - p) % 2
        wait_fetch(p, slot)

        @pl.when(p >= 1)
        def _():
            start_fetch(p - 1, 1 - slot)

        carry = carry_ref[...]
        pv, dc = attend(kbuf[slot], vbuf[slot], own, carry)
        acc_ref[...] += pv
        carry = carry + dc
        carry_ref[...] = carry
        return p - 1, live(carry)

    p_end, _ = lax.while_loop(cond, body, (jnp.int32(n_pages - 1), live(carry_ref[...])))

    @pl.when(p_end >= 0)
    def _():
        wait_fetch(p_end, (n_pages - 1 - p_end) % 2)

    acc = acc_ref[...]
    o_ref[...] = jnp.concatenate([acc[h * PAD_T:(h + 1) * PAD_T] * _silu(g_ref[h])
                                  for h in range(N_HEADS)], axis=1)


def sb_decode(q8, kn, vn, g8, cache_k, cache_v, page_table, layer, n_tok):
    bsz = page_table.shape[0]
    n_pages = page_table.shape[1]
    page_rows = cache_k.shape[2]
    tok_spec = pl.BlockSpec((None, PAD_T, GROUP_W), lambda b, pt: (b, 0, 0))
    head_spec = pl.BlockSpec((N_HEADS, PAD_T, HEAD_DIM), lambda b, pt: (0, b, 0))
    new_spec = pl.BlockSpec((None, kn.shape[1], HEAD_DIM), lambda b, pt: (b, 0, 0))
    any_spec = pl.BlockSpec(memory_space=pl.ANY)
    rows = N_HEADS * PAD_T
    row, col = _strict_lower(page_rows)
    u = jnp.where(row > col, 1.0, 0.0).astype(BF16)
    page_buf = pltpu.VMEM((2, page_rows, HEAD_DIM), F32)
    new_buf = pltpu.VMEM((page_rows, HEAD_DIM), F32)
    return pl.pallas_call(
        functools.partial(_sb_decode_kernel, layer=layer, n_pages=n_pages, n_tok=n_tok),
        grid_spec=pltpu.PrefetchScalarGridSpec(
            num_scalar_prefetch=1,
            grid=(bsz,),
            in_specs=[head_spec, new_spec, new_spec, head_spec,
                      pl.BlockSpec((page_rows, page_rows), lambda b, pt: (0, 0)), any_spec, any_spec],
            out_specs=tok_spec,
            scratch_shapes=[page_buf, page_buf, new_buf, new_buf,
                            pltpu.SemaphoreType.DMA((2, 2)),
                            pltpu.VMEM((rows, HEAD_DIM), F32), pltpu.VMEM((rows, 1), F32)]),
        out_shape=jax.ShapeDtypeStruct((bsz, PAD_T, GROUP_W), F32),
        compiler_params=_cparams(("arbitrary",)),
        name="sb_decode",
    )(page_table, q8, kn, vn, g8, u, cache_k, cache_v)


def _lambda(lq1_ref, lk1_ref, lq2_ref, lk2_ref, lam_init):
    s1 = jnp.sum(lq1_ref[...] * lk1_ref[...], axis=-1, keepdims=True)
    s2 = jnp.sum(lq2_ref[...] * lk2_ref[...], axis=-1, keepdims=True)
    return jnp.exp(s1) - jnp.exp(s2) + lam_init


def _diff_finish(o1, o2, lam, sg, gate, lam_init):
    w = o1 - lam * o2
    ms = jnp.mean(w * w, axis=-1, keepdims=True)
    y = (w * lax.rsqrt(ms + EPS)) * sg
    return (y * (1.0 - lam_init)) * _silu(gate)


def _split_components(q):
    lane = lax.broadcasted_iota(jnp.int32, q.shape, 1)
    zero = jnp.zeros_like(q)
    return jnp.concatenate([jnp.where(lane < DIFF_QK, q, zero), jnp.where(lane >= DIFF_QK, q, zero)],
                           axis=0)


def _diff_prompt_kernel(q_ref, k_ref, v_ref, g_ref, lq1_ref, lk1_ref, lq2_ref, lk2_ref, sg_ref,
                        o_ref, s0_ref, s1_ref, mt0_ref, mt1_ref, m_ref, acc_ref, *, tb, lam_init):
    i = pl.program_id(1)
    qq = _split_components(q_ref[...])
    m_ref[...] = jnp.full_like(m_ref, -jnp.inf)
    acc_ref[...] = jnp.zeros_like(acc_ref)
    bufs = ((s0_ref, mt0_ref), (s1_ref, mt1_ref))

    def score_stage(jb, slot, masked):
        s_ref, mt_ref = bufs[slot]
        start = pl.multiple_of(jb * tb, tb)
        s = _qk(qq, k_ref[pl.ds(start, tb), :])
        if masked:
            row = lax.broadcasted_iota(jnp.int32, s.shape, 0) & (tb - 1)
            col = lax.broadcasted_iota(jnp.int32, s.shape, 1)
            s = jnp.where(col <= row, s, -jnp.inf)
        s_ref[...] = s
        mt_ref[...] = jnp.broadcast_to(jnp.max(s, axis=1, keepdims=True), mt_ref.shape)

    def value_stage(jb, slot):
        s_ref, mt_ref = bufs[slot]
        m_old = m_ref[...]
        m_new = jnp.maximum(m_old, mt_ref[...])
        alpha = jnp.exp2(m_old - m_new)
        p = jnp.concatenate([jnp.exp2(c - m_new) for c in _lane_chunks(s_ref[...])], axis=1)
        start = pl.multiple_of(jb * tb, tb)
        pv = jnp.dot(p.astype(BF16), _with_ones(v_ref[pl.ds(start, tb), :]), preferred_element_type=F32)
        acc_ref[...] = jnp.concatenate([alpha, alpha], axis=1) * acc_ref[...] + pv
        m_ref[...] = m_new

    @pl.when(i == 0)
    def _():
        score_stage(0, 0, True)
        value_stage(0, 0)

    @pl.when(i > 0)
    def _():
        score_stage(0, 0, False)
        pairs = (i - 1) // 2

        def body(g, c):
            score_stage(2 * g + 1, 1, False)
            value_stage(2 * g, 0)
            score_stage(2 * g + 2, 0, False)
            value_stage(2 * g + 1, 1)
            return c

        lax.fori_loop(0, pairs, body, 0)
        t0 = 2 * pairs

        @pl.when(i - t0 == 1)
        def _():
            score_stage(i, 1, True)
            value_stage(t0, 0)
            value_stage(i, 1)

        @pl.when(i - t0 == 2)
        def _():
            score_stage(t0 + 1, 1, False)
            value_stage(t0, 0)
            score_stage(i, 0, True)
            value_stage(t0 + 1, 1)
            value_stage(i, 0)

    acc = acc_ref[...]
    o = acc[:, :HEAD_DIM] / acc[:, HEAD_DIM:]
    lam = _lambda(lq1_ref, lk1_ref, lq2_ref, lk2_ref, lam_init)
    o_ref[...] = _diff_finish(o[:tb], o[tb:], lam, sg_ref[...], g_ref[...], lam_init).astype(o_ref.dtype)


def diff_prompt(qb, kb, vb, proj, lam_params, subln_g, lam_init, tb):
    t = qb.shape[0]
    small = pl.BlockSpec((1, DIFF_QK), lambda h, i: (0, 0))
    return pl.pallas_call(
        functools.partial(_diff_prompt_kernel, tb=tb, lam_init=lam_init),
        grid=(N_HEADS, t // tb),
        in_specs=[pl.BlockSpec((tb, HEAD_DIM), lambda h, i: (i, h)),
                  pl.BlockSpec((t, HEAD_DIM), lambda h, i: (0, h)),
                  pl.BlockSpec((t, HEAD_DIM), lambda h, i: (0, h)),
                  pl.BlockSpec((None, tb, HEAD_DIM), lambda h, i: (S_DF_G, i, h)),
                  small, small, small, small,
                  pl.BlockSpec((1, HEAD_DIM), lambda h, i: (0, 0))],
        out_specs=pl.BlockSpec((tb, HEAD_DIM), lambda h, i: (i, h)),
        out_shape=jax.ShapeDtypeStruct((t, GROUP_W), BF16),
        scratch_shapes=[pltpu.VMEM((2 * tb, tb), F32), pltpu.VMEM((2 * tb, tb), F32),
                        pltpu.VMEM((2 * tb, HEAD_DIM), F32), pltpu.VMEM((2 * tb, HEAD_DIM), F32),
                        pltpu.VMEM((2 * tb, HEAD_DIM), F32), pltpu.VMEM((2 * tb, 2 * HEAD_DIM), F32)],
        compiler_params=_cparams(("parallel", "arbitrary")),
        name="diff_prompt",
    )(qb, kb, vb, proj, *[p.reshape(1, DIFF_QK) for p in lam_params], subln_g.reshape(1, HEAD_DIM))


def _diff_decode_kernel(pt_ref, q_ref, kn_ref, vn_ref, g_ref, lq1_ref, lk1_ref, lq2_ref, lk2_ref,
                        sg_ref, *rest, pages_per_step, group, lam_init):
    k_refs = rest[:pages_per_step]
    v_refs = rest[pages_per_step:2 * pages_per_step]
    o_ref, m_ref, acc_ref, newk, newv = rest[2 * pages_per_step:]
    c = pl.program_id(1)
    hrows = 2 * PAD_T
    rows = N_HEADS * hrows
    page_rows = newk.shape[0]

    qall = jnp.concatenate([_split_components(q_ref[h] * (DIFF_SCALE * LOG2E))
                            for h in range(N_HEADS)], axis=0).astype(BF16)
    zq = jnp.zeros_like(qall)
    qpair = jnp.concatenate([jnp.concatenate([qall, zq], axis=1),
                             jnp.concatenate([zq, qall], axis=1)], axis=0)
    own, key, tok = _page_masks(rows, hrows, page_rows)

    @pl.when(c == 0)
    def _():
        m_ref[...] = jnp.full_like(m_ref, -jnp.inf)
        acc_ref[...] = jnp.zeros_like(acc_ref)

    def page_scores(ks):
        out = []
        for n in range(0, len(ks) - 1, 2):
            kk = jnp.concatenate([ks[n].astype(BF16), ks[n + 1].astype(BF16)], axis=1)
            s2 = _qk(qpair, kk)
            out += [s2[:rows], s2[rows:]]
        if len(ks) % 2:
            out.append(_qk(qall, ks[-1]))
        return out

    def update(ks, vs, visible):
        s = jnp.concatenate([jnp.where(visible, z, -jnp.inf) for z in page_scores(ks)], axis=1)
        m_prev = m_ref[...]
        m_new = jnp.maximum(m_prev, jnp.max(s, axis=1, keepdims=True))
        p = jnp.exp2(s - m_new).astype(BF16)
        pv = _tree(jnp.add, [jnp.dot(p[:, n * page_rows:(n + 1) * page_rows], _with_ones(v),
                                     preferred_element_type=F32) for n, v in enumerate(vs)])
        return m_new, jnp.exp2(m_prev - m_new) * acc_ref[...] + pv

    for g0 in range(0, pages_per_step, group):
        m_new, acc_new = update([r[...] for r in k_refs[g0:g0 + group]],
                                [r[...] for r in v_refs[g0:g0 + group]], own)
        m_ref[...] = m_new
        acc_ref[...] = acc_new

    @pl.when(c == pl.num_programs(1) - 1)
    def _():
        newk[...] = jnp.zeros_like(newk)
        newv[...] = jnp.zeros_like(newv)
        newk[0:kn_ref.shape[0], :] = kn_ref[...]
        newv[0:vn_ref.shape[0], :] = vn_ref[...]
        _, acc = update([newk[...]], [newv[...]], jnp.logical_and(own, key <= tok))
        o = acc[:, :HEAD_DIM] / acc[:, HEAD_DIM:]
        lam = _lambda(lq1_ref, lk1_ref, lq2_ref, lk2_ref, lam_init)
        sg = sg_ref[...]
        outs = []
        for h in range(N_HEADS):
            o1 = o[h * hrows:h * hrows + PAD_T]
            o2 = o[h * hrows + PAD_T:(h + 1) * hrows]
            outs.append(_diff_finish(o1, o2, lam, sg, g_ref[h], lam_init))
        o_ref[...] = jnp.concatenate(outs, axis=1)


def diff_decode(q8, kn, vn, g8, cache_k, cache_v, page_table, lam_params, subln_g, lam_init,
                layer, pages_per_step, group):
    bsz = page_table.shape[0]
    n_pages = page_table.shape[1]
    page_rows = cache_k.shape[2]
    pps = pages_per_step
    tok_spec = pl.BlockSpec((None, PAD_T, GROUP_W), lambda b, c, pt: (b, 0, 0))
    head_spec = pl.BlockSpec((N_HEADS, PAD_T, HEAD_DIM), lambda b, c, pt: (0, b, 0))
    new_spec = pl.BlockSpec((None, kn.shape[1], HEAD_DIM), lambda b, c, pt: (b, 0, 0))
    small = pl.BlockSpec((1, DIFF_QK), lambda b, c, pt: (0, 0))

    def page_spec(n):
        return pl.BlockSpec((None, None, page_rows, HEAD_DIM),
                            lambda b, c, pt: (layer, pt[b, c * pps + n], 0, 0))

    rows = N_HEADS * 2 * PAD_T
    new_buf = pltpu.VMEM((page_rows, HEAD_DIM), F32)
    return pl.pallas_call(
        functools.partial(_diff_decode_kernel, pages_per_step=pps, group=group, lam_init=lam_init),
        grid_spec=pltpu.PrefetchScalarGridSpec(
            num_scalar_prefetch=1,
            grid=(bsz, n_pages // pps),
            in_specs=[head_spec, new_spec, new_spec, head_spec, small, small, small, small,
                      pl.BlockSpec((1, HEAD_DIM), lambda b, c, pt: (0, 0))]
                     + [page_spec(n) for n in range(pps)] + [page_spec(n) for n in range(pps)],
            out_specs=tok_spec,
            scratch_shapes=[pltpu.VMEM((rows, 1), F32), pltpu.VMEM((rows, 2 * HEAD_DIM), F32),
                            new_buf, new_buf]),
        out_shape=jax.ShapeDtypeStruct((bsz, PAD_T, GROUP_W), F32),
        compiler_params=_cparams(("parallel", "arbitrary")),
        name="diff_decode",
    )(page_table, q8, kn, vn, g8, *[p.reshape(1, DIFF_QK) for p in lam_params],
      subln_g.reshape(1, HEAD_DIM), *([cache_k] * pps), *([cache_v] * pps))


CONF_HALO = 32
SC_HALO = 8
SUBLANES = 8
MIX_ROWS = 32


def _rope_tables(pos):
    half = ROT_DIM // 2
    inv = ROPE_THETA ** (-2.0 * jnp.arange(half, dtype=F32) / ROT_DIM)
    ang = pos.astype(F32)[:, None] * inv[None, :]
    cos, sin = jnp.cos(ang), jnp.sin(ang)
    j = jnp.arange(HEAD_DIM) % DIFF_QK
    f = j % half
    c = jnp.where(j < ROT_DIM, cos[:, f], 1.0)
    sa = jnp.where((j >= half) & (j < ROT_DIM), sin[:, f], 0.0)
    sb = jnp.where(j < half, -sin[:, f], 0.0)
    return c, sa, sb


def _rope_rows(x, c, sa, sb):
    half = ROT_DIM // 2
    out = []
    for xh in _lane_chunks(x):
        out.append(xh * c + pltpu.roll(xh, half, 1) * sa + pltpu.roll(xh, HEAD_DIM - half, 1) * sb)
    return jnp.concatenate(out, axis=1)


def _prompt_mix_kernel(a_ref, b_ref, g_ref, dq_ref, dk_ref, dv_ref, sh_ref, sb_ref, sc_ref, sg_ref,
                       cw_ref, cb_ref, lg_ref, lb_ref, sw_ref, rc_ref, rsa_ref, rsb_ref, *rest,
                       tb, conf_w, sc_w, layer, whole):
    ya_ref, yd_ref, dqb_ref, dk_out_ref, dkb_ref, dvb_ref, nconf_ref, nsc_ref, hb, hs, ub = rest[-11:]
    i = pl.program_id(0)

    @pl.when(i == 0)
    def _():
        hb[0:CONF_HALO, :] = jnp.zeros((CONF_HALO, GROUP_W), F32)
        ub[0:SC_HALO, :] = jnp.zeros((SC_HALO, GROUP_W), F32)

    @pl.when(i > 0)
    def _():
        hb[0:CONF_HALO, :] = hb[tb:tb + CONF_HALO, :]
        ub[0:SC_HALO, :] = ub[tb:tb + SC_HALO, :]

    hb[CONF_HALO:CONF_HALO + tb, :] = a_ref[...] * (1.0 / (1.0 + jnp.exp(-b_ref[...])))
    ub[SC_HALO:SC_HALO + tb, :] = sc_ref[...] * sh_ref[...]

    off = CONF_HALO - (conf_w - 1)
    span = hs.shape[1]
    for k in range(1, SUBLANES):
        hs[k - 1, :, :] = hb[k:k + span, :]

    def tap(w_ref, w):
        return jnp.concatenate([w_ref[w]] * (MIX_ROWS // SUBLANES), axis=0)

    def window(r):
        k, base = r % SUBLANES, r - r % SUBLANES
        if k == 0:
            return hb[base:base + MIX_ROWS, :]
        return hs[k - 1, base:base + MIX_ROWS, :]

    for r0 in range(0, tb, MIX_ROWS):
        rows = slice(r0, r0 + MIX_ROWS)
        acc = jnp.zeros((MIX_ROWS, GROUP_W), F32) + cb_ref[...]
        for w in range(conf_w):
            acc = acc + tap(cw_ref, w) * window(r0 + off + w)
        mu = jnp.mean(acc, axis=-1, keepdims=True)
        xc = acc - mu
        y = (xc * lax.rsqrt(jnp.mean(xc * xc, axis=-1, keepdims=True) + EPS)) * lg_ref[...] + lb_ref[...]
        ya_ref[rows, :] = (_silu(y) * _silu(g_ref[rows, :])).astype(ya_ref.dtype)
        u0 = r0 + SC_HALO - (sc_w - 1)
        conv = jnp.zeros((MIX_ROWS, GROUP_W), F32)
        for w in range(sc_w):
            conv = conv + tap(sw_ref, w) * ub[u0 + w:u0 + w + MIX_ROWS, :]
        yd_ref[rows, :] = ((sb_ref[rows, :] * conv) * _silu(sg_ref[rows, :])).astype(yd_ref.dtype)

    c, sa, sb = rc_ref[...], rsa_ref[...], rsb_ref[...]
    dqb_ref[...] = (_rope_rows(dq_ref[...], c, sa, sb) * (DIFF_SCALE * LOG2E)).astype(dqb_ref.dtype)
    dk = _rope_rows(dk_ref[...], c, sa, sb)
    _store_layer_head_major(dk_out_ref, dk, layer, whole)
    dkb_ref[...] = dk.astype(dkb_ref.dtype)
    dvb_ref[...] = dv_ref[...].astype(dvb_ref.dtype)

    @pl.when(i == pl.num_programs(0) - 1)
    def _():
        nconf_ref[...] = hb[CONF_HALO + tb - (conf_w - 1):CONF_HALO + tb, :]
        nsc_ref[...] = ub[SC_HALO + tb - (sc_w - 1):SC_HALO + tb, :]


def prompt_mix(proj, conf_w, conf_b, ln_g, ln_b, sc_w, rope_tabs, tb, k_buf=None, layer=0, depth=1):
    _, t, _ = proj.shape
    cw, scw = conf_w.shape[0], sc_w.shape[0]

    def slab(s):
        return pl.BlockSpec((None, tb, GROUP_W), lambda i: (s, i, 0))

    def full(r):
        return pl.BlockSpec((r, GROUP_W), lambda i: (0, 0))

    def taps(n):
        return pl.BlockSpec((n, SUBLANES, GROUP_W), lambda i: (0, 0, 0))

    def replicate(w):
        return jnp.broadcast_to(w[:, None, :], (w.shape[0], SUBLANES, w.shape[1]))

    row_blk = pl.BlockSpec((tb, GROUP_W), lambda i: (i, 0))
    tab_blk = pl.BlockSpec((tb, HEAD_DIM), lambda i: (i, 0))
    slabs = (S_CONF_A, S_CONF_B, S_CONF_G, S_DF_Q, S_DF_K, S_DF_V, S_SC_H, S_SC_B, S_SC_C, S_SC_G)
    act = jax.ShapeDtypeStruct((t, GROUP_W), BF16)
    alias_in = [] if k_buf is None else [k_buf]
    n_in = len(slabs) + 8
    return pl.pallas_call(
        functools.partial(_prompt_mix_kernel, tb=tb, conf_w=cw, sc_w=scw, layer=layer,
                          whole=not alias_in),
        grid=(t // tb,),
        in_specs=[slab(s) for s in slabs]
                 + [taps(cw), full(1), full(1), full(1), taps(scw), tab_blk, tab_blk, tab_blk]
                 + [pl.BlockSpec(memory_space=pl.ANY)] * len(alias_in),
        out_specs=[row_blk, row_blk, row_blk,
                   _layer_rows_spec(tb * N_HEADS, depth, layer, not alias_in, 1),
                   row_blk, row_blk, full(cw - 1), full(scw - 1)],
        out_shape=[act, act, act, jax.ShapeDtypeStruct((depth, t * N_HEADS, HEAD_DIM), F32), act, act,
                   jax.ShapeDtypeStruct((cw - 1, GROUP_W), F32),
                   jax.ShapeDtypeStruct((scw - 1, GROUP_W), F32)],
        scratch_shapes=[pltpu.VMEM((CONF_HALO + tb, GROUP_W), F32),
                        pltpu.VMEM((SUBLANES - 1, CONF_HALO + tb - SUBLANES, GROUP_W), F32),
                        pltpu.VMEM((SC_HALO + tb, GROUP_W), F32)],
        input_output_aliases={n_in: 3} if alias_in else {},
        compiler_params=_cparams(("arbitrary",)),
        name="prompt_mix",
    )(*([proj] * len(slabs)), replicate(conf_w), conf_b.reshape(1, -1), ln_g.reshape(1, -1),
      ln_b.reshape(1, -1), replicate(sc_w), *rope_tabs, *alias_in)


def _decode_mix_kernel(proj_ref, sconf_ref, ssc_ref, cw_ref, cb_ref, lg_ref, lb_ref, sw_ref,
                       rc_ref, rsa_ref, rsb_ref,
                       ya_ref, yd_ref, sbq_ref, sbg_ref, dfq_ref, dfg_ref, sbk_ref, sbv_ref, dfk_ref,
                       dfv_ref, nconf_ref, nsc_ref, *, bsz, n_tok, conf_w, sc_w):
    def tok(slab, t):
        return proj_ref[slab, t * bsz:(t + 1) * bsz, :]

    def put_tok(ref, t, x):
        ref[t * bsz:(t + 1) * bsz, :] = x.astype(ref.dtype)

    def put_pad(ref, t, x):
        for h in range(N_HEADS):
            ref[h, pl.ds(t, bsz, stride=PAD_T), :] = x[:, _head_cols(h)]

    def put_heads(ref, t, x):
        for h in range(N_HEADS):
            ref[pl.ds(t * N_HEADS + h, bsz, stride=n_tok * N_HEADS), :] = x[:, _head_cols(h)]

    def tap(w_ref, w):
        if bsz % SUBLANES:
            return jnp.broadcast_to(w_ref[w, 0:1, :], (bsz, GROUP_W))
        return jnp.concatenate([w_ref[w]] * (bsz // SUBLANES), axis=0)

    h = [tok(S_CONF_A, t) * (1.0 / (1.0 + jnp.exp(-tok(S_CONF_B, t)))) for t in range(n_tok)]

    def hist(r):
        return sconf_ref[r] if r < conf_w - 1 else h[r - (conf_w - 1)]

    for t in range(n_tok):
        acc = jnp.zeros((bsz, GROUP_W), F32) + cb_ref[...]
        for w in range(conf_w):
            acc = acc + tap(cw_ref, w) * hist(t + w)
        mu = jnp.mean(acc, axis=-1, keepdims=True)
        xc = acc - mu
        y = (xc * lax.rsqrt(jnp.mean(xc * xc, axis=-1, keepdims=True) + EPS)) * lg_ref[...] + lb_ref[...]
        put_tok(ya_ref, t, _silu(y) * _silu(tok(S_CONF_G, t)))
    for r in range(conf_w - 1):
        nconf_ref[r] = hist(r + n_tok)

    u = [tok(S_SC_C, t) * tok(S_SC_H, t) for t in range(n_tok)]

    def uhist(r):
        return ssc_ref[r] if r < sc_w - 1 else u[r - (sc_w - 1)]

    for t in range(n_tok):
        conv = jnp.zeros((bsz, GROUP_W), F32)
        for w in range(sc_w):
            conv = conv + tap(sw_ref, w) * uhist(t + w)
        put_tok(yd_ref, t, (tok(S_SC_B, t) * conv) * _silu(tok(S_SC_G, t)))
    for r in range(sc_w - 1):
        nsc_ref[r] = uhist(r + n_tok)

    for ref in (sbq_ref, sbg_ref, dfq_ref, dfg_ref):
        ref[...] = jnp.zeros_like(ref)
    for t in range(n_tok):
        tabs = [jnp.broadcast_to(r[t:t + 1, :], (bsz, HEAD_DIM)) for r in (rc_ref, rsa_ref, rsb_ref)]
        put_pad(sbq_ref, t, tok(S_SB_Q, t))
        put_pad(sbg_ref, t, tok(S_SB_G, t))
        put_pad(dfq_ref, t, _rope_rows(tok(S_DF_Q, t), *tabs))
        put_pad(dfg_ref, t, tok(S_DF_G, t))
        put_heads(sbk_ref, t, tok(S_SB_K, t))
        put_heads(sbv_ref, t, tok(S_SB_V, t))
        put_heads(dfk_ref, t, _rope_rows(tok(S_DF_K, t), *tabs))
        put_heads(dfv_ref, t, tok(S_DF_V, t))


def decode_mix(proj, state_conf, state_sc, conf_w, conf_b, ln_g, ln_b, sc_w, rope_tabs, bsz, n_tok):
    cw, scw = conf_w.shape[0], sc_w.shape[0]

    def replicate(w):
        return jnp.broadcast_to(w[:, None, :], (w.shape[0], SUBLANES, w.shape[1]))

    act = jax.ShapeDtypeStruct((bsz * n_tok, GROUP_W), BF16)
    pad = jax.ShapeDtypeStruct((N_HEADS, bsz * PAD_T, HEAD_DIM), F32)
    hm = jax.ShapeDtypeStruct((bsz * n_tok * N_HEADS, HEAD_DIM), F32)
    return pl.pallas_call(
        functools.partial(_decode_mix_kernel, bsz=bsz, n_tok=n_tok, conf_w=cw, sc_w=scw),
        out_shape=[act, act, pad, pad, pad, pad, hm, hm, hm, hm,
                   jax.ShapeDtypeStruct(state_conf.shape, F32), jax.ShapeDtypeStruct(state_sc.shape, F32)],
        compiler_params=pltpu.CompilerParams(vmem_limit_bytes=VMEM_LIMIT),
        name="decode_mix",
    )(proj, state_conf, state_sc, replicate(conf_w), conf_b.reshape(1, -1), ln_g.reshape(1, -1),
      ln_b.reshape(1, -1), replicate(sc_w), *rope_tabs)


def _pick_tile(n, pref):
    t = min(n, pref)
    while n % t:
        t //= 2
    return t


def kernel(x_prompt, x_sample, cache_sb_k, cache_sb_v, cache_diff_k, cache_diff_v, state_conf_conv,
           state_short_conv, page_table, norm_g, w_in, conf_dw_w, conf_dw_b, conf_ln_g, conf_ln_b,
           diff_lq1, diff_lk1, diff_lq2, diff_lk2, diff_subln_g, sc_conv_w, w_out, final_norm_g):
    bp, tp, d = x_prompt.shape
    bs, ts, _ = x_sample.shape
    depth = w_in.shape[0]
    n_pages = page_table.shape[1]
    page = cache_sb_k.shape[2]
    n_past = n_pages * page
    pos_p = jnp.arange(tp, dtype=jnp.int32)
    pos_s = n_past + jnp.arange(ts, dtype=jnp.int32)
    assert bp == 1 and ts <= PAD_T

    caches = [c.reshape(c.shape[0], c.shape[1], page * N_HEADS, HEAD_DIM)
              for c in (cache_sb_k, cache_sb_v, cache_diff_k, cache_diff_v)]
    w_in_b = w_in.astype(BF16)
    w_out_b = w_out.astype(BF16)

    hp = x_prompt.reshape(bp * tp, d)
    hs = jnp.swapaxes(x_sample, 0, 1).reshape(ts * bs, d)
    conf_states = jnp.swapaxes(state_conf_conv, 1, 2)
    sc_states = jnp.swapaxes(state_short_conv, 1, 2)
    tm_p = _pick_tile(bp * tp, 512)
    slabs_per_step = 5
    assert N_IN_SLICES % slabs_per_step == 0
    tm_s = _pick_tile(bs * ts, 128)
    tb_sb = _pick_tile(tp, 256)
    tb_df = _pick_tile(tp, 512)
    tb_mix = _pick_tile(tp, 256)
    assert tb_mix >= CONF_HALO >= conf_dw_w.shape[1] - 1 and SC_HALO >= sc_conv_w.shape[1] - 1
    rope_tabs_p = _rope_tables(pos_p)
    rope_tabs_s = _rope_tables(pos_s)
    pps = _pick_tile(n_pages, 16)
    pgroup = _pick_tile(pps, 2)

    p_conv = [[], []]
    s_new = [[] for _ in range(6)]
    kv_bufs = None
    dk_buf = None
    for l in range(depth):
        lam_init = 0.8 - 0.6 * math.exp(-0.3 * l)
        lam_params = (diff_lq1[l], diff_lk1[l], diff_lq2[l], diff_lk2[l])
        final = l == depth - 1

        proj, kv_bufs = in_proj(hp, norm_g[l], w_in_b[l], tm_p, slabs_per_step,
                                direct=(S_SB_K, S_SB_V, S_DF_V), direct_bufs=kv_bufs, layer=l,
                                depth=depth)
        ya, yd, dqb, dk_buf, dkb, dvb, new_conf, new_sc = prompt_mix(
            proj, conf_dw_w[l], conf_dw_b[l], conf_ln_g[l], conf_ln_b[l], sc_conv_w[l], rope_tabs_p,
            tb_mix, k_buf=dk_buf, layer=l, depth=depth)
        yb = sb_prompt(proj, tb_sb)
        yc = diff_prompt(dqb, dkb, dvb, proj, lam_params, diff_subln_g[l], lam_init, tb_df)
        hp = out_proj((ya, yb, yc, yd), w_out_b[l], hp, final_norm_g, final,
                      _pick_tile(bp * tp, 256))
        p_conv[0].append(new_conf[None])
        p_conv[1].append(new_sc[None])
        heads = lambda a, bb, tt: a.reshape(bb, tt, N_HEADS, HEAD_DIM)

        proj, _ = in_proj(hs, norm_g[l], w_in_b[l], tm_s, slabs_per_step)
        ya, yd, sbq, sbg, dfq, dfg, sbk, sbv, dfk, dfv, new_conf, new_sc = decode_mix(
            proj, conf_states[l], sc_states[l], conf_dw_w[l], conf_dw_b[l], conf_ln_g[l],
            conf_ln_b[l], sc_conv_w[l], rope_tabs_s, bs, ts)
        new_rows = lambda a: a.reshape(bs, ts * N_HEADS, HEAD_DIM)
        yb = sb_decode(sbq, new_rows(sbk), new_rows(sbv), sbg, caches[0], caches[1], page_table, l, ts)
        yc = diff_decode(dfq, new_rows(dfk), new_rows(dfv), dfg, caches[2], caches[3], page_table,
                         lam_params, diff_subln_g[l], lam_init, l, pps, pgroup)
        flat = lambda a: jnp.swapaxes(a[:, :ts], 0, 1).reshape(ts * bs, GROUP_W).astype(BF16)
        hs = out_proj((ya, flat(yb), flat(yc), yd), w_out_b[l], hs, final_norm_g, final, tm_s)
        for n, a in enumerate((sbk, sbv, dfk, dfv)):
            s_new[n].append(a.reshape(bs, ts, N_HEADS, HEAD_DIM))
        s_new[4].append(new_conf)
        s_new[5].append(new_sc)

    y_prompt = hp.reshape(bp, tp, d)
    y_sample = jnp.swapaxes(hs.reshape(ts, bs, d), 0, 1)
    p_kv = [a.reshape(depth, bp, tp, N_HEADS, HEAD_DIM)
            for a in (kv_bufs[0], kv_bufs[1], dk_buf, kv_bufs[2])]
    s_out = [jnp.stack(a, 0) for a in s_new]
    s_out[4:] = [jnp.swapaxes(a, 1, 2) for a in s_out[4:]]
    return (y_prompt, y_sample, *p_kv, *[jnp.stack(a, 0) for a in p_conv], *s_out)
```

```python
import functools
import math

import jax
import jax.numpy as jnp
from jax import lax
from jax.experimental import pallas as pl
from jax.experimental.pallas import tpu as pltpu

F32 = jnp.float32
BF16 = jnp.bfloat16

N_MIXERS = 4
N_HEADS = 4
HEAD_DIM = 128
GROUP_W = N_HEADS * HEAD_DIM
DIFF_QK = HEAD_DIM // 2
ROT_DIM = DIFF_QK // 4
ROPE_THETA = 500000.0
N_IN_SLICES = 15
EPS = 1e-6
LOG2E = 1.4426950408889634
SB_SCALE_LOG2 = LOG2E / math.sqrt(HEAD_DIM)
DIFF_SCALE = 1.0 / math.sqrt(DIFF_QK)
PAD_T = 8
SB_EXIT_LOG2 = -104.0 * LOG2E
VMEM_LIMIT = 56 * 1024 * 1024

(S_CONF_A, S_CONF_B, S_CONF_G, S_SB_Q, S_SB_K, S_SB_V, S_SB_G, S_DF_Q, S_DF_K, S_DF_V, S_DF_G,
 S_SC_H, S_SC_B, S_SC_C, S_SC_G) = range(N_IN_SLICES)


def _silu(x):
    return x * (1.0 / (1.0 + jnp.exp(-x)))


def _cparams(sem):
    return pltpu.CompilerParams(dimension_semantics=sem, vmem_limit_bytes=VMEM_LIMIT)


def _log2(n):
    assert n & (n - 1) == 0
    return n.bit_length() - 1


def _head_cols(h):
    return slice(h * HEAD_DIM, (h + 1) * HEAD_DIM)


def _lane_chunks(x):
    return [x[:, c * HEAD_DIM:(c + 1) * HEAD_DIM] for c in range(x.shape[1] // HEAD_DIM)]


def _tree(op, xs):
    while len(xs) > 1:
        xs = [op(xs[n], xs[n + 1]) if n + 1 < len(xs) else xs[n] for n in range(0, len(xs), 2)]
    return xs[0]


def _qk(q, k):
    return lax.dot_general(q, k.astype(BF16), (((1,), (1,)), ((), ())), preferred_element_type=F32)


def _with_ones(v):
    v = v.astype(BF16)
    return jnp.concatenate([v, jnp.ones_like(v)], axis=1)


def _store_head_major(dst_ref, x):
    rows = x.shape[0]
    for h in range(N_HEADS):
        dst_ref[pl.ds(h, rows, stride=N_HEADS), :] = x[:, _head_cols(h)]


def _store_layer_head_major(dst_ref, x, layer, whole):
    if not whole:
        _store_head_major(dst_ref, x)
        return
    for other in range(dst_ref.shape[0]):
        if other != layer:
            dst_ref[other] = jnp.zeros(dst_ref.shape[1:], dst_ref.dtype)
    _store_head_major(dst_ref.at[layer], x)


def _inproj_kernel(x_ref, g_ref, w_ref, *rest, slabs_per_step, direct, layer, whole):
    n_direct = len(direct)
    o_ref = rest[-n_direct - 2]
    direct_refs = rest[-n_direct - 1:-1]
    xn_ref = rest[-1]
    j = pl.program_id(1)

    @pl.when(j == 0)
    def _():
        x = x_ref[...]
        ms = jnp.mean(x * x, axis=-1, keepdims=True)
        xn_ref[...] = ((x * lax.rsqrt(ms + EPS)) * g_ref[...]).astype(BF16)

    res = jnp.dot(xn_ref[...], w_ref[...], preferred_element_type=F32)
    for k in range(slabs_per_step):
        o_ref[k] = res[:, k * GROUP_W:(k + 1) * GROUP_W]
    for slab, d_ref in zip(direct, direct_refs):
        @pl.when(j == slab // slabs_per_step)
        def _(slab=slab, d_ref=d_ref):
            k = slab % slabs_per_step
            _store_layer_head_major(d_ref, res[:, k * GROUP_W:(k + 1) * GROUP_W], layer, whole)


def _layer_rows_spec(rows, depth, layer, whole, grid_rank):
    if grid_rank == 1:
        if whole:
            return pl.BlockSpec((depth, rows, HEAD_DIM), lambda i: (0, i, 0))
        return pl.BlockSpec((None, rows, HEAD_DIM), lambda i: (layer, i, 0))
    if whole:
        return pl.BlockSpec((depth, rows, HEAD_DIM), lambda i, j: (0, i, 0))
    return pl.BlockSpec((None, rows, HEAD_DIM), lambda i, j: (layer, i, 0))


def in_proj(x2d, g, w_bf16, tm, slabs_per_step, direct=(), direct_bufs=None, layer=0, depth=1):
    m, d = x2d.shape
    n = w_bf16.shape[2]
    ns = n // GROUP_W
    sps = slabs_per_step
    n_direct = len(direct)
    alias_in = list(direct_bufs) if direct_bufs is not None else []
    whole = not alias_in
    direct_shape = jax.ShapeDtypeStruct((depth, m * N_HEADS, HEAD_DIM), F32)
    outs = pl.pallas_call(
        functools.partial(_inproj_kernel, slabs_per_step=sps, direct=tuple(direct), layer=layer,
                          whole=whole),
        grid=(m // tm, ns // sps),
        in_specs=[pl.BlockSpec((tm, d), lambda i, j: (i, 0)),
                  pl.BlockSpec((1, d), lambda i, j: (0, 0)),
                  pl.BlockSpec((None, d, sps * GROUP_W), lambda i, j: (layer, 0, j))]
                 + [pl.BlockSpec(memory_space=pl.ANY)] * len(alias_in),
        out_specs=[pl.BlockSpec((sps, tm, GROUP_W), lambda i, j: (j, i, 0))]
                  + [_layer_rows_spec(tm * N_HEADS, depth, layer, whole, 2)] * n_direct,
        out_shape=[jax.ShapeDtypeStruct((ns, m, GROUP_W), F32)] + [direct_shape] * n_direct,
        input_output_aliases={3 + n: 1 + n for n in range(len(alias_in))},
        scratch_shapes=[pltpu.VMEM((tm, d), BF16)],
        compiler_params=_cparams(("parallel", "arbitrary")),
        name="in_proj",
    )(x2d, g.reshape(1, d), w_bf16, *alias_in)
    return outs[0], tuple(outs[1:])


def _outproj_kernel(ya_ref, yb_ref, yc_ref, yd_ref, w_ref, x_ref, fg_ref, o_ref, *, final):
    acc = x_ref[...]
    for m, y_ref in enumerate((ya_ref, yb_ref, yc_ref, yd_ref)):
        acc = acc + jnp.dot(y_ref[...], w_ref[m * GROUP_W:(m + 1) * GROUP_W, :],
                            preferred_element_type=F32)
    if final:
        ms = jnp.mean(acc * acc, axis=-1, keepdims=True)
        acc = (acc * lax.rsqrt(ms + EPS)) * fg_ref[...]
    o_ref[...] = acc


def out_proj(ys, w_bf16, layer, x2d, final_g, final, tm):
    m, d = x2d.shape
    yspec = pl.BlockSpec((tm, GROUP_W), lambda i: (i, 0))
    return pl.pallas_call(
        functools.partial(_outproj_kernel, final=final),
        grid=(m // tm,),
        in_specs=[yspec, yspec, yspec, yspec,
                  pl.BlockSpec((None, N_MIXERS * GROUP_W, d), lambda i: (layer, 0, 0)),
                  pl.BlockSpec((tm, d), lambda i: (i, 0)),
                  pl.BlockSpec((1, d), lambda i: (0, 0))],
        out_specs=pl.BlockSpec((tm, d), lambda i: (i, 0)),
        out_shape=jax.ShapeDtypeStruct((m, d), F32),
        compiler_params=_cparams(("parallel",)),
        name="out_proj",
    )(*ys, w_bf16, x2d, final_g.reshape(1, d))


def _sb_terms(z):
    neg = -z
    t = jnp.log2(1.0 + jnp.exp2(jnp.minimum(z, neg)))
    return jnp.minimum(z, 0.0) - t, jnp.minimum(neg, 0.0) - t


def _suffix_sums(lk, u):
    hi = lk.astype(BF16)
    lo = (lk - hi.astype(F32)).astype(BF16)
    return (jnp.dot(hi, u, preferred_element_type=F32) + jnp.dot(lo, u, preferred_element_type=F32))


def _strict_lower(n):
    row = lax.broadcasted_iota(jnp.int32, (n, n), 0)
    col = lax.broadcasted_iota(jnp.int32, (n, n), 1)
    return row, col


def _sb_prompt_kernel(q_ref, k_ref, v_ref, g_ref, o_ref, *scratch, tq, tk, hps):
    i = pl.program_id(1)
    acc_refs, carry_refs = scratch[:hps], scratch[hps:]
    n_diag = tq // tk
    row, col = _strict_lower(tk)
    u = jnp.where(row > col, 1.0, 0.0).astype(BF16)
    qs = [q_ref[:, _head_cols(h)].astype(BF16) for h in range(hps)]

    def scores(h, jb):
        start = pl.multiple_of(jb * tk, tk)
        z = _qk(qs[h], k_ref[pl.ds(start, tk), _head_cols(h)]) * SB_SCALE_LOG2
        return z, v_ref[pl.ds(start, tk), _head_cols(h)].astype(BF16)

    def sweep(jb, visible, first):
        for h in range(hps):
            z, v = scores(h, jb)
            ls, lk = _sb_terms(z)
            if visible is not None:
                lk = jnp.where(visible, lk, 0.0)
            e = ls + _suffix_sums(lk, u)
            if not first:
                e = e + carry_refs[h][...]
            a = jnp.exp2(e)
            if visible is not None:
                a = jnp.where(visible, a, 0.0)
            pv = jnp.dot(a.astype(BF16), v, preferred_element_type=F32)
            dc = jnp.sum(lk, axis=1, keepdims=True)
            if first:
                acc_refs[h][...] = pv
                carry_refs[h][...] = dc
            else:
                acc_refs[h][...] += pv
                carry_refs[h][...] += dc

    qrow = lax.broadcasted_iota(jnp.int32, (tq, tk), 0)
    kcol = lax.broadcasted_iota(jnp.int32, (tq, tk), 1)
    for d in reversed(range(n_diag)):
        sweep(i * n_diag + d, kcol + d * tk < qrow, d == n_diag - 1)

    def live():
        return _tree(jnp.maximum, [jnp.max(c[...]) for c in carry_refs]) > SB_EXIT_LOG2

    def cond(state):
        jb, go = state
        return jnp.logical_and(jb >= 0, go)

    def body(state):
        jb, _ = state
        sweep(jb, None, False)
        return jb - 1, live()

    lax.while_loop(cond, body, (i * n_diag - 1, live()))
    for h in range(hps):
        o_ref[:, _head_cols(h)] = (acc_refs[h][...] * _silu(g_ref[:, _head_cols(h)])).astype(o_ref.dtype)


def sb_prompt(proj, tq, tk, heads_per_step):
    _, t, _ = proj.shape
    hps = heads_per_step
    width = hps * HEAD_DIM

    def spec(rows, s, per_q):
        if per_q:
            return pl.BlockSpec((None, rows, width), lambda h, i: (s, i, h))
        return pl.BlockSpec((None, rows, width), lambda h, i: (s, 0, h))

    return pl.pallas_call(
        functools.partial(_sb_prompt_kernel, tq=tq, tk=tk, hps=hps),
        grid=(N_HEADS // hps, t // tq),
        in_specs=[spec(tq, S_SB_Q, True), spec(t, S_SB_K, False), spec(t, S_SB_V, False),
                  spec(tq, S_SB_G, True)],
        out_specs=pl.BlockSpec((tq, width), lambda h, i: (i, h)),
        out_shape=jax.ShapeDtypeStruct((t, GROUP_W), BF16),
        scratch_shapes=[pltpu.VMEM((tq, HEAD_DIM), F32)] * hps + [pltpu.VMEM((tq, 1), F32)] * hps,
        compiler_params=_cparams(("parallel", "arbitrary")),
        name="sb_prompt",
    )(proj, proj, proj, proj)


def _page_masks(rows, rows_per_head, page_rows):
    row = lax.broadcasted_iota(jnp.int32, (rows, page_rows), 0)
    col = lax.broadcasted_iota(jnp.int32, (rows, page_rows), 1)
    own = (col & (N_HEADS - 1)) == (row >> _log2(rows_per_head))
    return own, col >> _log2(N_HEADS), row & (PAD_T - 1)


def _sb_decode_kernel(pt_ref, q_ref, kn_ref, vn_ref, g_ref, u_ref, ck_ref, cv_ref, o_ref,
                      kbuf, vbuf, newk, newv, sem, acc_ref, carry_ref, *, layer, n_pages, n_tok):
    b = pl.program_id(0)
    rows = N_HEADS * PAD_T
    page_rows = newk.shape[0]

    def page_copies(p, slot):
        pg = pt_ref[b, p]
        return (pltpu.make_async_copy(ck_ref.at[layer, pg], kbuf.at[slot], sem.at[0, slot]),
                pltpu.make_async_copy(cv_ref.at[layer, pg], vbuf.at[slot], sem.at[1, slot]))

    def start_fetch(p, slot):
        for c in page_copies(p, slot):
            c.start()

    def wait_fetch(p, slot):
        for c in page_copies(p, slot):
            c.wait()

    start_fetch(n_pages - 1, 0)

    qall = jnp.concatenate([q_ref[h] for h in range(N_HEADS)], axis=0).astype(BF16)
    own, key, tok = _page_masks(rows, PAD_T, page_rows)
    real_row = (lax.broadcasted_iota(jnp.int32, (rows, 1), 0) & (PAD_T - 1)) < n_tok
    u = u_ref[...]

    def attend(k, v, visible, carry):
        ls, lk = _sb_terms(_qk(qall, k) * SB_SCALE_LOG2)
        lk = jnp.where(visible, lk, 0.0)
        e = ls + _suffix_sums(lk, u)
        if carry is not None:
            e = e + carry
        a = jnp.where(visible, jnp.exp2(e), 0.0).astype(BF16)
        return (jnp.dot(a, v.astype(BF16), preferred_element_type=F32),
                jnp.sum(lk, axis=1, keepdims=True))

    def live(carry):
        return jnp.max(jnp.where(real_row, carry, -jnp.inf)) > SB_EXIT_LOG2

    newk[...] = jnp.zeros_like(newk)
    newv[...] = jnp.zeros_like(newv)
    newk[0:kn_ref.shape[0], :] = kn_ref[...]
    newv[0:vn_ref.shape[0], :] = vn_ref[...]
    pv, dc = attend(newk[...], newv[...], jnp.logical_and(own, key < tok), None)
    acc_ref[...] = pv
    carry_ref[...] = dc

    def cond(state):
        p, go = state
        return jnp.logical_and(p >= 0, go)

    def body(state):
        p, _ = state
        slot = (n_pages - 1 - p) % 2
        wait_fetch(p, slot)

        @pl.when(p >= 1)
        def _():
            start_fetch(p - 1, 1 - slot)

        carry = carry_ref[...]
        pv, dc = attend(kbuf[slot], vbuf[slot], own, carry)
        acc_ref[...] += pv
        carry = carry + dc
        carry_ref[...] = carry
        return p - 1, live(carry)

    p_end, _ = lax.while_loop(cond, body, (jnp.int32(n_pages - 1), live(carry_ref[...])))

    @pl.when(p_end >= 0)
    def _():
        wait_fetch(p_end, (n_pages - 1 - p_end) % 2)

    acc = acc_ref[...]
    o_ref[...] = jnp.concatenate([acc[h * PAD_T:(h + 1) * PAD_T] * _silu(g_ref[h])
                                  for h in range(N_HEADS)], axis=1)


def sb_decode(q8, kn, vn, g8, cache_k, cache_v, page_table, layer, n_tok):
    bsz = page_table.shape[0]
    n_pages = page_table.shape[1]
    page_rows = cache_k.shape[2]
    tok_spec = pl.BlockSpec((None, PAD_T, GROUP_W), lambda b, pt: (b, 0, 0))
    head_spec = pl.BlockSpec((N_HEADS, PAD_T, HEAD_DIM), lambda b, pt: (0, b, 0))
    new_spec = pl.BlockSpec((None, kn.shape[1], HEAD_DIM), lambda b, pt: (b, 0, 0))
    any_spec = pl.BlockSpec(memory_space=pl.ANY)
    rows = N_HEADS * PAD_T
    row, col = _strict_lower(page_rows)
    u = jnp.where(row > col, 1.0, 0.0).astype(BF16)
    page_buf = pltpu.VMEM((2, page_rows, HEAD_DIM), F32)
    new_buf = pltpu.VMEM((page_rows, HEAD_DIM), F32)
    return pl.pallas_call(
        functools.partial(_sb_decode_kernel, layer=layer, n_pages=n_pages, n_tok=n_tok),
        grid_spec=pltpu.PrefetchScalarGridSpec(
            num_scalar_prefetch=1,
            grid=(bsz,),
            in_specs=[head_spec, new_spec, new_spec, head_spec,
                      pl.BlockSpec((page_rows, page_rows), lambda b, pt: (0, 0)), any_spec, any_spec],
            out_specs=tok_spec,
            scratch_shapes=[page_buf, page_buf, new_buf, new_buf,
                            pltpu.SemaphoreType.DMA((2, 2)),
                            pltpu.VMEM((rows, HEAD_DIM), F32), pltpu.VMEM((rows, 1), F32)]),
        out_shape=jax.ShapeDtypeStruct((bsz, PAD_T, GROUP_W), F32),
        compiler_params=_cparams(("arbitrary",)),
        name="sb_decode",
    )(page_table, q8, kn, vn, g8, u, cache_k, cache_v)


def _lambda(lq1_ref, lk1_ref, lq2_ref, lk2_ref, lam_init):
    s1 = jnp.sum(lq1_ref[...] * lk1_ref[...], axis=-1, keepdims=True)
    s2 = jnp.sum(lq2_ref[...] * lk2_ref[...], axis=-1, keepdims=True)
    return jnp.exp(s1) - jnp.exp(s2) + lam_init


def _diff_finish(o1, o2, lam, sg, gate, lam_init):
    w = o1 - lam * o2
    ms = jnp.mean(w * w, axis=-1, keepdims=True)
    y = (w * lax.rsqrt(ms + EPS)) * sg
    return (y * (1.0 - lam_init)) * _silu(gate)


def _split_components(q):
    lane = lax.broadcasted_iota(jnp.int32, q.shape, 1)
    zero = jnp.zeros_like(q)
    return jnp.concatenate([jnp.where(lane < DIFF_QK, q, zero), jnp.where(lane >= DIFF_QK, q, zero)],
                           axis=0)


def _diff_prompt_kernel(q_ref, k_ref, v_ref, g_ref, lq1_ref, lk1_ref, lq2_ref, lk2_ref, sg_ref,
                        o_ref, s0_ref, s1_ref, mt0_ref, mt1_ref, m_ref, acc_ref, *, tb, lam_init):
    i = pl.program_id(1)
    qq = _split_components(q_ref[...])
    m_ref[...] = jnp.full_like(m_ref, -jnp.inf)
    acc_ref[...] = jnp.zeros_like(acc_ref)
    bufs = ((s0_ref, mt0_ref), (s1_ref, mt1_ref))

    def score_stage(jb, slot, masked):
        s_ref, mt_ref = bufs[slot]
        start = pl.multiple_of(jb * tb, tb)
        s = _qk(qq, k_ref[pl.ds(start, tb), :])
        if masked:
            row = lax.broadcasted_iota(jnp.int32, s.shape, 0) & (tb - 1)
            col = lax.broadcasted_iota(jnp.int32, s.shape, 1)
            s = jnp.where(col <= row, s, -jnp.inf)
        s_ref[...] = s
        mt_ref[...] = jnp.broadcast_to(jnp.max(s, axis=1, keepdims=True), mt_ref.shape)

    def value_stage(jb, slot):
        s_ref, mt_ref = bufs[slot]
        m_old = m_ref[...]
        m_new = jnp.maximum(m_old, mt_ref[...])
        alpha = jnp.exp2(m_old - m_new)
        p = jnp.concatenate([jnp.exp2(c - m_new) for c in _lane_chunks(s_ref[...])], axis=1)
        start = pl.multiple_of(jb * tb, tb)
        pv = jnp.dot(p.astype(BF16), _with_ones(v_ref[pl.ds(start, tb), :]), preferred_element_type=F32)
        acc_ref[...] = jnp.concatenate([alpha, alpha], axis=1) * acc_ref[...] + pv
        m_ref[...] = m_new

    @pl.when(i == 0)
    def _():
        score_stage(0, 0, True)
        value_stage(0, 0)

    @pl.when(i > 0)
    def _():
        score_stage(0, 0, False)
        pairs = (i - 1) // 2

        def body(g, c):
            score_stage(2 * g + 1, 1, False)
            value_stage(2 * g, 0)
            score_stage(2 * g + 2, 0, False)
            value_stage(2 * g + 1, 1)
            return c

        lax.fori_loop(0, pairs, body, 0)
        t0 = 2 * pairs

        @pl.when(i - t0 == 1)
        def _():
            score_stage(i, 1, True)
            value_stage(t0, 0)
            value_stage(i, 1)

        @pl.when(i - t0 == 2)
        def _():
            score_stage(t0 + 1, 1, False)
            value_stage(t0, 0)
            score_stage(i, 0, True)
            value_stage(t0 + 1, 1)
            value_stage(i, 0)

    acc = acc_ref[...]
    o = acc[:, :HEAD_DIM] / acc[:, HEAD_DIM:]
    lam = _lambda(lq1_ref, lk1_ref, lq2_ref, lk2_ref, lam_init)
    o_ref[...] = _diff_finish(o[:tb], o[tb:], lam, sg_ref[...], g_ref[...], lam_init).astype(o_ref.dtype)


def diff_prompt(qb, kb, vb, proj, lam_params, subln_g, lam_init, tb):
    t = qb.shape[0]
    small = pl.BlockSpec((1, DIFF_QK), lambda h, i: (0, 0))
    return pl.pallas_call(
        functools.partial(_diff_prompt_kernel, tb=tb, lam_init=lam_init),
        grid=(N_HEADS, t // tb),
        in_specs=[pl.BlockSpec((tb, HEAD_DIM), lambda h, i: (i, h)),
                  pl.BlockSpec((t, HEAD_DIM), lambda h, i: (0, h)),
                  pl.BlockSpec((t, HEAD_DIM), lambda h, i: (0, h)),
                  pl.BlockSpec((None, tb, HEAD_DIM), lambda h, i: (S_DF_G, i, h)),
                  small, small, small, small,
                  pl.BlockSpec((1, HEAD_DIM), lambda h, i: (0, 0))],
        out_specs=pl.BlockSpec((tb, HEAD_DIM), lambda h, i: (i, h)),
        out_shape=jax.ShapeDtypeStruct((t, GROUP_W), BF16),
        scratch_shapes=[pltpu.VMEM((2 * tb, tb), F32), pltpu.VMEM((2 * tb, tb), F32),
                        pltpu.VMEM((2 * tb, HEAD_DIM), F32), pltpu.VMEM((2 * tb, HEAD_DIM), F32),
                        pltpu.VMEM((2 * tb, HEAD_DIM), F32), pltpu.VMEM((2 * tb, 2 * HEAD_DIM), F32)],
        compiler_params=_cparams(("parallel", "arbitrary")),
        name="diff_prompt",
    )(qb, kb, vb, proj, *[p.reshape(1, DIFF_QK) for p in lam_params], subln_g.reshape(1, HEAD_DIM))


def _diff_decode_kernel(pt_ref, q_ref, kn_ref, vn_ref, g_ref, lq1_ref, lk1_ref, lq2_ref, lk2_ref,
                        sg_ref, ck_ref, cv_ref, o_ref, kbuf, vbuf, sem, m_ref, acc_ref, newk, newv, *,
                        layer, pages_per_step, group, lam_init):
    b = pl.program_id(0)
    c = pl.program_id(1)
    n_chunks = pl.num_programs(1)
    hrows = 2 * PAD_T
    rows = N_HEADS * hrows
    page_rows = newk.shape[0]

    step = b * n_chunks + c
    slot = step % 2

    def chunk_copies(bb, cc, sl):
        copies = []
        for n in range(pages_per_step):
            pg = pt_ref[bb, cc * pages_per_step + n]
            copies.append(pltpu.make_async_copy(ck_ref.at[layer, pg], kbuf.at[sl, n], sem.at[sl, 0, n]))
            copies.append(pltpu.make_async_copy(cv_ref.at[layer, pg], vbuf.at[sl, n], sem.at[sl, 1, n]))
        return copies

    @pl.when(step == 0)
    def _():
        for cp in chunk_copies(b, c, slot):
            cp.start()

    @pl.when(step + 1 < pl.num_programs(0) * n_chunks)
    def _():
        wrap = c + 1 == n_chunks
        for cp in chunk_copies(jnp.where(wrap, b + 1, b), jnp.where(wrap, 0, c + 1), 1 - slot):
            cp.start()

    for cp in chunk_copies(b, c, slot):
        cp.wait()

    qall = jnp.concatenate([_split_components(q_ref[h] * (DIFF_SCALE * LOG2E))
                            for h in range(N_HEADS)], axis=0).astype(BF16)
    zq = jnp.zeros_like(qall)
    qpair = jnp.concatenate([jnp.concatenate([qall, zq], axis=1),
                             jnp.concatenate([zq, qall], axis=1)], axis=0)
    own, key, tok = _page_masks(rows, hrows, page_rows)

    @pl.when(c == 0)
    def _():
        m_ref[...] = jnp.full_like(m_ref, -jnp.inf)
        acc_ref[...] = jnp.zeros_like(acc_ref)

    def page_scores(ks):
        out = []
        for n in range(0, len(ks) - 1, 2):
            kk = jnp.concatenate([ks[n].astype(BF16), ks[n + 1].astype(BF16)], axis=1)
            s2 = _qk(qpair, kk)
            out += [s2[:rows], s2[rows:]]
        if len(ks) % 2:
            out.append(_qk(qall, ks[-1]))
        return out

    def update(ks, vs, visible):
        s = jnp.concatenate([jnp.where(visible, z, -jnp.inf) for z in page_scores(ks)], axis=1)
        m_prev = m_ref[...]
        m_new = jnp.maximum(m_prev, jnp.max(s, axis=1, keepdims=True))
        p = jnp.exp2(s - m_new).astype(BF16)
        pv = _tree(jnp.add, [jnp.dot(p[:, n * page_rows:(n + 1) * page_rows], _with_ones(v),
                                     preferred_element_type=F32) for n, v in enumerate(vs)])
        return m_new, jnp.exp2(m_prev - m_new) * acc_ref[...] + pv

    for g0 in range(0, pages_per_step, group):
        m_new, acc_new = update([kbuf[slot, n] for n in range(g0, g0 + group)],
                                [vbuf[slot, n] for n in range(g0, g0 + group)], own)
        m_ref[...] = m_new
        acc_ref[...] = acc_new

    @pl.when(c == pl.num_programs(1) - 1)
    def _():
        newk[...] = jnp.zeros_like(newk)
        newv[...] = jnp.zeros_like(newv)
        newk[0:kn_ref.shape[0], :] = kn_ref[...]
        newv[0:vn_ref.shape[0], :] = vn_ref[...]
        _, acc = update([newk[...]], [newv[...]], jnp.logical_and(own, key <= tok))
        o = acc[:, :HEAD_DIM] / acc[:, HEAD_DIM:]
        lam = _lambda(lq1_ref, lk1_ref, lq2_ref, lk2_ref, lam_init)
        sg = sg_ref[...]
        outs = []
        for h in range(N_HEADS):
            o1 = o[h * hrows:h * hrows + PAD_T]
            o2 = o[h * hrows + PAD_T:(h + 1) * hrows]
            outs.append(_diff_finish(o1, o2, lam, sg, g_ref[h], lam_init))
        o_ref[...] = jnp.concatenate(outs, axis=1)


def diff_decode(q8, kn, vn, g8, cache_k, cache_v, page_table, lam_params, subln_g, lam_init,
                layer, pages_per_step, group):
    bsz = page_table.shape[0]
    n_pages = page_table.shape[1]
    page_rows = cache_k.shape[2]
    pps = pages_per_step
    tok_spec = pl.BlockSpec((None, PAD_T, GROUP_W), lambda b, c, pt: (b, 0, 0))
    head_spec = pl.BlockSpec((N_HEADS, PAD_T, HEAD_DIM), lambda b, c, pt: (0, b, 0))
    new_spec = pl.BlockSpec((None, kn.shape[1], HEAD_DIM), lambda b, c, pt: (b, 0, 0))
    small = pl.BlockSpec((1, DIFF_QK), lambda b, c, pt: (0, 0))

    rows = N_HEADS * 2 * PAD_T
    any_spec = pl.BlockSpec(memory_space=pl.ANY)
    page_bufs = pltpu.VMEM((2, pps, page_rows, HEAD_DIM), F32)
    new_buf = pltpu.VMEM((page_rows, HEAD_DIM), F32)
    return pl.pallas_call(
        functools.partial(_diff_decode_kernel, layer=layer, pages_per_step=pps, group=group,
                          lam_init=lam_init),
        grid_spec=pltpu.PrefetchScalarGridSpec(
            num_scalar_prefetch=1,
            grid=(bsz, n_pages // pps),
            in_specs=[head_spec, new_spec, new_spec, head_spec, small, small, small, small,
                      pl.BlockSpec((1, HEAD_DIM), lambda b, c, pt: (0, 0)), any_spec, any_spec],
            out_specs=tok_spec,
            scratch_shapes=[page_bufs, page_bufs, pltpu.SemaphoreType.DMA((2, 2, pps)),
                            pltpu.VMEM((rows, 1), F32), pltpu.VMEM((rows, 2 * HEAD_DIM), F32),
                            new_buf, new_buf]),
        out_shape=jax.ShapeDtypeStruct((bsz, PAD_T, GROUP_W), F32),
        compiler_params=_cparams(("arbitrary", "arbitrary")),
        name="diff_decode",
    )(page_table, q8, kn, vn, g8, *[p.reshape(1, DIFF_QK) for p in lam_params],
      subln_g.reshape(1, HEAD_DIM), cache_k, cache_v)


CONF_HALO = 32
SC_HALO = 8
SUBLANES = 8
MIX_ROWS = 32


def _rope_tables(pos):
    half = ROT_DIM // 2
    inv = ROPE_THETA ** (-2.0 * jnp.arange(half, dtype=F32) / ROT_DIM)
    ang = pos.astype(F32)[:, None] * inv[None, :]
    cos, sin = jnp.cos(ang), jnp.sin(ang)
    j = jnp.arange(HEAD_DIM) % DIFF_QK
    f = j % half
    c = jnp.where(j < ROT_DIM, cos[:, f], 1.0)
    sa = jnp.where((j >= half) & (j < ROT_DIM), sin[:, f], 0.0)
    sb = jnp.where(j < half, -sin[:, f], 0.0)
    return c, sa, sb


def _rope_rows(x, c, sa, sb):
    half = ROT_DIM // 2
    out = []
    for xh in _lane_chunks(x):
        out.append(xh * c + pltpu.roll(xh, half, 1) * sa + pltpu.roll(xh, HEAD_DIM - half, 1) * sb)
    return jnp.concatenate(out, axis=1)


def _prompt_mix_kernel(a_ref, b_ref, g_ref, dq_ref, dk_ref, dv_ref, sh_ref, sb_ref, sc_ref, sg_ref,
                       cw_ref, cb_ref, lg_ref, lb_ref, sw_ref, rc_ref, rsa_ref, rsb_ref, *rest,
                       tb, conf_w, sc_w, layer, whole):
    ya_ref, yd_ref, dqb_ref, dk_out_ref, dkb_ref, dvb_ref, nconf_ref, nsc_ref, hb, hs, ub = rest[-11:]
    i = pl.program_id(0)

    @pl.when(i == 0)
    def _():
        hb[0:CONF_HALO, :] = jnp.zeros((CONF_HALO, GROUP_W), F32)
        ub[0:SC_HALO, :] = jnp.zeros((SC_HALO, GROUP_W), F32)

    @pl.when(i > 0)
    def _():
        hb[0:CONF_HALO, :] = hb[tb:tb + CONF_HALO, :]
        ub[0:SC_HALO, :] = ub[tb:tb + SC_HALO, :]

    hb[CONF_HALO:CONF_HALO + tb, :] = a_ref[...] * (1.0 / (1.0 + jnp.exp(-b_ref[...])))
    ub[SC_HALO:SC_HALO + tb, :] = sc_ref[...] * sh_ref[...]

    off = CONF_HALO - (conf_w - 1)
    span = hs.shape[1]
    for k in range(1, SUBLANES):
        hs[k - 1, :, :] = hb[k:k + span, :]

    def tap(w_ref, w):
        return jnp.concatenate([w_ref[w]] * (MIX_ROWS // SUBLANES), axis=0)

    def window(r):
        k, base = r % SUBLANES, r - r % SUBLANES
        if k == 0:
            return hb[base:base + MIX_ROWS, :]
        return hs[k - 1, base:base + MIX_ROWS, :]

    for r0 in range(0, tb, MIX_ROWS):
        rows = slice(r0, r0 + MIX_ROWS)
        acc = jnp.zeros((MIX_ROWS, GROUP_W), F32) + cb_ref[...]
        for w in range(conf_w):
            acc = acc + tap(cw_ref, w) * window(r0 + off + w)
        mu = jnp.mean(acc, axis=-1, keepdims=True)
        xc = acc - mu
        y = (xc * lax.rsqrt(jnp.mean(xc * xc, axis=-1, keepdims=True) + EPS)) * lg_ref[...] + lb_ref[...]
        ya_ref[rows, :] = (_silu(y) * _silu(g_ref[rows, :])).astype(ya_ref.dtype)
        u0 = r0 + SC_HALO - (sc_w - 1)
        conv = jnp.zeros((MIX_ROWS, GROUP_W), F32)
        for w in range(sc_w):
            conv = conv + tap(sw_ref, w) * ub[u0 + w:u0 + w + MIX_ROWS, :]
        yd_ref[rows, :] = ((sb_ref[rows, :] * conv) * _silu(sg_ref[rows, :])).astype(yd_ref.dtype)

    c, sa, sb = rc_ref[...], rsa_ref[...], rsb_ref[...]
    dqb_ref[...] = (_rope_rows(dq_ref[...], c, sa, sb) * (DIFF_SCALE * LOG2E)).astype(dqb_ref.dtype)
    dk = _rope_rows(dk_ref[...], c, sa, sb)
    _store_layer_head_major(dk_out_ref, dk, layer, whole)
    dkb_ref[...] = dk.astype(dkb_ref.dtype)
    dvb_ref[...] = dv_ref[...].astype(dvb_ref.dtype)

    @pl.when(i == pl.num_programs(0) - 1)
    def _():
        nconf_ref[...] = hb[CONF_HALO + tb - (conf_w - 1):CONF_HALO + tb, :]
        nsc_ref[...] = ub[SC_HALO + tb - (sc_w - 1):SC_HALO + tb, :]


def prompt_mix(proj, conf_w, conf_b, ln_g, ln_b, sc_w, rope_tabs, tb, k_buf=None, layer=0, depth=1):
    _, t, _ = proj.shape
    cw, scw = conf_w.shape[0], sc_w.shape[0]

    def slab(s):
        return pl.BlockSpec((None, tb, GROUP_W), lambda i: (s, i, 0))

    def full(r):
        return pl.BlockSpec((r, GROUP_W), lambda i: (0, 0))

    def taps(n):
        return pl.BlockSpec((n, SUBLANES, GROUP_W), lambda i: (0, 0, 0))

    def replicate(w):
        return jnp.broadcast_to(w[:, None, :], (w.shape[0], SUBLANES, w.shape[1]))

    row_blk = pl.BlockSpec((tb, GROUP_W), lambda i: (i, 0))
    tab_blk = pl.BlockSpec((tb, HEAD_DIM), lambda i: (i, 0))
    slabs = (S_CONF_A, S_CONF_B, S_CONF_G, S_DF_Q, S_DF_K, S_DF_V, S_SC_H, S_SC_B, S_SC_C, S_SC_G)
    act = jax.ShapeDtypeStruct((t, GROUP_W), BF16)
    alias_in = [] if k_buf is None else [k_buf]
    n_in = len(slabs) + 8
    return pl.pallas_call(
        functools.partial(_prompt_mix_kernel, tb=tb, conf_w=cw, sc_w=scw, layer=layer,
                          whole=not alias_in),
        grid=(t // tb,),
        in_specs=[slab(s) for s in slabs]
                 + [taps(cw), full(1), full(1), full(1), taps(scw), tab_blk, tab_blk, tab_blk]
                 + [pl.BlockSpec(memory_space=pl.ANY)] * len(alias_in),
        out_specs=[row_blk, row_blk, row_blk,
                   _layer_rows_spec(tb * N_HEADS, depth, layer, not alias_in, 1),
                   row_blk, row_blk, full(cw - 1), full(scw - 1)],
        out_shape=[act, act, act, jax.ShapeDtypeStruct((depth, t * N_HEADS, HEAD_DIM), F32), act, act,
                   jax.ShapeDtypeStruct((cw - 1, GROUP_W), F32),
                   jax.ShapeDtypeStruct((scw - 1, GROUP_W), F32)],
        scratch_shapes=[pltpu.VMEM((CONF_HALO + tb, GROUP_W), F32),
                        pltpu.VMEM((SUBLANES - 1, CONF_HALO + tb - SUBLANES, GROUP_W), F32),
                        pltpu.VMEM((SC_HALO + tb, GROUP_W), F32)],
        input_output_aliases={n_in: 3} if alias_in else {},
        compiler_params=_cparams(("arbitrary",)),
        name="prompt_mix",
    )(*([proj] * len(slabs)), replicate(conf_w), conf_b.reshape(1, -1), ln_g.reshape(1, -1),
      ln_b.reshape(1, -1), replicate(sc_w), *rope_tabs, *alias_in)


def _decode_mix_kernel(proj_ref, sconf_ref, ssc_ref, cw_ref, cb_ref, lg_ref, lb_ref, sw_ref,
                       rc_ref, rsa_ref, rsb_ref,
                       ya_ref, yd_ref, sbq_ref, sbg_ref, dfq_ref, dfg_ref, sbk_ref, sbv_ref, dfk_ref,
                       dfv_ref, nconf_ref, nsc_ref, *, bsz, n_tok, conf_w, sc_w):
    def tok(slab, t):
        return proj_ref[slab, t * bsz:(t + 1) * bsz, :]

    def put_tok(ref, t, x):
        ref[t * bsz:(t + 1) * bsz, :] = x.astype(ref.dtype)

    def put_pad(ref, t, x):
        for h in range(N_HEADS):
            ref[h, pl.ds(t, bsz, stride=PAD_T), :] = x[:, _head_cols(h)]

    def put_heads(ref, t, x):
        for h in range(N_HEADS):
            ref[pl.ds(t * N_HEADS + h, bsz, stride=n_tok * N_HEADS), :] = x[:, _head_cols(h)]

    def tap(w_ref, w):
        if bsz % SUBLANES:
            return jnp.broadcast_to(w_ref[w, 0:1, :], (bsz, GROUP_W))
        return jnp.concatenate([w_ref[w]] * (bsz // SUBLANES), axis=0)

    h = [tok(S_CONF_A, t) * (1.0 / (1.0 + jnp.exp(-tok(S_CONF_B, t)))) for t in range(n_tok)]

    def hist(r):
        return sconf_ref[r] if r < conf_w - 1 else h[r - (conf_w - 1)]

    for t in range(n_tok):
        acc = jnp.zeros((bsz, GROUP_W), F32) + cb_ref[...]
        for w in range(conf_w):
            acc = acc + tap(cw_ref, w) * hist(t + w)
        mu = jnp.mean(acc, axis=-1, keepdims=True)
        xc = acc - mu
        y = (xc * lax.rsqrt(jnp.mean(xc * xc, axis=-1, keepdims=True) + EPS)) * lg_ref[...] + lb_ref[...]
        put_tok(ya_ref, t, _silu(y) * _silu(tok(S_CONF_G, t)))
    for r in range(conf_w - 1):
        nconf_ref[r] = hist(r + n_tok)

    u = [tok(S_SC_C, t) * tok(S_SC_H, t) for t in range(n_tok)]

    def uhist(r):
        return ssc_ref[r] if r < sc_w - 1 else u[r - (sc_w - 1)]

    for t in range(n_tok):
        conv = jnp.zeros((bsz, GROUP_W), F32)
        for w in range(sc_w):
            conv = conv + tap(sw_ref, w) * uhist(t + w)
        put_tok(yd_ref, t, (tok(S_SC_B, t) * conv) * _silu(tok(S_SC_G, t)))
    for r in range(sc_w - 1):
        nsc_ref[r] = uhist(r + n_tok)

    for ref in (sbq_ref, sbg_ref, dfq_ref, dfg_ref):
        ref[...] = jnp.zeros_like(ref)
    for t in range(n_tok):
        tabs = [jnp.broadcast_to(r[t:t + 1, :], (bsz, HEAD_DIM)) for r in (rc_ref, rsa_ref, rsb_ref)]
        put_pad(sbq_ref, t, tok(S_SB_Q, t))
        put_pad(sbg_ref, t, tok(S_SB_G, t))
        put_pad(dfq_ref, t, _rope_rows(tok(S_DF_Q, t), *tabs))
        put_pad(dfg_ref, t, tok(S_DF_G, t))
        put_heads(sbk_ref, t, tok(S_SB_K, t))
        put_heads(sbv_ref, t, tok(S_SB_V, t))
        put_heads(dfk_ref, t, _rope_rows(tok(S_DF_K, t), *tabs))
        put_heads(dfv_ref, t, tok(S_DF_V, t))


def decode_mix(proj, state_conf, state_sc, conf_w, conf_b, ln_g, ln_b, sc_w, rope_tabs, bsz, n_tok):
    cw, scw = conf_w.shape[0], sc_w.shape[0]

    def replicate(w):
        return jnp.broadcast_to(w[:, None, :], (w.shape[0], SUBLANES, w.shape[1]))

    act = jax.ShapeDtypeStruct((bsz * n_tok, GROUP_W), BF16)
    pad = jax.ShapeDtypeStruct((N_HEADS, bsz * PAD_T, HEAD_DIM), F32)
    hm = jax.ShapeDtypeStruct((bsz * n_tok * N_HEADS, HEAD_DIM), F32)
    return pl.pallas_call(
        functools.partial(_decode_mix_kernel, bsz=bsz, n_tok=n_tok, conf_w=cw, sc_w=scw),
        out_shape=[act, act, pad, pad, pad, pad, hm, hm, hm, hm,
                   jax.ShapeDtypeStruct(state_conf.shape, F32), jax.ShapeDtypeStruct(state_sc.shape, F32)],
        compiler_params=pltpu.CompilerParams(vmem_limit_bytes=VMEM_LIMIT),
        name="decode_mix",
    )(proj, state_conf, state_sc, replicate(conf_w), conf_b.reshape(1, -1), ln_g.reshape(1, -1),
      ln_b.reshape(1, -1), replicate(sc_w), *rope_tabs)


def _pick_tile(n, pref):
    t = min(n, pref)
    while n % t:
        t //= 2
    return t


def kernel(x_prompt, x_sample, cache_sb_k, cache_sb_v, cache_diff_k, cache_diff_v, state_conf_conv,
           state_short_conv, page_table, norm_g, w_in, conf_dw_w, conf_dw_b, conf_ln_g, conf_ln_b,
           diff_lq1, diff_lk1, diff_lq2, diff_lk2, diff_subln_g, sc_conv_w, w_out, final_norm_g):
    bp, tp, d = x_prompt.shape
    bs, ts, _ = x_sample.shape
    depth = w_in.shape[0]
    n_pages = page_table.shape[1]
    page = cache_sb_k.shape[2]
    n_past = n_pages * page
    pos_p = jnp.arange(tp, dtype=jnp.int32)
    pos_s = n_past + jnp.arange(ts, dtype=jnp.int32)
    assert bp == 1 and ts <= PAD_T

    caches = [c.reshape(c.shape[0], c.shape[1], page * N_HEADS, HEAD_DIM)
              for c in (cache_sb_k, cache_sb_v, cache_diff_k, cache_diff_v)]
    w_in_b = w_in.astype(BF16)
    w_out_b = w_out.astype(BF16)

    hp = x_prompt.reshape(bp * tp, d)
    hs = jnp.swapaxes(x_sample, 0, 1).reshape(ts * bs, d)
    conf_states = jnp.swapaxes(state_conf_conv, 1, 2)
    sc_states = jnp.swapaxes(state_short_conv, 1, 2)
    tm_p = _pick_tile(bp * tp, 512)
    slabs_per_step = 5
    assert N_IN_SLICES % slabs_per_step == 0
    tm_s = _pick_tile(bs * ts, 128)
    tq_sb = _pick_tile(tp, 512)
    tk_sb = _pick_tile(tq_sb, 256)
    tb_df = _pick_tile(tp, 512)
    tb_mix = _pick_tile(tp, 256)
    assert tb_mix >= CONF_HALO >= conf_dw_w.shape[1] - 1 and SC_HALO >= sc_conv_w.shape[1] - 1
    rope_tabs_p = _rope_tables(pos_p)
    rope_tabs_s = _rope_tables(pos_s)
    pps = _pick_tile(n_pages, 16)
    pgroup = _pick_tile(pps, 2)

    p_conv = [[], []]
    s_new = [[] for _ in range(6)]
    kv_bufs = None
    dk_buf = None
    for l in range(depth):
        lam_init = 0.8 - 0.6 * math.exp(-0.3 * l)
        lam_params = (diff_lq1[l], diff_lk1[l], diff_lq2[l], diff_lk2[l])
        final = l == depth - 1

        proj, kv_bufs = in_proj(hp, norm_g[l], w_in_b, tm_p, slabs_per_step,
                                direct=(S_SB_K, S_SB_V, S_DF_V), direct_bufs=kv_bufs, layer=l,
                                depth=depth)
        ya, yd, dqb, dk_buf, dkb, dvb, new_conf, new_sc = prompt_mix(
            proj, conf_dw_w[l], conf_dw_b[l], conf_ln_g[l], conf_ln_b[l], sc_conv_w[l], rope_tabs_p,
            tb_mix, k_buf=dk_buf, layer=l, depth=depth)
        yb = sb_prompt(proj, tq_sb, tk_sb, 2)
        yc = diff_prompt(dqb, dkb, dvb, proj, lam_params, diff_subln_g[l], lam_init, tb_df)
        hp = out_proj((ya, yb, yc, yd), w_out_b, l, hp, final_norm_g, final,
                      _pick_tile(bp * tp, 256))
        p_conv[0].append(new_conf[None])
        p_conv[1].append(new_sc[None])
        heads = lambda a, bb, tt: a.reshape(bb, tt, N_HEADS, HEAD_DIM)

        proj, _ = in_proj(hs, norm_g[l], w_in_b, tm_s, slabs_per_step, layer=l)
        ya, yd, sbq, sbg, dfq, dfg, sbk, sbv, dfk, dfv, new_conf, new_sc = decode_mix(
            proj, conf_states[l], sc_states[l], conf_dw_w[l], conf_dw_b[l], conf_ln_g[l],
            conf_ln_b[l], sc_conv_w[l], rope_tabs_s, bs, ts)
        new_rows = lambda a: a.reshape(bs, ts * N_HEADS, HEAD_DIM)
        yb = sb_decode(sbq, new_rows(sbk), new_rows(sbv), sbg, caches[0], caches[1], page_table, l, ts)
        yc = diff_decode(dfq, new_rows(dfk), new_rows(dfv), dfg, caches[2], caches[3], page_table,
                         lam_params, diff_subln_g[l], lam_init, l, pps, pgroup)
        flat = lambda a: jnp.swapaxes(a[:, :ts], 0, 1).reshape(ts * bs, GROUP_W).astype(BF16)
        hs = out_proj((ya, flat(yb), flat(yc), yd), w_out_b, l, hs, final_norm_g, final, tm_s)
        for n, a in enumerate((sbk, sbv, dfk, dfv)):
            s_new[n].append(a.reshape(bs, ts, N_HEADS, HEAD_DIM))
        s_new[4].append(new_conf)
        s_new[5].append(new_sc)

    y_prompt = hp.reshape(bp, tp, d)
    y_sample = jnp.swapaxes(hs.reshape(ts, bs, d), 0, 1)
    p_kv = [a.reshape(depth, bp, tp, N_HEADS, HEAD_DIM)
            for a in (kv_bufs[0], kv_bufs[1], dk_buf, kv_bufs[2])]
    s_out = [jnp.stack(a, 0) for a in s_new]
    s_out[4:] = [jnp.swapaxes(a, 1, 2) for a in s_out[4:]]
    return (y_prompt, y_sample, *p_kv, *[jnp.stack(a, 0) for a in p_conv], *s_out)
```

```python
import functools
import math

import jax
import jax.numpy as jnp
from jax import lax
from jax.experimental import pallas as pl
from jax.experimental.pallas import tpu as pltpu

F32 = jnp.float32
BF16 = jnp.bfloat16

N_MIXERS = 4
N_HEADS = 4
HEAD_DIM = 128
GROUP_W = N_HEADS * HEAD_DIM
DIFF_QK = HEAD_DIM // 2
ROT_DIM = DIFF_QK // 4
ROPE_THETA = 500000.0
N_IN_SLICES = 15
EPS = 1e-6
LOG2E = 1.4426950408889634
SB_SCALE_LOG2 = LOG2E / math.sqrt(HEAD_DIM)
DIFF_SCALE = 1.0 / math.sqrt(DIFF_QK)
PAD_T = 8
SB_EXIT_LOG2 = -104.0 * LOG2E
VMEM_LIMIT = 56 * 1024 * 1024

(S_CONF_A, S_CONF_B, S_CONF_G, S_SB_Q, S_SB_K, S_SB_V, S_SB_G, S_DF_Q, S_DF_K, S_DF_V, S_DF_G,
 S_SC_H, S_SC_B, S_SC_C, S_SC_G) = range(N_IN_SLICES)


def _silu(x):
    return x * (1.0 / (1.0 + jnp.exp(-x)))


def _cparams(sem):
    return pltpu.CompilerParams(dimension_semantics=sem, vmem_limit_bytes=VMEM_LIMIT)


def _log2(n):
    assert n & (n - 1) == 0
    return n.bit_length() - 1


def _head_cols(h):
    return slice(h * HEAD_DIM, (h + 1) * HEAD_DIM)


def _lane_chunks(x):
    return [x[:, c * HEAD_DIM:(c + 1) * HEAD_DIM] for c in range(x.shape[1] // HEAD_DIM)]


def _tree(op, xs):
    while len(xs) > 1:
        xs = [op(xs[n], xs[n + 1]) if n + 1 < len(xs) else xs[n] for n in range(0, len(xs), 2)]
    return xs[0]


def _qk(q, k):
    return lax.dot_general(q, k.astype(BF16), (((1,), (1,)), ((), ())), preferred_element_type=F32)


def _with_ones(v):
    v = v.astype(BF16)
    return jnp.concatenate([v, jnp.ones_like(v)], axis=1)


def _store_head_major(dst_ref, x):
    rows = x.shape[0]
    for h in range(N_HEADS):
        dst_ref[pl.ds(h, rows, stride=N_HEADS), :] = x[:, _head_cols(h)]


def _store_layer_head_major(dst_ref, x, layer, whole):
    if not whole:
        _store_head_major(dst_ref, x)
        return
    for other in range(dst_ref.shape[0]):
        if other != layer:
            dst_ref[other] = jnp.zeros(dst_ref.shape[1:], dst_ref.dtype)
    _store_head_major(dst_ref.at[layer], x)


def _inproj_kernel(x_ref, g_ref, w_ref, *rest, slabs_per_step, direct, layer, whole):
    n_direct = len(direct)
    o_ref = rest[-n_direct - 2]
    direct_refs = rest[-n_direct - 1:-1]
    xn_ref = rest[-1]
    j = pl.program_id(1)

    @pl.when(j == 0)
    def _():
        x = x_ref[...]
        ms = jnp.mean(x * x, axis=-1, keepdims=True)
        xn_ref[...] = ((x * lax.rsqrt(ms + EPS)) * g_ref[...]).astype(BF16)

    res = jnp.dot(xn_ref[...], w_ref[...], preferred_element_type=F32)
    for k in range(slabs_per_step):
        o_ref[k] = res[:, k * GROUP_W:(k + 1) * GROUP_W]
    for slab, d_ref in zip(direct, direct_refs):
        @pl.when(j == slab // slabs_per_step)
        def _(slab=slab, d_ref=d_ref):
            k = slab % slabs_per_step
            _store_layer_head_major(d_ref, res[:, k * GROUP_W:(k + 1) * GROUP_W], layer, whole)


def _layer_rows_spec(rows, depth, layer, whole, grid_rank):
    if grid_rank == 1:
        if whole:
            return pl.BlockSpec((depth, rows, HEAD_DIM), lambda i: (0, i, 0))
        return pl.BlockSpec((None, rows, HEAD_DIM), lambda i: (layer, i, 0))
    if whole:
        return pl.BlockSpec((depth, rows, HEAD_DIM), lambda i, j: (0, i, 0))
    return pl.BlockSpec((None, rows, HEAD_DIM), lambda i, j: (layer, i, 0))


def in_proj(x2d, g, w_bf16, tm, slabs_per_step, direct=(), direct_bufs=None, layer=0, depth=1):
    m, d = x2d.shape
    n = w_bf16.shape[2]
    ns = n // GROUP_W
    sps = slabs_per_step
    n_direct = len(direct)
    alias_in = list(direct_bufs) if direct_bufs is not None else []
    whole = not alias_in
    direct_shape = jax.ShapeDtypeStruct((depth, m * N_HEADS, HEAD_DIM), F32)
    outs = pl.pallas_call(
        functools.partial(_inproj_kernel, slabs_per_step=sps, direct=tuple(direct), layer=layer,
                          whole=whole),
        grid=(m // tm, ns // sps),
        in_specs=[pl.BlockSpec((tm, d), lambda i, j: (i, 0)),
                  pl.BlockSpec((1, d), lambda i, j: (0, 0)),
                  pl.BlockSpec((None, d, sps * GROUP_W), lambda i, j: (layer, 0, j))]
                 + [pl.BlockSpec(memory_space=pl.ANY)] * len(alias_in),
        out_specs=[pl.BlockSpec((sps, tm, GROUP_W), lambda i, j: (j, i, 0))]
                  + [_layer_rows_spec(tm * N_HEADS, depth, layer, whole, 2)] * n_direct,
        out_shape=[jax.ShapeDtypeStruct((ns, m, GROUP_W), F32)] + [direct_shape] * n_direct,
        input_output_aliases={3 + n: 1 + n for n in range(len(alias_in))},
        scratch_shapes=[pltpu.VMEM((tm, d), BF16)],
        compiler_params=_cparams(("parallel", "arbitrary")),
        name="in_proj",
    )(x2d, g.reshape(1, d), w_bf16, *alias_in)
    return outs[0], tuple(outs[1:])


def _outproj_kernel(ya_ref, yb_ref, yc_ref, yd_ref, w_ref, x_ref, fg_ref, o_ref, *, final):
    acc = x_ref[...]
    for m, y_ref in enumerate((ya_ref, yb_ref, yc_ref, yd_ref)):
        acc = acc + jnp.dot(y_ref[...], w_ref[m * GROUP_W:(m + 1) * GROUP_W, :],
                            preferred_element_type=F32)
    if final:
        ms = jnp.mean(acc * acc, axis=-1, keepdims=True)
        acc = (acc * lax.rsqrt(ms + EPS)) * fg_ref[...]
    o_ref[...] = acc


def out_proj(ys, w_bf16, layer, x2d, final_g, final, tm):
    m, d = x2d.shape
    yspec = pl.BlockSpec((tm, GROUP_W), lambda i: (i, 0))
    return pl.pallas_call(
        functools.partial(_outproj_kernel, final=final),
        grid=(m // tm,),
        in_specs=[yspec, yspec, yspec, yspec,
                  pl.BlockSpec((None, N_MIXERS * GROUP_W, d), lambda i: (layer, 0, 0)),
                  pl.BlockSpec((tm, d), lambda i: (i, 0)),
                  pl.BlockSpec((1, d), lambda i: (0, 0))],
        out_specs=pl.BlockSpec((tm, d), lambda i: (i, 0)),
        out_shape=jax.ShapeDtypeStruct((m, d), F32),
        compiler_params=_cparams(("parallel",)),
        name="out_proj",
    )(*ys, w_bf16, x2d, final_g.reshape(1, d))


def _sb_terms(z):
    neg = -z
    t = jnp.log2(1.0 + jnp.exp2(jnp.minimum(z, neg)))
    return jnp.minimum(z, 0.0) - t, jnp.minimum(neg, 0.0) - t


def _suffix_sums(lk, u):
    hi = lk.astype(BF16)
    lo = (lk - hi.astype(F32)).astype(BF16)
    return (jnp.dot(hi, u, preferred_element_type=F32) + jnp.dot(lo, u, preferred_element_type=F32))


def _strict_lower(n):
    row = lax.broadcasted_iota(jnp.int32, (n, n), 0)
    col = lax.broadcasted_iota(jnp.int32, (n, n), 1)
    return row, col


def _sb_prompt_kernel(q_ref, k_ref, v_ref, g_ref, o_ref, *scratch, tq, tk, hps):
    i = pl.program_id(1)
    acc_refs, carry_refs = scratch[:hps], scratch[hps:]
    n_diag = tq // tk
    row, col = _strict_lower(tk)
    u = jnp.where(row > col, 1.0, 0.0).astype(BF16)
    qs = [q_ref[:, _head_cols(h)].astype(BF16) for h in range(hps)]

    def scores(h, jb):
        start = pl.multiple_of(jb * tk, tk)
        z = _qk(qs[h], k_ref[pl.ds(start, tk), _head_cols(h)]) * SB_SCALE_LOG2
        return z, v_ref[pl.ds(start, tk), _head_cols(h)].astype(BF16)

    def sweep(jb, visible, first):
        for h in range(hps):
            z, v = scores(h, jb)
            ls, lk = _sb_terms(z)
            if visible is not None:
                lk = jnp.where(visible, lk, 0.0)
            e = ls + _suffix_sums(lk, u)
            if not first:
                e = e + carry_refs[h][...]
            a = jnp.exp2(e)
            if visible is not None:
                a = jnp.where(visible, a, 0.0)
            pv = jnp.dot(a.astype(BF16), v, preferred_element_type=F32)
            dc = jnp.sum(lk, axis=1, keepdims=True)
            if first:
                acc_refs[h][...] = pv
                carry_refs[h][...] = dc
            else:
                acc_refs[h][...] += pv
                carry_refs[h][...] += dc

    qrow = lax.broadcasted_iota(jnp.int32, (tq, tk), 0)
    kcol = lax.broadcasted_iota(jnp.int32, (tq, tk), 1)
    for d in reversed(range(n_diag)):
        sweep(i * n_diag + d, kcol + d * tk < qrow, d == n_diag - 1)

    def live():
        return _tree(jnp.maximum, [jnp.max(c[...]) for c in carry_refs]) > SB_EXIT_LOG2

    def cond(state):
        jb, go = state
        return jnp.logical_and(jb >= 0, go)

    def body(state):
        jb, _ = state
        sweep(jb, None, False)
        return jb - 1, live()

    lax.while_loop(cond, body, (i * n_diag - 1, live()))
    for h in range(hps):
        o_ref[:, _head_cols(h)] = (acc_refs[h][...] * _silu(g_ref[:, _head_cols(h)])).astype(o_ref.dtype)


def sb_prompt(proj, tq, tk, heads_per_step):
    _, t, _ = proj.shape
    hps = heads_per_step
    width = hps * HEAD_DIM

    def spec(rows, s, per_q):
        if per_q:
            return pl.BlockSpec((None, rows, width), lambda h, i: (s, i, h))
        return pl.BlockSpec((None, rows, width), lambda h, i: (s, 0, h))

    return pl.pallas_call(
        functools.partial(_sb_prompt_kernel, tq=tq, tk=tk, hps=hps),
        grid=(N_HEADS // hps, t // tq),
        in_specs=[spec(tq, S_SB_Q, True), spec(t, S_SB_K, False), spec(t, S_SB_V, False),
                  spec(tq, S_SB_G, True)],
        out_specs=pl.BlockSpec((tq, width), lambda h, i: (i, h)),
        out_shape=jax.ShapeDtypeStruct((t, GROUP_W), BF16),
        scratch_shapes=[pltpu.VMEM((tq, HEAD_DIM), F32)] * hps + [pltpu.VMEM((tq, 1), F32)] * hps,
        compiler_params=_cparams(("parallel", "arbitrary")),
        name="sb_prompt",
    )(proj, proj, proj, proj)


def _page_masks(rows, rows_per_head, page_rows):
    row = lax.broadcasted_iota(jnp.int32, (rows, page_rows), 0)
    col = lax.broadcasted_iota(jnp.int32, (rows, page_rows), 1)
    own = (col & (N_HEADS - 1)) == (row >> _log2(rows_per_head))
    return own, col >> _log2(N_HEADS), row & (PAD_T - 1)


def _sb_decode_kernel(pt_ref, q_ref, kn_ref, vn_ref, g_ref, u_ref, ck_ref, cv_ref, o_ref,
                      kbuf, vbuf, newk, newv, sem, acc_ref, carry_ref, *, layer, n_pages, n_tok):
    b = pl.program_id(0)
    rows = N_HEADS * PAD_T
    page_rows = newk.shape[0]

    def page_copies(p, slot):
        pg = pt_ref[b, p]
        return (pltpu.make_async_copy(ck_ref.at[layer, pg], kbuf.at[slot], sem.at[0, slot]),
                pltpu.make_async_copy(cv_ref.at[layer, pg], vbuf.at[slot], sem.at[1, slot]))

    def start_fetch(p, slot):
        for c in page_copies(p, slot):
            c.start()

    def wait_fetch(p, slot):
        for c in page_copies(p, slot):
            c.wait()

    start_fetch(n_pages - 1, 0)

    qall = jnp.concatenate([q_ref[h] for h in range(N_HEADS)], axis=0).astype(BF16)
    own, key, tok = _page_masks(rows, PAD_T, page_rows)
    real_row = (lax.broadcasted_iota(jnp.int32, (rows, 1), 0) & (PAD_T - 1)) < n_tok
    u = u_ref[...]

    def attend(k, v, visible, carry):
        ls, lk = _sb_terms(_qk(qall, k) * SB_SCALE_LOG2)
        lk = jnp.where(visible, lk, 0.0)
        e = ls + _suffix_sums(lk, u)
        if carry is not None:
            e = e + carry
        a = jnp.where(visible, jnp.exp2(e), 0.0).astype(BF16)
        return (jnp.dot(a, v.astype(BF16), preferred_element_type=F32),
                jnp.sum(lk, axis=1, keepdims=True))

    def live(carry):
        return jnp.max(jnp.where(real_row, carry, -jnp.inf)) > SB_EXIT_LOG2

    newk[...] = jnp.zeros_like(newk)
    newv[...] = jnp.zeros_like(newv)
    newk[0:kn_ref.shape[0], :] = kn_ref[...]
    newv[0:vn_ref.shape[0], :] = vn_ref[...]
    pv, dc = attend(newk[...], newv[...], jnp.logical_and(own, key < tok), None)
    acc_ref[...] = pv
    carry_ref[...] = dc

    def cond(state):
        p, go = state
        return jnp.logical_and(p >= 0, go)

    def body(state):
        p, _ = state
        slot = (n_pages - 1 - p) % 2
        wait_fetch(p, slot)

        @pl.when(p >= 1)
        def _():
            start_fetch(p - 1, 1 - slot)

        carry = carry_ref[...]
        pv, dc = attend(kbuf[slot], vbuf[slot], own, carry)
        acc_ref[...] += pv
        carry = carry + dc
        carry_ref[...] = carry
        return p - 1, live(carry)

    p_end, _ = lax.while_loop(cond, body, (jnp.int32(n_pages - 1), live(carry_ref[...])))

    @pl.when(p_end >= 0)
    def _():
        wait_fetch(p_end, (n_pages - 1 - p_end) % 2)

    acc = acc_ref[...]
    o_ref[...] = jnp.concatenate([acc[h * PAD_T:(h + 1) * PAD_T] * _silu(g_ref[h])
                                  for h in range(N_HEADS)], axis=1)


def sb_decode(q8, kn, vn, g8, cache_k, cache_v, page_table, layer, n_tok):
    bsz = page_table.shape[0]
    n_pages = page_table.shape[1]
    page_rows = cache_k.shape[2]
    tok_spec = pl.BlockSpec((None, PAD_T, GROUP_W), lambda b, pt: (b, 0, 0))
    head_spec = pl.BlockSpec((N_HEADS, PAD_T, HEAD_DIM), lambda b, pt: (0, b, 0))
    new_spec = pl.BlockSpec((None, kn.shape[1], HEAD_DIM), lambda b, pt: (b, 0, 0))
    any_spec = pl.BlockSpec(memory_space=pl.ANY)
    rows = N_HEADS * PAD_T
    row, col = _strict_lower(page_rows)
    u = jnp.where(row > col, 1.0, 0.0).astype(BF16)
    page_buf = pltpu.VMEM((2, page_rows, HEAD_DIM), F32)
    new_buf = pltpu.VMEM((page_rows, HEAD_DIM), F32)
    return pl.pallas_call(
        functools.partial(_sb_decode_kernel, layer=layer, n_pages=n_pages, n_tok=n_tok),
        grid_spec=pltpu.PrefetchScalarGridSpec(
            num_scalar_prefetch=1,
            grid=(bsz,),
            in_specs=[head_spec, new_spec, new_spec, head_spec,
                      pl.BlockSpec((page_rows, page_rows), lambda b, pt: (0, 0)), any_spec, any_spec],
            out_specs=tok_spec,
            scratch_shapes=[page_buf, page_buf, new_buf, new_buf,
                            pltpu.SemaphoreType.DMA((2, 2)),
                            pltpu.VMEM((rows, HEAD_DIM), F32), pltpu.VMEM((rows, 1), F32)]),
        out_shape=jax.ShapeDtypeStruct((bsz, PAD_T, GROUP_W), F32),
        compiler_params=_cparams(("arbitrary",)),
        name="sb_decode",
    )(page_table, q8, kn, vn, g8, u, cache_k, cache_v)


def _lambda(lq1_ref, lk1_ref, lq2_ref, lk2_ref, lam_init):
    s1 = jnp.sum(lq1_ref[...] * lk1_ref[...], axis=-1, keepdims=True)
    s2 = jnp.sum(lq2_ref[...] * lk2_ref[...], axis=-1, keepdims=True)
    return jnp.exp(s1) - jnp.exp(s2) + lam_init


def _diff_finish(o1, o2, lam, sg, gate, lam_init):
    w = o1 - lam * o2
    ms = jnp.mean(w * w, axis=-1, keepdims=True)
    y = (w * lax.rsqrt(ms + EPS)) * sg
    return (y * (1.0 - lam_init)) * _silu(gate)


def _split_components(q):
    lane = lax.broadcasted_iota(jnp.int32, q.shape, 1)
    zero = jnp.zeros_like(q)
    return jnp.concatenate([jnp.where(lane < DIFF_QK, q, zero), jnp.where(lane >= DIFF_QK, q, zero)],
                           axis=0)


def _diff_prompt_kernel(q_ref, k_ref, v_ref, g_ref, lq1_ref, lk1_ref, lq2_ref, lk2_ref, sg_ref,
                        o_ref, s0_ref, s1_ref, mt0_ref, mt1_ref, m_ref, acc_ref, *, tq, tk, lam_init):
    i = pl.program_id(1)
    qq = _split_components(q_ref[...])
    m_ref[...] = jnp.full_like(m_ref, -jnp.inf)
    acc_ref[...] = jnp.zeros_like(acc_ref)
    bufs = ((s0_ref, mt0_ref), (s1_ref, mt1_ref))

    def scores(jt, diag=None):
        s = _qk(qq, k_ref[pl.ds(pl.multiple_of(jt * tk, tk), tk), :])
        if diag is not None:
            row = lax.broadcasted_iota(jnp.int32, s.shape, 0) & (tq - 1)
            col = lax.broadcasted_iota(jnp.int32, s.shape, 1) + diag * tk
            s = jnp.where(col <= row, s, -jnp.inf)
        return s

    def score_stage(slot, s):
        s_ref, mt_ref = bufs[slot]
        s_ref[...] = s
        mt_ref[...] = jnp.broadcast_to(jnp.max(s, axis=1, keepdims=True), mt_ref.shape)

    def value_stage(jt, slot):
        s_ref, mt_ref = bufs[slot]
        m_old = m_ref[...]
        m_new = jnp.maximum(m_old, mt_ref[...])
        alpha = jnp.exp2(m_old - m_new)
        p = jnp.concatenate([jnp.exp2(c - m_new) for c in _lane_chunks(s_ref[...])], axis=1)
        v = v_ref[pl.ds(pl.multiple_of(jt * tk, tk), tk), :]
        pv = jnp.dot(p.astype(BF16), _with_ones(v), preferred_element_type=F32)
        acc_ref[...] = jnp.concatenate([alpha, alpha], axis=1) * acc_ref[...] + pv
        m_ref[...] = m_new

    d0 = 2 * i

    @pl.when(i == 0)
    def _():
        score_stage(0, scores(0, 0))
        score_stage(1, scores(1, 1))
        value_stage(0, 0)
        value_stage(1, 1)

    @pl.when(i > 0)
    def _():
        score_stage(0, scores(0))

        def body(g, c):
            score_stage(1, scores(2 * g + 1))
            value_stage(2 * g, 0)
            score_stage(0, scores(2 * g + 2))
            value_stage(2 * g + 1, 1)
            return c

        lax.fori_loop(0, i - 1, body, 0)
        score_stage(1, scores(d0 - 1))
        value_stage(d0 - 2, 0)
        score_stage(0, scores(d0, 0))
        value_stage(d0 - 1, 1)
        score_stage(1, scores(d0 + 1, 1))
        value_stage(d0, 0)
        value_stage(d0 + 1, 1)

    acc = acc_ref[...]
    o = acc[:, :HEAD_DIM] / acc[:, HEAD_DIM:]
    lam = _lambda(lq1_ref, lk1_ref, lq2_ref, lk2_ref, lam_init)
    o_ref[...] = _diff_finish(o[:tq], o[tq:], lam, sg_ref[...], g_ref[...], lam_init).astype(o_ref.dtype)


def diff_prompt(qb, kb, vb, proj, lam_params, subln_g, lam_init, tk):
    t = qb.shape[0]
    tq = 2 * tk
    small = pl.BlockSpec((1, DIFF_QK), lambda h, i: (0, 0))
    return pl.pallas_call(
        functools.partial(_diff_prompt_kernel, tq=tq, tk=tk, lam_init=lam_init),
        grid=(N_HEADS, t // tq),
        in_specs=[pl.BlockSpec((tq, HEAD_DIM), lambda h, i: (i, h)),
                  pl.BlockSpec((t, HEAD_DIM), lambda h, i: (0, h)),
                  pl.BlockSpec((t, HEAD_DIM), lambda h, i: (0, h)),
                  pl.BlockSpec((None, tq, HEAD_DIM), lambda h, i: (S_DF_G, i, h)),
                  small, small, small, small,
                  pl.BlockSpec((1, HEAD_DIM), lambda h, i: (0, 0))],
        out_specs=pl.BlockSpec((tq, HEAD_DIM), lambda h, i: (i, h)),
        out_shape=jax.ShapeDtypeStruct((t, GROUP_W), BF16),
        scratch_shapes=[pltpu.VMEM((2 * tq, tk), F32), pltpu.VMEM((2 * tq, tk), F32),
                        pltpu.VMEM((2 * tq, HEAD_DIM), F32), pltpu.VMEM((2 * tq, HEAD_DIM), F32),
                        pltpu.VMEM((2 * tq, HEAD_DIM), F32), pltpu.VMEM((2 * tq, 2 * HEAD_DIM), F32)],
        compiler_params=_cparams(("parallel", "arbitrary")),
        name="diff_prompt",
    )(qb, kb, vb, proj, *[p.reshape(1, DIFF_QK) for p in lam_params], subln_g.reshape(1, HEAD_DIM))


def _diff_decode_kernel(pt_ref, q_ref, kn_ref, vn_ref, g_ref, lq1_ref, lk1_ref, lq2_ref, lk2_ref,
                        sg_ref, ck_ref, cv_ref, o_ref, kbuf, vbuf, sem, m_ref, acc_ref, newk, newv, *,
                        layer, pages_per_step, group, lam_init):
    b = pl.program_id(0)
    c = pl.program_id(1)
    n_chunks = pl.num_programs(1)
    hrows = 2 * PAD_T
    rows = N_HEADS * hrows
    page_rows = newk.shape[0]

    step = b * n_chunks + c
    slot = step % 2

    def chunk_copies(bb, cc, sl):
        copies = []
        for n in range(pages_per_step):
            pg = pt_ref[bb, cc * pages_per_step + n]
            copies.append(pltpu.make_async_copy(ck_ref.at[layer, pg], kbuf.at[sl, n], sem.at[sl, 0, n]))
            copies.append(pltpu.make_async_copy(cv_ref.at[layer, pg], vbuf.at[sl, n], sem.at[sl, 1, n]))
        return copies

    def start_all(copies):
        for n, cp in enumerate(copies):
            cp.start(priority=n % 2)

    @pl.when(step == 0)
    def _():
        start_all(chunk_copies(b, c, slot))

    @pl.when(step + 1 < pl.num_programs(0) * n_chunks)
    def _():
        wrap = c + 1 == n_chunks
        start_all(chunk_copies(jnp.where(wrap, b + 1, b), jnp.where(wrap, 0, c + 1), 1 - slot))

    for cp in chunk_copies(b, c, slot):
        cp.wait()

    qall = jnp.concatenate([_split_components(q_ref[h] * (DIFF_SCALE * LOG2E))
                            for h in range(N_HEADS)], axis=0).astype(BF16)
    zq = jnp.zeros_like(qall)
    qpair = jnp.concatenate([jnp.concatenate([qall, zq], axis=1),
                             jnp.concatenate([zq, qall], axis=1)], axis=0)
    own, key, tok = _page_masks(rows, hrows, page_rows)

    @pl.when(c == 0)
    def _():
        m_ref[...] = jnp.full_like(m_ref, -jnp.inf)
        acc_ref[...] = jnp.zeros_like(acc_ref)

    def page_scores(ks):
        out = []
        for n in range(0, len(ks) - 1, 2):
            kk = jnp.concatenate([ks[n].astype(BF16), ks[n + 1].astype(BF16)], axis=1)
            s2 = _qk(qpair, kk)
            out += [s2[:rows], s2[rows:]]
        if len(ks) % 2:
            out.append(_qk(qall, ks[-1]))
        return out

    def update(ks, vs, visible):
        s = jnp.concatenate([jnp.where(visible, z, -jnp.inf) for z in page_scores(ks)], axis=1)
        m_prev = m_ref[...]
        m_new = jnp.maximum(m_prev, jnp.max(s, axis=1, keepdims=True))
        p = jnp.exp2(s - m_new).astype(BF16)
        pv = _tree(jnp.add, [jnp.dot(p[:, n * page_rows:(n + 1) * page_rows], _with_ones(v),
                                     preferred_element_type=F32) for n, v in enumerate(vs)])
        return m_new, jnp.exp2(m_prev - m_new) * acc_ref[...] + pv

    for g0 in range(0, pages_per_step, group):
        m_new, acc_new = update([kbuf[slot, n] for n in range(g0, g0 + group)],
                                [vbuf[slot, n] for n in range(g0, g0 + group)], own)
        m_ref[...] = m_new
        acc_ref[...] = acc_new

    @pl.when(c == pl.num_programs(1) - 1)
    def _():
        newk[...] = jnp.zeros_like(newk)
        newv[...] = jnp.zeros_like(newv)
        newk[0:kn_ref.shape[0], :] = kn_ref[...]
        newv[0:vn_ref.shape[0], :] = vn_ref[...]
        _, acc = update([newk[...]], [newv[...]], jnp.logical_and(own, key <= tok))
        o = acc[:, :HEAD_DIM] / acc[:, HEAD_DIM:]
        lam = _lambda(lq1_ref, lk1_ref, lq2_ref, lk2_ref, lam_init)
        sg = sg_ref[...]
        outs = []
        for h in range(N_HEADS):
            o1 = o[h * hrows:h * hrows + PAD_T]
            o2 = o[h * hrows + PAD_T:(h + 1) * hrows]
            outs.append(_diff_finish(o1, o2, lam, sg, g_ref[h], lam_init))
        o_ref[...] = jnp.concatenate(outs, axis=1)


def diff_decode(q8, kn, vn, g8, cache_k, cache_v, page_table, lam_params, subln_g, lam_init,
                layer, pages_per_step, group):
    bsz = page_table.shape[0]
    n_pages = page_table.shape[1]
    page_rows = cache_k.shape[2]
    pps = pages_per_step
    tok_spec = pl.BlockSpec((None, PAD_T, GROUP_W), lambda b, c, pt: (b, 0, 0))
    head_spec = pl.BlockSpec((N_HEADS, PAD_T, HEAD_DIM), lambda b, c, pt: (0, b, 0))
    new_spec = pl.BlockSpec((None, kn.shape[1], HEAD_DIM), lambda b, c, pt: (b, 0, 0))
    small = pl.BlockSpec((1, DIFF_QK), lambda b, c, pt: (0, 0))

    rows = N_HEADS * 2 * PAD_T
    any_spec = pl.BlockSpec(memory_space=pl.ANY)
    page_bufs = pltpu.VMEM((2, pps, page_rows, HEAD_DIM), F32)
    new_buf = pltpu.VMEM((page_rows, HEAD_DIM), F32)
    return pl.pallas_call(
        functools.partial(_diff_decode_kernel, layer=layer, pages_per_step=pps, group=group,
                          lam_init=lam_init),
        grid_spec=pltpu.PrefetchScalarGridSpec(
            num_scalar_prefetch=1,
            grid=(bsz, n_pages // pps),
            in_specs=[head_spec, new_spec, new_spec, head_spec, small, small, small, small,
                      pl.BlockSpec((1, HEAD_DIM), lambda b, c, pt: (0, 0)), any_spec, any_spec],
            out_specs=tok_spec,
            scratch_shapes=[page_bufs, page_bufs, pltpu.SemaphoreType.DMA((2, 2, pps)),
                            pltpu.VMEM((rows, 1), F32), pltpu.VMEM((rows, 2 * HEAD_DIM), F32),
                            new_buf, new_buf]),
        out_shape=jax.ShapeDtypeStruct((bsz, PAD_T, GROUP_W), F32),
        compiler_params=_cparams(("arbitrary", "arbitrary")),
        name="diff_decode",
    )(page_table, q8, kn, vn, g8, *[p.reshape(1, DIFF_QK) for p in lam_params],
      subln_g.reshape(1, HEAD_DIM), cache_k, cache_v)


CONF_HALO = 32
SC_HALO = 8
SUBLANES = 8
MIX_ROWS = 32


def _rope_tables(pos):
    half = ROT_DIM // 2
    inv = ROPE_THETA ** (-2.0 * jnp.arange(half, dtype=F32) / ROT_DIM)
    ang = pos.astype(F32)[:, None] * inv[None, :]
    cos, sin = jnp.cos(ang), jnp.sin(ang)
    j = jnp.arange(HEAD_DIM) % DIFF_QK
    f = j % half
    c = jnp.where(j < ROT_DIM, cos[:, f], 1.0)
    sa = jnp.where((j >= half) & (j < ROT_DIM), sin[:, f], 0.0)
    sb = jnp.where(j < half, -sin[:, f], 0.0)
    return c, sa, sb


def _rope_rows(x, c, sa, sb):
    half = ROT_DIM // 2
    out = []
    for xh in _lane_chunks(x):
        out.append(xh * c + pltpu.roll(xh, half, 1) * sa + pltpu.roll(xh, HEAD_DIM - half, 1) * sb)
    return jnp.concatenate(out, axis=1)


def _prompt_mix_kernel(a_ref, b_ref, g_ref, dq_ref, dk_ref, dv_ref, sh_ref, sb_ref, sc_ref, sg_ref,
                       cw_ref, cb_ref, lg_ref, lb_ref, sw_ref, rc_ref, rsa_ref, rsb_ref, *rest,
                       tb, conf_w, sc_w, layer, whole):
    ya_ref, yd_ref, dqb_ref, dk_out_ref, dkb_ref, dvb_ref, nconf_ref, nsc_ref, hb, hs, ub = rest[-11:]
    i = pl.program_id(0)

    @pl.when(i == 0)
    def _():
        hb[0:CONF_HALO, :] = jnp.zeros((CONF_HALO, GROUP_W), F32)
        ub[0:SC_HALO, :] = jnp.zeros((SC_HALO, GROUP_W), F32)

    @pl.when(i > 0)
    def _():
        hb[0:CONF_HALO, :] = hb[tb:tb + CONF_HALO, :]
        ub[0:SC_HALO, :] = ub[tb:tb + SC_HALO, :]

    hb[CONF_HALO:CONF_HALO + tb, :] = a_ref[...] * (1.0 / (1.0 + jnp.exp(-b_ref[...])))
    ub[SC_HALO:SC_HALO + tb, :] = sc_ref[...] * sh_ref[...]

    off = CONF_HALO - (conf_w - 1)
    span = hs.shape[1]
    for k in range(1, SUBLANES):
        hs[k - 1, :, :] = hb[k:k + span, :]

    def tap(w_ref, w):
        return jnp.concatenate([w_ref[w]] * (MIX_ROWS // SUBLANES), axis=0)

    def window(r):
        k, base = r % SUBLANES, r - r % SUBLANES
        if k == 0:
            return hb[base:base + MIX_ROWS, :]
        return hs[k - 1, base:base + MIX_ROWS, :]

    for r0 in range(0, tb, MIX_ROWS):
        rows = slice(r0, r0 + MIX_ROWS)
        acc = jnp.zeros((MIX_ROWS, GROUP_W), F32) + cb_ref[...]
        for w in range(conf_w):
            acc = acc + tap(cw_ref, w) * window(r0 + off + w)
        mu = jnp.mean(acc, axis=-1, keepdims=True)
        xc = acc - mu
        y = (xc * lax.rsqrt(jnp.mean(xc * xc, axis=-1, keepdims=True) + EPS)) * lg_ref[...] + lb_ref[...]
        ya_ref[rows, :] = (_silu(y) * _silu(g_ref[rows, :])).astype(ya_ref.dtype)
        u0 = r0 + SC_HALO - (sc_w - 1)
        conv = jnp.zeros((MIX_ROWS, GROUP_W), F32)
        for w in range(sc_w):
            conv = conv + tap(sw_ref, w) * ub[u0 + w:u0 + w + MIX_ROWS, :]
        yd_ref[rows, :] = ((sb_ref[rows, :] * conv) * _silu(sg_ref[rows, :])).astype(yd_ref.dtype)

    c, sa, sb = rc_ref[...], rsa_ref[...], rsb_ref[...]
    dqb_ref[...] = (_rope_rows(dq_ref[...], c, sa, sb) * (DIFF_SCALE * LOG2E)).astype(dqb_ref.dtype)
    dk = _rope_rows(dk_ref[...], c, sa, sb)
    _store_layer_head_major(dk_out_ref, dk, layer, whole)
    dkb_ref[...] = dk.astype(dkb_ref.dtype)
    dvb_ref[...] = dv_ref[...].astype(dvb_ref.dtype)

    @pl.when(i == pl.num_programs(0) - 1)
    def _():
        nconf_ref[...] = hb[CONF_HALO + tb - (conf_w - 1):CONF_HALO + tb, :]
        nsc_ref[...] = ub[SC_HALO + tb - (sc_w - 1):SC_HALO + tb, :]


def prompt_mix(proj, conf_w, conf_b, ln_g, ln_b, sc_w, rope_tabs, tb, k_buf=None, layer=0, depth=1):
    _, t, _ = proj.shape
    cw, scw = conf_w.shape[0], sc_w.shape[0]

    def slab(s):
        return pl.BlockSpec((None, tb, GROUP_W), lambda i: (s, i, 0))

    def full(r):
        return pl.BlockSpec((r, GROUP_W), lambda i: (0, 0))

    def taps(n):
        return pl.BlockSpec((n, SUBLANES, GROUP_W), lambda i: (0, 0, 0))

    def replicate(w):
        return jnp.broadcast_to(w[:, None, :], (w.shape[0], SUBLANES, w.shape[1]))

    row_blk = pl.BlockSpec((tb, GROUP_W), lambda i: (i, 0))
    tab_blk = pl.BlockSpec((tb, HEAD_DIM), lambda i: (i, 0))
    slabs = (S_CONF_A, S_CONF_B, S_CONF_G, S_DF_Q, S_DF_K, S_DF_V, S_SC_H, S_SC_B, S_SC_C, S_SC_G)
    act = jax.ShapeDtypeStruct((t, GROUP_W), BF16)
    alias_in = [] if k_buf is None else [k_buf]
    n_in = len(slabs) + 8
    return pl.pallas_call(
        functools.partial(_prompt_mix_kernel, tb=tb, conf_w=cw, sc_w=scw, layer=layer,
                          whole=not alias_in),
        grid=(t // tb,),
        in_specs=[slab(s) for s in slabs]
                 + [taps(cw), full(1), full(1), full(1), taps(scw), tab_blk, tab_blk, tab_blk]
                 + [pl.BlockSpec(memory_space=pl.ANY)] * len(alias_in),
        out_specs=[row_blk, row_blk, row_blk,
                   _layer_rows_spec(tb * N_HEADS, depth, layer, not alias_in, 1),
                   row_blk, row_blk, full(cw - 1), full(scw - 1)],
        out_shape=[act, act, act, jax.ShapeDtypeStruct((depth, t * N_HEADS, HEAD_DIM), F32), act, act,
                   jax.ShapeDtypeStruct((cw - 1, GROUP_W), F32),
                   jax.ShapeDtypeStruct((scw - 1, GROUP_W), F32)],
        scratch_shapes=[pltpu.VMEM((CONF_HALO + tb, GROUP_W), F32),
                        pltpu.VMEM((SUBLANES - 1, CONF_HALO + tb - SUBLANES, GROUP_W), F32),
                        pltpu.VMEM((SC_HALO + tb, GROUP_W), F32)],
        input_output_aliases={n_in: 3} if alias_in else {},
        compiler_params=_cparams(("arbitrary",)),
        name="prompt_mix",
    )(*([proj] * len(slabs)), replicate(conf_w), conf_b.reshape(1, -1), ln_g.reshape(1, -1),
      ln_b.reshape(1, -1), replicate(sc_w), *rope_tabs, *alias_in)


def _decode_mix_kernel(proj_ref, sconf_ref, ssc_ref, cw_ref, cb_ref, lg_ref, lb_ref, sw_ref,
                       rc_ref, rsa_ref, rsb_ref,
                       ya_ref, yd_ref, sbq_ref, sbg_ref, dfq_ref, dfg_ref, sbk_ref, sbv_ref, dfk_ref,
                       dfv_ref, nconf_ref, nsc_ref, *, bsz, n_tok, conf_w, sc_w):
    def tok(slab, t):
        return proj_ref[slab, t * bsz:(t + 1) * bsz, :]

    def put_tok(ref, t, x):
        ref[t * bsz:(t + 1) * bsz, :] = x.astype(ref.dtype)

    def put_pad(ref, t, x):
        for h in range(N_HEADS):
            ref[h, pl.ds(t, bsz, stride=PAD_T), :] = x[:, _head_cols(h)]

    def put_heads(ref, t, x):
        for h in range(N_HEADS):
            ref[pl.ds(t * N_HEADS + h, bsz, stride=n_tok * N_HEADS), :] = x[:, _head_cols(h)]

    def tap(w_ref, w):
        if bsz % SUBLANES:
            return jnp.broadcast_to(w_ref[w, 0:1, :], (bsz, GROUP_W))
        return jnp.concatenate([w_ref[w]] * (bsz // SUBLANES), axis=0)

    h = [tok(S_CONF_A, t) * (1.0 / (1.0 + jnp.exp(-tok(S_CONF_B, t)))) for t in range(n_tok)]

    def hist(r):
        return sconf_ref[r] if r < conf_w - 1 else h[r - (conf_w - 1)]

    for t in range(n_tok):
        acc = jnp.zeros((bsz, GROUP_W), F32) + cb_ref[...]
        for w in range(conf_w):
            acc = acc + tap(cw_ref, w) * hist(t + w)
        mu = jnp.mean(acc, axis=-1, keepdims=True)
        xc = acc - mu
        y = (xc * lax.rsqrt(jnp.mean(xc * xc, axis=-1, keepdims=True) + EPS)) * lg_ref[...] + lb_ref[...]
        put_tok(ya_ref, t, _silu(y) * _silu(tok(S_CONF_G, t)))
    for r in range(conf_w - 1):
        nconf_ref[r] = hist(r + n_tok)

    u = [tok(S_SC_C, t) * tok(S_SC_H, t) for t in range(n_tok)]

    def uhist(r):
        return ssc_ref[r] if r < sc_w - 1 else u[r - (sc_w - 1)]

    for t in range(n_tok):
        conv = jnp.zeros((bsz, GROUP_W), F32)
        for w in range(sc_w):
            conv = conv + tap(sw_ref, w) * uhist(t + w)
        put_tok(yd_ref, t, (tok(S_SC_B, t) * conv) * _silu(tok(S_SC_G, t)))
    for r in range(sc_w - 1):
        nsc_ref[r] = uhist(r + n_tok)

    for ref in (sbq_ref, sbg_ref, dfq_ref, dfg_ref):
        ref[...] = jnp.zeros_like(ref)
    for t in range(n_tok):
        tabs = [jnp.broadcast_to(r[t:t + 1, :], (bsz, HEAD_DIM)) for r in (rc_ref, rsa_ref, rsb_ref)]
        put_pad(sbq_ref, t, tok(S_SB_Q, t))
        put_pad(sbg_ref, t, tok(S_SB_G, t))
        put_pad(dfq_ref, t, _rope_rows(tok(S_DF_Q, t), *tabs))
        put_pad(dfg_ref, t, tok(S_DF_G, t))
        put_heads(sbk_ref, t, tok(S_SB_K, t))
        put_heads(sbv_ref, t, tok(S_SB_V, t))
        put_heads(dfk_ref, t, _rope_rows(tok(S_DF_K, t), *tabs))
        put_heads(dfv_ref, t, tok(S_DF_V, t))


def decode_mix(proj, state_conf, state_sc, conf_w, conf_b, ln_g, ln_b, sc_w, rope_tabs, bsz, n_tok):
    cw, scw = conf_w.shape[0], sc_w.shape[0]

    def replicate(w):
        return jnp.broadcast_to(w[:, None, :], (w.shape[0], SUBLANES, w.shape[1]))

    act = jax.ShapeDtypeStruct((bsz * n_tok, GROUP_W), BF16)
    pad = jax.ShapeDtypeStruct((N_HEADS, bsz * PAD_T, HEAD_DIM), F32)
    hm = jax.ShapeDtypeStruct((bsz * n_tok * N_HEADS, HEAD_DIM), F32)
    return pl.pallas_call(
        functools.partial(_decode_mix_kernel, bsz=bsz, n_tok=n_tok, conf_w=cw, sc_w=scw),
        out_shape=[act, act, pad, pad, pad, pad, hm, hm, hm, hm,
                   jax.ShapeDtypeStruct(state_conf.shape, F32), jax.ShapeDtypeStruct(state_sc.shape, F32)],
        compiler_params=pltpu.CompilerParams(vmem_limit_bytes=VMEM_LIMIT),
        name="decode_mix",
    )(proj, state_conf, state_sc, replicate(conf_w), conf_b.reshape(1, -1), ln_g.reshape(1, -1),
      ln_b.reshape(1, -1), replicate(sc_w), *rope_tabs)


def _pick_tile(n, pref):
    t = min(n, pref)
    while n % t:
        t //= 2
    return t


def kernel(x_prompt, x_sample, cache_sb_k, cache_sb_v, cache_diff_k, cache_diff_v, state_conf_conv,
           state_short_conv, page_table, norm_g, w_in, conf_dw_w, conf_dw_b, conf_ln_g, conf_ln_b,
           diff_lq1, diff_lk1, diff_lq2, diff_lk2, diff_subln_g, sc_conv_w, w_out, final_norm_g):
    bp, tp, d = x_prompt.shape
    bs, ts, _ = x_sample.shape
    depth = w_in.shape[0]
    n_pages = page_table.shape[1]
    page = cache_sb_k.shape[2]
    n_past = n_pages * page
    pos_p = jnp.arange(tp, dtype=jnp.int32)
    pos_s = n_past + jnp.arange(ts, dtype=jnp.int32)
    assert bp == 1 and ts <= PAD_T

    caches = [c.reshape(c.shape[0], c.shape[1], page * N_HEADS, HEAD_DIM)
              for c in (cache_sb_k, cache_sb_v, cache_diff_k, cache_diff_v)]
    w_in_b = w_in.astype(BF16)
    w_out_b = w_out.astype(BF16)

    hp = x_prompt.reshape(bp * tp, d)
    hs = jnp.swapaxes(x_sample, 0, 1).reshape(ts * bs, d)
    conf_states = jnp.swapaxes(state_conf_conv, 1, 2)
    sc_states = jnp.swapaxes(state_short_conv, 1, 2)
    tm_p = _pick_tile(bp * tp, 512)
    slabs_per_step = 5
    assert N_IN_SLICES % slabs_per_step == 0
    tm_s = _pick_tile(bs * ts, 128)
    tq_sb = _pick_tile(tp, 512)
    tk_sb = _pick_tile(tq_sb, 256)
    tk_df = _pick_tile(tp // 2, 512)
    tb_mix = _pick_tile(tp, 256)
    assert tb_mix >= CONF_HALO >= conf_dw_w.shape[1] - 1 and SC_HALO >= sc_conv_w.shape[1] - 1
    rope_tabs_p = _rope_tables(pos_p)
    rope_tabs_s = _rope_tables(pos_s)
    pps = _pick_tile(n_pages, 16)
    pgroup = _pick_tile(pps, 2)

    p_conv = [[], []]
    s_new = [[] for _ in range(6)]
    kv_bufs = None
    dk_buf = None
    for l in range(depth):
        lam_init = 0.8 - 0.6 * math.exp(-0.3 * l)
        lam_params = (diff_lq1[l], diff_lk1[l], diff_lq2[l], diff_lk2[l])
        final = l == depth - 1

        proj, kv_bufs = in_proj(hp, norm_g[l], w_in_b, tm_p, slabs_per_step,
                                direct=(S_SB_K, S_SB_V, S_DF_V), direct_bufs=kv_bufs, layer=l,
                                depth=depth)
        ya, yd, dqb, dk_buf, dkb, dvb, new_conf, new_sc = prompt_mix(
            proj, conf_dw_w[l], conf_dw_b[l], conf_ln_g[l], conf_ln_b[l], sc_conv_w[l], rope_tabs_p,
            tb_mix, k_buf=dk_buf, layer=l, depth=depth)
        yb = sb_prompt(proj, tq_sb, tk_sb, 2)
        yc = diff_prompt(dqb, dkb, dvb, proj, lam_params, diff_subln_g[l], lam_init, tk_df)
        hp = out_proj((ya, yb, yc, yd), w_out_b, l, hp, final_norm_g, final,
                      _pick_tile(bp * tp, 512))
        p_conv[0].append(new_conf[None])
        p_conv[1].append(new_sc[None])
        heads = lambda a, bb, tt: a.reshape(bb, tt, N_HEADS, HEAD_DIM)

        proj, _ = in_proj(hs, norm_g[l], w_in_b, tm_s, slabs_per_step, layer=l)
        ya, yd, sbq, sbg, dfq, dfg, sbk, sbv, dfk, dfv, new_conf, new_sc = decode_mix(
            proj, conf_states[l], sc_states[l], conf_dw_w[l], conf_dw_b[l], conf_ln_g[l],
            conf_ln_b[l], sc_conv_w[l], rope_tabs_s, bs, ts)
        new_rows = lambda a: a.reshape(bs, ts * N_HEADS, HEAD_DIM)
        yb = sb_decode(sbq, new_rows(sbk), new_rows(sbv), sbg, caches[0], caches[1], page_table, l, ts)
        yc = diff_decode(dfq, new_rows(dfk), new_rows(dfv), dfg, caches[2], caches[3], page_table,
                         lam_params, diff_subln_g[l], lam_init, l, pps, pgroup)
        flat = lambda a: jnp.swapaxes(a[:, :ts], 0, 1).reshape(ts * bs, GROUP_W).astype(BF16)
        hs = out_proj((ya, flat(yb), flat(yc), yd), w_out_b, l, hs, final_norm_g, final, tm_s)
        for n, a in enumerate((sbk, sbv, dfk, dfv)):
            s_new[n].append(a.reshape(bs, ts, N_HEADS, HEAD_DIM))
        s_new[4].append(new_conf)
        s_new[5].append(new_sc)

    y_prompt = hp.reshape(bp, tp, d)
    y_sample = jnp.swapaxes(hs.reshape(ts, bs, d), 0, 1)
    p_kv = [a.reshape(depth, bp, tp, N_HEADS, HEAD_DIM)
            for a in (kv_bufs[0], kv_bufs[1], dk_buf, kv_bufs[2])]
    s_out = [jnp.stack(a, 0) for a in s_new]
    s_out[4:] = [jnp.swapaxes(a, 1, 2) for a in s_out[4:]]
    return (y_prompt, y_sample, *p_kv, *[jnp.stack(a, 0) for a in p_conv], *s_out)
```

```python
import functools
import math

import jax
import jax.numpy as jnp
from jax import lax
from jax.experimental import pallas as pl
from jax.experimental.pallas import tpu as pltpu

F32 = jnp.float32
BF16 = jnp.bfloat16

N_MIXERS = 4
N_HEADS = 4
HEAD_DIM = 128
GROUP_W = N_HEADS * HEAD_DIM
DIFF_QK = HEAD_DIM // 2
ROT_DIM = DIFF_QK // 4
ROPE_THETA = 500000.0
N_IN_SLICES = 15
EPS = 1e-6
LOG2E = 1.4426950408889634
SB_SCALE_LOG2 = LOG2E / math.sqrt(HEAD_DIM)
DIFF_SCALE = 1.0 / math.sqrt(DIFF_QK)
PAD_T = 8
SB_EXIT_LOG2 = -104.0 * LOG2E
VMEM_LIMIT = 56 * 1024 * 1024
DECODE_PREFETCH_STEPS = 2

(S_CONF_A, S_CONF_B, S_CONF_G, S_SB_Q, S_SB_K, S_SB_V, S_SB_G, S_DF_Q, S_DF_K, S_DF_V, S_DF_G,
 S_SC_H, S_SC_B, S_SC_C, S_SC_G) = range(N_IN_SLICES)


def _silu(x):
    return x * (1.0 / (1.0 + jnp.exp(-x)))


def _cparams(sem):
    return pltpu.CompilerParams(dimension_semantics=sem, vmem_limit_bytes=VMEM_LIMIT)


def _log2(n):
    assert n & (n - 1) == 0
    return n.bit_length() - 1


def _head_cols(h):
    return slice(h * HEAD_DIM, (h + 1) * HEAD_DIM)


def _lane_chunks(x):
    return [x[:, c * HEAD_DIM:(c + 1) * HEAD_DIM] for c in range(x.shape[1] // HEAD_DIM)]


def _tree(op, xs):
    while len(xs) > 1:
        xs = [op(xs[n], xs[n + 1]) if n + 1 < len(xs) else xs[n] for n in range(0, len(xs), 2)]
    return xs[0]


def _qk(q, k):
    return lax.dot_general(q, k.astype(BF16), (((1,), (1,)), ((), ())), preferred_element_type=F32)


def _with_ones(v):
    v = v.astype(BF16)
    return jnp.concatenate([v, jnp.ones_like(v)], axis=1)


def _store_head_major(dst_ref, x):
    rows = x.shape[0]
    for h in range(N_HEADS):
        dst_ref[pl.ds(h, rows, stride=N_HEADS), :] = x[:, _head_cols(h)]


def _store_layer_head_major(dst_ref, x, layer, whole):
    if not whole:
        _store_head_major(dst_ref, x)
        return
    for other in range(dst_ref.shape[0]):
        if other != layer:
            dst_ref[other] = jnp.zeros(dst_ref.shape[1:], dst_ref.dtype)
    _store_head_major(dst_ref.at[layer], x)


def _inproj_kernel(x_ref, g_ref, w_ref, *rest, slabs_per_step, direct, layer, whole):
    n_direct = len(direct)
    o_ref = rest[-n_direct - 2]
    direct_refs = rest[-n_direct - 1:-1]
    xn_ref = rest[-1]
    j = pl.program_id(1)

    @pl.when(j == 0)
    def _():
        x = x_ref[...]
        ms = jnp.mean(x * x, axis=-1, keepdims=True)
        xn_ref[...] = ((x * lax.rsqrt(ms + EPS)) * g_ref[...]).astype(BF16)

    res = jnp.dot(xn_ref[...], w_ref[...], preferred_element_type=F32)
    for k in range(slabs_per_step):
        o_ref[k] = res[:, k * GROUP_W:(k + 1) * GROUP_W]
    for slab, d_ref in zip(direct, direct_refs):
        @pl.when(j == slab // slabs_per_step)
        def _(slab=slab, d_ref=d_ref):
            k = slab % slabs_per_step
            _store_layer_head_major(d_ref, res[:, k * GROUP_W:(k + 1) * GROUP_W], layer, whole)


def _layer_rows_spec(rows, depth, layer, whole, grid_rank):
    if grid_rank == 1:
        if whole:
            return pl.BlockSpec((depth, rows, HEAD_DIM), lambda i: (0, i, 0))
        return pl.BlockSpec((None, rows, HEAD_DIM), lambda i: (layer, i, 0))
    if whole:
        return pl.BlockSpec((depth, rows, HEAD_DIM), lambda i, j: (0, i, 0))
    return pl.BlockSpec((None, rows, HEAD_DIM), lambda i, j: (layer, i, 0))


def in_proj(x2d, g, w_bf16, tm, slabs_per_step, direct=(), direct_bufs=None, layer=0, depth=1):
    m, d = x2d.shape
    n = w_bf16.shape[2]
    ns = n // GROUP_W
    sps = slabs_per_step
    n_direct = len(direct)
    alias_in = list(direct_bufs) if direct_bufs is not None else []
    whole = not alias_in
    direct_shape = jax.ShapeDtypeStruct((depth, m * N_HEADS, HEAD_DIM), F32)
    outs = pl.pallas_call(
        functools.partial(_inproj_kernel, slabs_per_step=sps, direct=tuple(direct), layer=layer,
                          whole=whole),
        grid=(m // tm, ns // sps),
        in_specs=[pl.BlockSpec((tm, d), lambda i, j: (i, 0)),
                  pl.BlockSpec((1, d), lambda i, j: (0, 0)),
                  pl.BlockSpec((None, d, sps * GROUP_W), lambda i, j: (layer, 0, j))]
                 + [pl.BlockSpec(memory_space=pl.ANY)] * len(alias_in),
        out_specs=[pl.BlockSpec((sps, tm, GROUP_W), lambda i, j: (j, i, 0))]
                  + [_layer_rows_spec(tm * N_HEADS, depth, layer, whole, 2)] * n_direct,
        out_shape=[jax.ShapeDtypeStruct((ns, m, GROUP_W), F32)] + [direct_shape] * n_direct,
        input_output_aliases={3 + n: 1 + n for n in range(len(alias_in))},
        scratch_shapes=[pltpu.VMEM((tm, d), BF16)],
        compiler_params=_cparams(("parallel", "arbitrary")),
        name="in_proj",
    )(x2d, g.reshape(1, d), w_bf16, *alias_in)
    return outs[0], tuple(outs[1:])


def _outproj_kernel(ya_ref, yb_ref, yc_ref, yd_ref, w_ref, x_ref, fg_ref, o_ref, *, final):
    acc = x_ref[...]
    for m, y_ref in enumerate((ya_ref, yb_ref, yc_ref, yd_ref)):
        acc = acc + jnp.dot(y_ref[...], w_ref[m * GROUP_W:(m + 1) * GROUP_W, :],
                            preferred_element_type=F32)
    if final:
        ms = jnp.mean(acc * acc, axis=-1, keepdims=True)
        acc = (acc * lax.rsqrt(ms + EPS)) * fg_ref[...]
    o_ref[...] = acc


def out_proj(ys, w_bf16, layer, x2d, final_g, final, tm):
    m, d = x2d.shape
    yspec = pl.BlockSpec((tm, GROUP_W), lambda i: (i, 0))
    return pl.pallas_call(
        functools.partial(_outproj_kernel, final=final),
        grid=(m // tm,),
        in_specs=[yspec, yspec, yspec, yspec,
                  pl.BlockSpec((None, N_MIXERS * GROUP_W, d), lambda i: (layer, 0, 0)),
                  pl.BlockSpec((tm, d), lambda i: (i, 0)),
                  pl.BlockSpec((1, d), lambda i: (0, 0))],
        out_specs=pl.BlockSpec((tm, d), lambda i: (i, 0)),
        out_shape=jax.ShapeDtypeStruct((m, d), F32),
        compiler_params=_cparams(("parallel",)),
        name="out_proj",
    )(*ys, w_bf16, x2d, final_g.reshape(1, d))


def _sb_terms(z):
    neg = -z
    t = jnp.log2(1.0 + jnp.exp2(jnp.minimum(z, neg)))
    return jnp.minimum(z, 0.0) - t, jnp.minimum(neg, 0.0) - t


def _suffix_sums(lk, u):
    hi = lk.astype(BF16)
    lo = (lk - hi.astype(F32)).astype(BF16)
    return (jnp.dot(hi, u, preferred_element_type=F32) + jnp.dot(lo, u, preferred_element_type=F32))


def _strict_lower(n):
    row = lax.broadcasted_iota(jnp.int32, (n, n), 0)
    col = lax.broadcasted_iota(jnp.int32, (n, n), 1)
    return row, col


def _sb_prompt_kernel(q_ref, k_ref, v_ref, g_ref, o_ref, *scratch, tq, tk, hps):
    i = pl.program_id(1)
    acc_refs, carry_refs = scratch[:hps], scratch[hps:]
    n_diag = tq // tk
    row, col = _strict_lower(tk)
    u = jnp.where(row > col, 1.0, 0.0).astype(BF16)
    qs = [q_ref[:, _head_cols(h)].astype(BF16) for h in range(hps)]

    def scores(h, jb):
        start = pl.multiple_of(jb * tk, tk)
        z = _qk(qs[h], k_ref[pl.ds(start, tk), _head_cols(h)]) * SB_SCALE_LOG2
        return z, v_ref[pl.ds(start, tk), _head_cols(h)].astype(BF16)

    def sweep(jb, visible, first):
        for h in range(hps):
            z, v = scores(h, jb)
            ls, lk = _sb_terms(z)
            if visible is not None:
                lk = jnp.where(visible, lk, 0.0)
            e = ls + _suffix_sums(lk, u)
            if not first:
                e = e + carry_refs[h][...]
            a = jnp.exp2(e)
            if visible is not None:
                a = jnp.where(visible, a, 0.0)
            pv = jnp.dot(a.astype(BF16), v, preferred_element_type=F32)
            dc = jnp.sum(lk, axis=1, keepdims=True)
            if first:
                acc_refs[h][...] = pv
                carry_refs[h][...] = dc
            else:
                acc_refs[h][...] += pv
                carry_refs[h][...] += dc

    qrow = lax.broadcasted_iota(jnp.int32, (tq, tk), 0)
    kcol = lax.broadcasted_iota(jnp.int32, (tq, tk), 1)
    for d in reversed(range(n_diag)):
        sweep(i * n_diag + d, kcol + d * tk < qrow, d == n_diag - 1)

    def live():
        return _tree(jnp.maximum, [jnp.max(c[...]) for c in carry_refs]) > SB_EXIT_LOG2

    def cond(state):
        jb, go = state
        return jnp.logical_and(jb >= 0, go)

    def body(state):
        jb, _ = state
        sweep(jb, None, False)
        return jb - 1, live()

    lax.while_loop(cond, body, (i * n_diag - 1, live()))
    for h in range(hps):
        o_ref[:, _head_cols(h)] = (acc_refs[h][...] * _silu(g_ref[:, _head_cols(h)])).astype(o_ref.dtype)


def sb_prompt(proj, tq, tk, heads_per_step):
    _, t, _ = proj.shape
    hps = heads_per_step
    width = hps * HEAD_DIM

    def spec(rows, s, per_q):
        if per_q:
            return pl.BlockSpec((None, rows, width), lambda h, i: (s, i, h))
        return pl.BlockSpec((None, rows, width), lambda h, i: (s, 0, h))

    return pl.pallas_call(
        functools.partial(_sb_prompt_kernel, tq=tq, tk=tk, hps=hps),
        grid=(N_HEADS // hps, t // tq),
        in_specs=[spec(tq, S_SB_Q, True), spec(t, S_SB_K, False), spec(t, S_SB_V, False),
                  spec(tq, S_SB_G, True)],
        out_specs=pl.BlockSpec((tq, width), lambda h, i: (i, h)),
        out_shape=jax.ShapeDtypeStruct((t, GROUP_W), BF16),
        scratch_shapes=[pltpu.VMEM((tq, HEAD_DIM), F32)] * hps + [pltpu.VMEM((tq, 1), F32)] * hps,
        compiler_params=_cparams(("parallel", "arbitrary")),
        name="sb_prompt",
    )(proj, proj, proj, proj)


def _page_masks(rows, rows_per_head, page_rows):
    row = lax.broadcasted_iota(jnp.int32, (rows, page_rows), 0)
    col = lax.broadcasted_iota(jnp.int32, (rows, page_rows), 1)
    own = (col & (N_HEADS - 1)) == (row >> _log2(rows_per_head))
    return own, col >> _log2(N_HEADS), row & (PAD_T - 1)


def _sb_decode_kernel(pt_ref, q_ref, kn_ref, vn_ref, g_ref, u_ref, ck_ref, cv_ref, o_ref,
                      kbuf, vbuf, newk, newv, sem, acc_ref, carry_ref, *, layer, n_pages, n_tok):
    b = pl.program_id(0)
    rows = N_HEADS * PAD_T
    page_rows = newk.shape[0]

    def page_copies(p, slot):
        pg = pt_ref[b, p]
        return (pltpu.make_async_copy(ck_ref.at[layer, pg], kbuf.at[slot], sem.at[0, slot]),
                pltpu.make_async_copy(cv_ref.at[layer, pg], vbuf.at[slot], sem.at[1, slot]))

    def start_fetch(p, slot):
        for c in page_copies(p, slot):
            c.start()

    def wait_fetch(p, slot):
        for c in page_copies(p, slot):
            c.wait()

    start_fetch(n_pages - 1, 0)

    qall = jnp.concatenate([q_ref[h] for h in range(N_HEADS)], axis=0).astype(BF16)
    own, key, tok = _page_masks(rows, PAD_T, page_rows)
    real_row = (lax.broadcasted_iota(jnp.int32, (rows, 1), 0) & (PAD_T - 1)) < n_tok
    u = u_ref[...]

    def attend(k, v, visible, carry):
        ls, lk = _sb_terms(_qk(qall, k) * SB_SCALE_LOG2)
        lk = jnp.where(visible, lk, 0.0)
        e = ls + _suffix_sums(lk, u)
        if carry is not None:
            e = e + carry
        a = jnp.where(visible, jnp.exp2(e), 0.0).astype(BF16)
        return (jnp.dot(a, v.astype(BF16), preferred_element_type=F32),
                jnp.sum(lk, axis=1, keepdims=True))

    def live(carry):
        return jnp.max(jnp.where(real_row, carry, -jnp.inf)) > SB_EXIT_LOG2

    newk[...] = jnp.zeros_like(newk)
    newv[...] = jnp.zeros_like(newv)
    newk[0:kn_ref.shape[0], :] = kn_ref[...]
    newv[0:vn_ref.shape[0], :] = vn_ref[...]
    pv, dc = attend(newk[...], newv[...], jnp.logical_and(own, key < tok), None)
    acc_ref[...] = pv
    carry_ref[...] = dc

    def cond(state):
        p, go = state
        return jnp.logical_and(p >= 0, go)

    def body(state):
        p, _ = state
        slot = (n_pages - 1 - p) % 2
        wait_fetch(p, slot)

        @pl.when(p >= 1)
        def _():
            start_fetch(p - 1, 1 - slot)

        carry = carry_ref[...]
        pv, dc = attend(kbuf[slot], vbuf[slot], own, carry)
        acc_ref[...] += pv
        carry = carry + dc
        carry_ref[...] = carry
        return p - 1, live(carry)

    p_end, _ = lax.while_loop(cond, body, (jnp.int32(n_pages - 1), live(carry_ref[...])))

    @pl.when(p_end >= 0)
    def _():
        wait_fetch(p_end, (n_pages - 1 - p_end) % 2)

    acc = acc_ref[...]
    o_ref[...] = jnp.concatenate([acc[h * PAD_T:(h + 1) * PAD_T] * _silu(g_ref[h])
                                  for h in range(N_HEADS)], axis=1)


def sb_decode(q8, kn, vn, g8, cache_k, cache_v, page_table, layer, n_tok):
    bsz = page_table.shape[0]
    n_pages = page_table.shape[1]
    page_rows = cache_k.shape[2]
    tok_spec = pl.BlockSpec((None, PAD_T, GROUP_W), lambda b, pt: (b, 0, 0))
    head_spec = pl.BlockSpec((N_HEADS, PAD_T, HEAD_DIM), lambda b, pt: (0, b, 0))
    new_spec = pl.BlockSpec((None, kn.shape[1], HEAD_DIM), lambda b, pt: (b, 0, 0))
    any_spec = pl.BlockSpec(memory_space=pl.ANY)
    rows = N_HEADS * PAD_T
    row, col = _strict_lower(page_rows)
    u = jnp.where(row > col, 1.0, 0.0).astype(BF16)
    page_buf = pltpu.VMEM((2, page_rows, HEAD_DIM), F32)
    new_buf = pltpu.VMEM((page_rows, HEAD_DIM), F32)
    return pl.pallas_call(
        functools.partial(_sb_decode_kernel, layer=layer, n_pages=n_pages, n_tok=n_tok),
        grid_spec=pltpu.PrefetchScalarGridSpec(
            num_scalar_prefetch=1,
            grid=(bsz,),
            in_specs=[head_spec, new_spec, new_spec, head_spec,
                      pl.BlockSpec((page_rows, page_rows), lambda b, pt: (0, 0)), any_spec, any_spec],
            out_specs=tok_spec,
            scratch_shapes=[page_buf, page_buf, new_buf, new_buf,
                            pltpu.SemaphoreType.DMA((2, 2)),
                            pltpu.VMEM((rows, HEAD_DIM), F32), pltpu.VMEM((rows, 1), F32)]),
        out_shape=jax.ShapeDtypeStruct((bsz, PAD_T, GROUP_W), F32),
        compiler_params=_cparams(("arbitrary",)),
        name="sb_decode",
    )(page_table, q8, kn, vn, g8, u, cache_k, cache_v)


def _lambda(lq1_ref, lk1_ref, lq2_ref, lk2_ref, lam_init):
    s1 = jnp.sum(lq1_ref[...] * lk1_ref[...], axis=-1, keepdims=True)
    s2 = jnp.sum(lq2_ref[...] * lk2_ref[...], axis=-1, keepdims=True)
    return jnp.exp(s1) - jnp.exp(s2) + lam_init


def _diff_finish(o1, o2, lam, sg, gate, lam_init):
    w = o1 - lam * o2
    ms = jnp.mean(w * w, axis=-1, keepdims=True)
    y = (w * lax.rsqrt(ms + EPS)) * sg
    return (y * (1.0 - lam_init)) * _silu(gate)


def _split_components(q):
    lane = lax.broadcasted_iota(jnp.int32, q.shape, 1)
    zero = jnp.zeros_like(q)
    return jnp.concatenate([jnp.where(lane < DIFF_QK, q, zero), jnp.where(lane >= DIFF_QK, q, zero)],
                           axis=0)


def _diff_prompt_kernel(q_ref, k_ref, v_ref, g_ref, lq1_ref, lk1_ref, lq2_ref, lk2_ref, sg_ref,
                        o_ref, s0_ref, s1_ref, mt0_ref, mt1_ref, m_ref, acc_ref, *, tq, tk, lam_init):
    i = pl.program_id(1)
    qq = _split_components(q_ref[...])
    m_ref[...] = jnp.full_like(m_ref, -jnp.inf)
    acc_ref[...] = jnp.zeros_like(acc_ref)
    bufs = ((s0_ref, mt0_ref), (s1_ref, mt1_ref))

    def scores(jt, diag=None):
        s = _qk(qq, k_ref[pl.ds(pl.multiple_of(jt * tk, tk), tk), :])
        if diag is not None:
            row = lax.broadcasted_iota(jnp.int32, s.shape, 0) & (tq - 1)
            col = lax.broadcasted_iota(jnp.int32, s.shape, 1) + diag * tk
            s = jnp.where(col <= row, s, -jnp.inf)
        return s

    def score_stage(slot, s):
        s_ref, mt_ref = bufs[slot]
        s_ref[...] = s
        mt_ref[...] = jnp.broadcast_to(jnp.max(s, axis=1, keepdims=True), mt_ref.shape)

    def value_stage(jt, slot):
        s_ref, mt_ref = bufs[slot]
        m_old = m_ref[...]
        m_new = jnp.maximum(m_old, mt_ref[...])
        alpha = jnp.exp2(m_old - m_new)
        p = jnp.concatenate([jnp.exp2(c - m_new) for c in _lane_chunks(s_ref[...])], axis=1)
        v = v_ref[pl.ds(pl.multiple_of(jt * tk, tk), tk), :]
        pv = jnp.dot(p.astype(BF16), _with_ones(v), preferred_element_type=F32)
        acc_ref[...] = jnp.concatenate([alpha, alpha], axis=1) * acc_ref[...] + pv
        m_ref[...] = m_new

    d0 = 2 * i

    @pl.when(i == 0)
    def _():
        score_stage(0, scores(0, 0))
        score_stage(1, scores(1, 1))
        value_stage(0, 0)
        value_stage(1, 1)

    @pl.when(i > 0)
    def _():
        score_stage(0, scores(0))

        def body(g, c):
            score_stage(1, scores(2 * g + 1))
            value_stage(2 * g, 0)
            score_stage(0, scores(2 * g + 2))
            value_stage(2 * g + 1, 1)
            return c

        lax.fori_loop(0, i - 1, body, 0)
        score_stage(1, scores(d0 - 1))
        value_stage(d0 - 2, 0)
        score_stage(0, scores(d0, 0))
        value_stage(d0 - 1, 1)
        score_stage(1, scores(d0 + 1, 1))
        value_stage(d0, 0)
        value_stage(d0 + 1, 1)

    acc = acc_ref[...]
    o = acc[:, :HEAD_DIM] / acc[:, HEAD_DIM:]
    lam = _lambda(lq1_ref, lk1_ref, lq2_ref, lk2_ref, lam_init)
    o_ref[...] = _diff_finish(o[:tq], o[tq:], lam, sg_ref[...], g_ref[...], lam_init).astype(o_ref.dtype)


def diff_prompt(qb, kb, vb, proj, lam_params, subln_g, lam_init, tk):
    t = qb.shape[0]
    tq = 2 * tk
    small = pl.BlockSpec((1, DIFF_QK), lambda h, i: (0, 0))
    return pl.pallas_call(
        functools.partial(_diff_prompt_kernel, tq=tq, tk=tk, lam_init=lam_init),
        grid=(N_HEADS, t // tq),
        in_specs=[pl.BlockSpec((tq, HEAD_DIM), lambda h, i: (i, h)),
                  pl.BlockSpec((t, HEAD_DIM), lambda h, i: (0, h)),
                  pl.BlockSpec((t, HEAD_DIM), lambda h, i: (0, h)),
                  pl.BlockSpec((None, tq, HEAD_DIM), lambda h, i: (S_DF_G, i, h)),
                  small, small, small, small,
                  pl.BlockSpec((1, HEAD_DIM), lambda h, i: (0, 0))],
        out_specs=pl.BlockSpec((tq, HEAD_DIM), lambda h, i: (i, h)),
        out_shape=jax.ShapeDtypeStruct((t, GROUP_W), BF16),
        scratch_shapes=[pltpu.VMEM((2 * tq, tk), F32), pltpu.VMEM((2 * tq, tk), F32),
                        pltpu.VMEM((2 * tq, HEAD_DIM), F32), pltpu.VMEM((2 * tq, HEAD_DIM), F32),
                        pltpu.VMEM((2 * tq, HEAD_DIM), F32), pltpu.VMEM((2 * tq, 2 * HEAD_DIM), F32)],
        compiler_params=_cparams(("parallel", "arbitrary")),
        name="diff_prompt",
    )(qb, kb, vb, proj, *[p.reshape(1, DIFF_QK) for p in lam_params], subln_g.reshape(1, HEAD_DIM))


def _diff_decode_kernel(pt_ref, q_ref, kn_ref, vn_ref, g_ref, lq1_ref, lk1_ref, lq2_ref, lk2_ref,
                        sg_ref, ck_ref, cv_ref, o_ref, kbuf, vbuf, sem, m_ref, acc_ref, newk, newv, *,
                        layer, pages_per_step, n_chunks, group, lam_init):
    b = pl.program_id(0)
    c = pl.program_id(1)
    hrows = 2 * PAD_T
    rows = N_HEADS * hrows
    page_rows = newk.shape[0]

    n_slots = kbuf.shape[0]
    n_steps = pl.num_programs(0) * n_chunks
    step = b * n_chunks + c
    slot = step % n_slots

    def step_copies(t):
        bb, cc, sl = t // n_chunks, t % n_chunks, t % n_slots
        copies = []
        for n in range(pages_per_step):
            pg = pt_ref[bb, cc * pages_per_step + n]
            copies.append(pltpu.make_async_copy(ck_ref.at[layer, pg], kbuf.at[sl, n], sem.at[sl, 0, n]))
            copies.append(pltpu.make_async_copy(cv_ref.at[layer, pg], vbuf.at[sl, n], sem.at[sl, 1, n]))
        return copies

    def start_step(t):
        for n, cp in enumerate(step_copies(t)):
            cp.start(priority=n % 2)

    @pl.when(step == 0)
    def _():
        for t in range(DECODE_PREFETCH_STEPS):
            @pl.when(t < n_steps)
            def _(t=t):
                start_step(t)

    @pl.when(step + DECODE_PREFETCH_STEPS < n_steps)
    def _():
        start_step(step + DECODE_PREFETCH_STEPS)

    for cp in step_copies(step):
        cp.wait()

    qall = jnp.concatenate([_split_components(q_ref[h] * (DIFF_SCALE * LOG2E))
                            for h in range(N_HEADS)], axis=0).astype(BF16)
    zq = jnp.zeros_like(qall)
    qpair = jnp.concatenate([jnp.concatenate([qall, zq], axis=1),
                             jnp.concatenate([zq, qall], axis=1)], axis=0)
    own, key, tok = _page_masks(rows, hrows, page_rows)

    @pl.when(c == 0)
    def _():
        m_ref[...] = jnp.full_like(m_ref, -jnp.inf)
        acc_ref[...] = jnp.zeros_like(acc_ref)

    def page_scores(ks):
        out = []
        for n in range(0, len(ks) - 1, 2):
            kk = jnp.concatenate([ks[n].astype(BF16), ks[n + 1].astype(BF16)], axis=1)
            s2 = _qk(qpair, kk)
            out += [s2[:rows], s2[rows:]]
        if len(ks) % 2:
            out.append(_qk(qall, ks[-1]))
        return out

    def update(ks, vs, visible):
        s = jnp.concatenate([jnp.where(visible, z, -jnp.inf) for z in page_scores(ks)], axis=1)
        m_prev = m_ref[...]
        m_new = jnp.maximum(m_prev, jnp.max(s, axis=1, keepdims=True))
        p = jnp.exp2(s - m_new).astype(BF16)
        pv = _tree(jnp.add, [jnp.dot(p[:, n * page_rows:(n + 1) * page_rows], _with_ones(v),
                                     preferred_element_type=F32) for n, v in enumerate(vs)])
        return m_new, jnp.exp2(m_prev - m_new) * acc_ref[...] + pv

    for g0 in range(0, pages_per_step, group):
        m_new, acc_new = update([kbuf[slot, n] for n in range(g0, g0 + group)],
                                [vbuf[slot, n] for n in range(g0, g0 + group)], own)
        m_ref[...] = m_new
        acc_ref[...] = acc_new

    @pl.when(c == pl.num_programs(1) - 1)
    def _():
        newk[...] = jnp.zeros_like(newk)
        newv[...] = jnp.zeros_like(newv)
        newk[0:kn_ref.shape[0], :] = kn_ref[...]
        newv[0:vn_ref.shape[0], :] = vn_ref[...]
        _, acc = update([newk[...]], [newv[...]], jnp.logical_and(own, key <= tok))
        o = acc[:, :HEAD_DIM] / acc[:, HEAD_DIM:]
        lam = _lambda(lq1_ref, lk1_ref, lq2_ref, lk2_ref, lam_init)
        sg = sg_ref[...]
        outs = []
        for h in range(N_HEADS):
            o1 = o[h * hrows:h * hrows + PAD_T]
            o2 = o[h * hrows + PAD_T:(h + 1) * hrows]
            outs.append(_diff_finish(o1, o2, lam, sg, g_ref[h], lam_init))
        o_ref[...] = jnp.concatenate(outs, axis=1)


def diff_decode(q8, kn, vn, g8, cache_k, cache_v, page_table, lam_params, subln_g, lam_init,
                layer, pages_per_step, group):
    bsz = page_table.shape[0]
    n_pages = page_table.shape[1]
    page_rows = cache_k.shape[2]
    pps = pages_per_step
    tok_spec = pl.BlockSpec((None, PAD_T, GROUP_W), lambda b, c, pt: (b, 0, 0))
    head_spec = pl.BlockSpec((N_HEADS, PAD_T, HEAD_DIM), lambda b, c, pt: (0, b, 0))
    new_spec = pl.BlockSpec((None, kn.shape[1], HEAD_DIM), lambda b, c, pt: (b, 0, 0))
    small = pl.BlockSpec((1, DIFF_QK), lambda b, c, pt: (0, 0))

    rows = N_HEADS * 2 * PAD_T
    any_spec = pl.BlockSpec(memory_space=pl.ANY)
    n_slots = DECODE_PREFETCH_STEPS + 1
    page_bufs = pltpu.VMEM((n_slots, pps, page_rows, HEAD_DIM), F32)
    new_buf = pltpu.VMEM((page_rows, HEAD_DIM), F32)
    return pl.pallas_call(
        functools.partial(_diff_decode_kernel, layer=layer, pages_per_step=pps,
                          n_chunks=n_pages // pps, group=group, lam_init=lam_init),
        grid_spec=pltpu.PrefetchScalarGridSpec(
            num_scalar_prefetch=1,
            grid=(bsz, n_pages // pps),
            in_specs=[head_spec, new_spec, new_spec, head_spec, small, small, small, small,
                      pl.BlockSpec((1, HEAD_DIM), lambda b, c, pt: (0, 0)), any_spec, any_spec],
            out_specs=tok_spec,
            scratch_shapes=[page_bufs, page_bufs, pltpu.SemaphoreType.DMA((n_slots, 2, pps)),
                            pltpu.VMEM((rows, 1), F32), pltpu.VMEM((rows, 2 * HEAD_DIM), F32),
                            new_buf, new_buf]),
        out_shape=jax.ShapeDtypeStruct((bsz, PAD_T, GROUP_W), F32),
        compiler_params=_cparams(("arbitrary", "arbitrary")),
        name="diff_decode",
    )(page_table, q8, kn, vn, g8, *[p.reshape(1, DIFF_QK) for p in lam_params],
      subln_g.reshape(1, HEAD_DIM), cache_k, cache_v)


CONF_HALO = 32
SC_HALO = 8
SUBLANES = 8
MIX_ROWS = 32


def _rope_tables(pos):
    half = ROT_DIM // 2
    inv = ROPE_THETA ** (-2.0 * jnp.arange(half, dtype=F32) / ROT_DIM)
    ang = pos.astype(F32)[:, None] * inv[None, :]
    cos, sin = jnp.cos(ang), jnp.sin(ang)
    j = jnp.arange(HEAD_DIM) % DIFF_QK
    f = j % half
    c = jnp.where(j < ROT_DIM, cos[:, f], 1.0)
    sa = jnp.where((j >= half) & (j < ROT_DIM), sin[:, f], 0.0)
    sb = jnp.where(j < half, -sin[:, f], 0.0)
    return c, sa, sb


def _rope_rows(x, c, sa, sb):
    half = ROT_DIM // 2
    out = []
    for xh in _lane_chunks(x):
        out.append(xh * c + pltpu.roll(xh, half, 1) * sa + pltpu.roll(xh, HEAD_DIM - half, 1) * sb)
    return jnp.concatenate(out, axis=1)


def _prompt_mix_kernel(a_ref, b_ref, g_ref, dq_ref, dk_ref, dv_ref, sh_ref, sb_ref, sc_ref, sg_ref,
                       cw_ref, cb_ref, lg_ref, lb_ref, sw_ref, rc_ref, rsa_ref, rsb_ref, *rest,
                       tb, conf_w, sc_w, layer, whole):
    ya_ref, yd_ref, dqb_ref, dk_out_ref, dkb_ref, dvb_ref, nconf_ref, nsc_ref, hb, hs, ub = rest[-11:]
    i = pl.program_id(0)

    @pl.when(i == 0)
    def _():
        hb[0:CONF_HALO, :] = jnp.zeros((CONF_HALO, GROUP_W), F32)
        ub[0:SC_HALO, :] = jnp.zeros((SC_HALO, GROUP_W), F32)

    @pl.when(i > 0)
    def _():
        hb[0:CONF_HALO, :] = hb[tb:tb + CONF_HALO, :]
        ub[0:SC_HALO, :] = ub[tb:tb + SC_HALO, :]

    hb[CONF_HALO:CONF_HALO + tb, :] = a_ref[...] * (1.0 / (1.0 + jnp.exp(-b_ref[...])))
    ub[SC_HALO:SC_HALO + tb, :] = sc_ref[...] * sh_ref[...]

    off = CONF_HALO - (conf_w - 1)
    span = hs.shape[1]
    for k in range(1, SUBLANES):
        hs[k - 1, :, :] = hb[k:k + span, :]

    def tap(w_ref, w):
        return jnp.concatenate([w_ref[w]] * (MIX_ROWS // SUBLANES), axis=0)

    def window(r):
        k, base = r % SUBLANES, r - r % SUBLANES
        if k == 0:
            return hb[base:base + MIX_ROWS, :]
        return hs[k - 1, base:base + MIX_ROWS, :]

    for r0 in range(0, tb, MIX_ROWS):
        rows = slice(r0, r0 + MIX_ROWS)
        acc = jnp.zeros((MIX_ROWS, GROUP_W), F32) + cb_ref[...]
        for w in range(conf_w):
            acc = acc + tap(cw_ref, w) * window(r0 + off + w)
        mu = jnp.mean(acc, axis=-1, keepdims=True)
        xc = acc - mu
        y = (xc * lax.rsqrt(jnp.mean(xc * xc, axis=-1, keepdims=True) + EPS)) * lg_ref[...] + lb_ref[...]
        ya_ref[rows, :] = (_silu(y) * _silu(g_ref[rows, :])).astype(ya_ref.dtype)
        u0 = r0 + SC_HALO - (sc_w - 1)
        conv = jnp.zeros((MIX_ROWS, GROUP_W), F32)
        for w in range(sc_w):
            conv = conv + tap(sw_ref, w) * ub[u0 + w:u0 + w + MIX_ROWS, :]
        yd_ref[rows, :] = ((sb_ref[rows, :] * conv) * _silu(sg_ref[rows, :])).astype(yd_ref.dtype)

    c, sa, sb = rc_ref[...], rsa_ref[...], rsb_ref[...]
    dqb_ref[...] = (_rope_rows(dq_ref[...], c, sa, sb) * (DIFF_SCALE * LOG2E)).astype(dqb_ref.dtype)
    dk = _rope_rows(dk_ref[...], c, sa, sb)
    _store_layer_head_major(dk_out_ref, dk, layer, whole)
    dkb_ref[...] = dk.astype(dkb_ref.dtype)
    dvb_ref[...] = dv_ref[...].astype(dvb_ref.dtype)

    @pl.when(i == pl.num_programs(0) - 1)
    def _():
        nconf_ref[...] = hb[CONF_HALO + tb - (conf_w - 1):CONF_HALO + tb, :]
        nsc_ref[...] = ub[SC_HALO + tb - (sc_w - 1):SC_HALO + tb, :]


def prompt_mix(proj, conf_w, conf_b, ln_g, ln_b, sc_w, rope_tabs, tb, k_buf=None, layer=0, depth=1):
    _, t, _ = proj.shape
    cw, scw = conf_w.shape[0], sc_w.shape[0]

    def slab(s):
        return pl.BlockSpec((None, tb, GROUP_W), lambda i: (s, i, 0))

    def full(r):
        return pl.BlockSpec((r, GROUP_W), lambda i: (0, 0))

    def taps(n):
        return pl.BlockSpec((n, SUBLANES, GROUP_W), lambda i: (0, 0, 0))

    def replicate(w):
        return jnp.broadcast_to(w[:, None, :], (w.shape[0], SUBLANES, w.shape[1]))

    row_blk = pl.BlockSpec((tb, GROUP_W), lambda i: (i, 0))
    tab_blk = pl.BlockSpec((tb, HEAD_DIM), lambda i: (i, 0))
    slabs = (S_CONF_A, S_CONF_B, S_CONF_G, S_DF_Q, S_DF_K, S_DF_V, S_SC_H, S_SC_B, S_SC_C, S_SC_G)
    act = jax.ShapeDtypeStruct((t, GROUP_W), BF16)
    alias_in = [] if k_buf is None else [k_buf]
    n_in = len(slabs) + 8
    return pl.pallas_call(
        functools.partial(_prompt_mix_kernel, tb=tb, conf_w=cw, sc_w=scw, layer=layer,
                          whole=not alias_in),
        grid=(t // tb,),
        in_specs=[slab(s) for s in slabs]
                 + [taps(cw), full(1), full(1), full(1), taps(scw), tab_blk, tab_blk, tab_blk]
                 + [pl.BlockSpec(memory_space=pl.ANY)] * len(alias_in),
        out_specs=[row_blk, row_blk, row_blk,
                   _layer_rows_spec(tb * N_HEADS, depth, layer, not alias_in, 1),
                   row_blk, row_blk, full(cw - 1), full(scw - 1)],
        out_shape=[act, act, act, jax.ShapeDtypeStruct((depth, t * N_HEADS, HEAD_DIM), F32), act, act,
                   jax.ShapeDtypeStruct((cw - 1, GROUP_W), F32),
                   jax.ShapeDtypeStruct((scw - 1, GROUP_W), F32)],
        scratch_shapes=[pltpu.VMEM((CONF_HALO + tb, GROUP_W), F32),
                        pltpu.VMEM((SUBLANES - 1, CONF_HALO + tb - SUBLANES, GROUP_W), F32),
                        pltpu.VMEM((SC_HALO + tb, GROUP_W), F32)],
        input_output_aliases={n_in: 3} if alias_in else {},
        compiler_params=_cparams(("arbitrary",)),
        name="prompt_mix",
    )(*([proj] * len(slabs)), replicate(conf_w), conf_b.reshape(1, -1), ln_g.reshape(1, -1),
      ln_b.reshape(1, -1), replicate(sc_w), *rope_tabs, *alias_in)


def _decode_mix_kernel(proj_ref, sconf_ref, ssc_ref, cw_ref, cb_ref, lg_ref, lb_ref, sw_ref,
                       rc_ref, rsa_ref, rsb_ref,
                       ya_ref, yd_ref, sbq_ref, sbg_ref, dfq_ref, dfg_ref, sbk_ref, sbv_ref, dfk_ref,
                       dfv_ref, nconf_ref, nsc_ref, *, bsz, n_tok, conf_w, sc_w):
    def tok(slab, t):
        return proj_ref[slab, t * bsz:(t + 1) * bsz, :]

    def put_tok(ref, t, x):
        ref[t * bsz:(t + 1) * bsz, :] = x.astype(ref.dtype)

    def put_pad(ref, t, x):
        for h in range(N_HEADS):
            ref[h, pl.ds(t, bsz, stride=PAD_T), :] = x[:, _head_cols(h)]

    def put_heads(ref, t, x):
        for h in range(N_HEADS):
            ref[pl.ds(t * N_HEADS + h, bsz, stride=n_tok * N_HEADS), :] = x[:, _head_cols(h)]

    def tap(w_ref, w):
        if bsz % SUBLANES:
            return jnp.broadcast_to(w_ref[w, 0:1, :], (bsz, GROUP_W))
        return jnp.concatenate([w_ref[w]] * (bsz // SUBLANES), axis=0)

    h = [tok(S_CONF_A, t) * (1.0 / (1.0 + jnp.exp(-tok(S_CONF_B, t)))) for t in range(n_tok)]

    def hist(r):
        return sconf_ref[r] if r < conf_w - 1 else h[r - (conf_w - 1)]

    for t in range(n_tok):
        acc = jnp.zeros((bsz, GROUP_W), F32) + cb_ref[...]
        for w in range(conf_w):
            acc = acc + tap(cw_ref, w) * hist(t + w)
        mu = jnp.mean(acc, axis=-1, keepdims=True)
        xc = acc - mu
        y = (xc * lax.rsqrt(jnp.mean(xc * xc, axis=-1, keepdims=True) + EPS)) * lg_ref[...] + lb_ref[...]
        put_tok(ya_ref, t, _silu(y) * _silu(tok(S_CONF_G, t)))
    for r in range(conf_w - 1):
        nconf_ref[r] = hist(r + n_tok)

    u = [tok(S_SC_C, t) * tok(S_SC_H, t) for t in range(n_tok)]

    def uhist(r):
        return ssc_ref[r] if r < sc_w - 1 else u[r - (sc_w - 1)]

    for t in range(n_tok):
        conv = jnp.zeros((bsz, GROUP_W), F32)
        for w in range(sc_w):
            conv = conv + tap(sw_ref, w) * uhist(t + w)
        put_tok(yd_ref, t, (tok(S_SC_B, t) * conv) * _silu(tok(S_SC_G, t)))
    for r in range(sc_w - 1):
        nsc_ref[r] = uhist(r + n_tok)

    for ref in (sbq_ref, sbg_ref, dfq_ref, dfg_ref):
        ref[...] = jnp.zeros_like(ref)
    for t in range(n_tok):
        tabs = [jnp.broadcast_to(r[t:t + 1, :], (bsz, HEAD_DIM)) for r in (rc_ref, rsa_ref, rsb_ref)]
        put_pad(sbq_ref, t, tok(S_SB_Q, t))
        put_pad(sbg_ref, t, tok(S_SB_G, t))
        put_pad(dfq_ref, t, _rope_rows(tok(S_DF_Q, t), *tabs))
        put_pad(dfg_ref, t, tok(S_DF_G, t))
        put_heads(sbk_ref, t, tok(S_SB_K, t))
        put_heads(sbv_ref, t, tok(S_SB_V, t))
        put_heads(dfk_ref, t, _rope_rows(tok(S_DF_K, t), *tabs))
        put_heads(dfv_ref, t, tok(S_DF_V, t))


def decode_mix(proj, state_conf, state_sc, conf_w, conf_b, ln_g, ln_b, sc_w, rope_tabs, bsz, n_tok):
    cw, scw = conf_w.shape[0], sc_w.shape[0]

    def replicate(w):
        return jnp.broadcast_to(w[:, None, :], (w.shape[0], SUBLANES, w.shape[1]))

    act = jax.ShapeDtypeStruct((bsz * n_tok, GROUP_W), BF16)
    pad = jax.ShapeDtypeStruct((N_HEADS, bsz * PAD_T, HEAD_DIM), F32)
    hm = jax.ShapeDtypeStruct((bsz * n_tok * N_HEADS, HEAD_DIM), F32)
    return pl.pallas_call(
        functools.partial(_decode_mix_kernel, bsz=bsz, n_tok=n_tok, conf_w=cw, sc_w=scw),
        out_shape=[act, act, pad, pad, pad, pad, hm, hm, hm, hm,
                   jax.ShapeDtypeStruct(state_conf.shape, F32), jax.ShapeDtypeStruct(state_sc.shape, F32)],
        compiler_params=pltpu.CompilerParams(vmem_limit_bytes=VMEM_LIMIT),
        name="decode_mix",
    )(proj, state_conf, state_sc, replicate(conf_w), conf_b.reshape(1, -1), ln_g.reshape(1, -1),
      ln_b.reshape(1, -1), replicate(sc_w), *rope_tabs)


def _pick_tile(n, pref):
    t = min(n, pref)
    while n % t:
        t //= 2
    return t


def kernel(x_prompt, x_sample, cache_sb_k, cache_sb_v, cache_diff_k, cache_diff_v, state_conf_conv,
           state_short_conv, page_table, norm_g, w_in, conf_dw_w, conf_dw_b, conf_ln_g, conf_ln_b,
           diff_lq1, diff_lk1, diff_lq2, diff_lk2, diff_subln_g, sc_conv_w, w_out, final_norm_g):
    bp, tp, d = x_prompt.shape
    bs, ts, _ = x_sample.shape
    depth = w_in.shape[0]
    n_pages = page_table.shape[1]
    page = cache_sb_k.shape[2]
    n_past = n_pages * page
    pos_p = jnp.arange(tp, dtype=jnp.int32)
    pos_s = n_past + jnp.arange(ts, dtype=jnp.int32)
    assert bp == 1 and ts <= PAD_T

    caches = [c.reshape(c.shape[0], c.shape[1], page * N_HEADS, HEAD_DIM)
              for c in (cache_sb_k, cache_sb_v, cache_diff_k, cache_diff_v)]
    w_in_b = w_in.astype(BF16)
    w_out_b = w_out.astype(BF16)

    hp = x_prompt.reshape(bp * tp, d)
    hs = jnp.swapaxes(x_sample, 0, 1).reshape(ts * bs, d)
    conf_states = jnp.swapaxes(state_conf_conv, 1, 2)
    sc_states = jnp.swapaxes(state_short_conv, 1, 2)
    tm_p = _pick_tile(bp * tp, 512)
    slabs_per_step = 5
    assert N_IN_SLICES % slabs_per_step == 0
    tm_s = _pick_tile(bs * ts, 128)
    tq_sb = _pick_tile(tp, 512)
    tk_sb = _pick_tile(tq_sb, 256)
    tk_df = _pick_tile(tp // 2, 512)
    tb_mix = _pick_tile(tp, 256)
    assert tb_mix >= CONF_HALO >= conf_dw_w.shape[1] - 1 and SC_HALO >= sc_conv_w.shape[1] - 1
    rope_tabs_p = _rope_tables(pos_p)
    rope_tabs_s = _rope_tables(pos_s)
    pps = _pick_tile(n_pages, 16)
    pgroup = _pick_tile(pps, 2)

    p_conv = [[], []]
    s_new = [[] for _ in range(6)]
    kv_bufs = None
    dk_buf = None
    for l in range(depth):
        lam_init = 0.8 - 0.6 * math.exp(-0.3 * l)
        lam_params = (diff_lq1[l], diff_lk1[l], diff_lq2[l], diff_lk2[l])
        final = l == depth - 1

        proj, kv_bufs = in_proj(hp, norm_g[l], w_in_b, tm_p, slabs_per_step,
                                direct=(S_SB_K, S_SB_V, S_DF_V), direct_bufs=kv_bufs, layer=l,
                                depth=depth)
        ya, yd, dqb, dk_buf, dkb, dvb, new_conf, new_sc = prompt_mix(
            proj, conf_dw_w[l], conf_dw_b[l], conf_ln_g[l], conf_ln_b[l], sc_conv_w[l], rope_tabs_p,
            tb_mix, k_buf=dk_buf, layer=l, depth=depth)
        yb = sb_prompt(proj, tq_sb, tk_sb, 2)
        yc = diff_prompt(dqb, dkb, dvb, proj, lam_params, diff_subln_g[l], lam_init, tk_df)
        hp = out_proj((ya, yb, yc, yd), w_out_b, l, hp, final_norm_g, final,
                      _pick_tile(bp * tp, 512))
        p_conv[0].append(new_conf[None])
        p_conv[1].append(new_sc[None])
        heads = lambda a, bb, tt: a.reshape(bb, tt, N_HEADS, HEAD_DIM)

        proj, _ = in_proj(hs, norm_g[l], w_in_b, tm_s, slabs_per_step, layer=l)
        ya, yd, sbq, sbg, dfq, dfg, sbk, sbv, dfk, dfv, new_conf, new_sc = decode_mix(
            proj, conf_states[l], sc_states[l], conf_dw_w[l], conf_dw_b[l], conf_ln_g[l],
            conf_ln_b[l], sc_conv_w[l], rope_tabs_s, bs, ts)
        new_rows = lambda a: a.reshape(bs, ts * N_HEADS, HEAD_DIM)
        yb = sb_decode(sbq, new_rows(sbk), new_rows(sbv), sbg, caches[0], caches[1], page_table, l, ts)
        yc = diff_decode(dfq, new_rows(dfk), new_rows(dfv), dfg, caches[2], caches[3], page_table,
                         lam_params, diff_subln_g[l], lam_init, l, pps, pgroup)
        flat = lambda a: jnp.swapaxes(a[:, :ts], 0, 1).reshape(ts * bs, GROUP_W).astype(BF16)
        hs = out_proj((ya, flat(yb), flat(yc), yd), w_out_b, l, hs, final_norm_g, final, tm_s)
        for n, a in enumerate((sbk, sbv, dfk, dfv)):
            s_new[n].append(a.reshape(bs, ts, N_HEADS, HEAD_DIM))
        s_new[4].append(new_conf)
        s_new[5].append(new_sc)

    y_prompt = hp.reshape(bp, tp, d)
    y_sample = jnp.swapaxes(hs.reshape(ts, bs, d), 0, 1)
    p_kv = [a.reshape(depth, bp, tp, N_HEADS, HEAD_DIM)
            for a in (kv_bufs[0], kv_bufs[1], dk_buf, kv_bufs[2])]
    s_out = [jnp.stack(a, 0) for a in s_new]
    s_out[4:] = [jnp.swapaxes(a, 1, 2) for a in s_out[4:]]
    return (y_prompt, y_sample, *p_kv, *[jnp.stack(a, 0) for a in p_conv], *s_out)
```

```python
import functools
import math

import jax
import jax.numpy as jnp
from jax import lax
from jax.experimental import pallas as pl
from jax.experimental.pallas import tpu as pltpu

F32 = jnp.float32
BF16 = jnp.bfloat16

N_MIXERS = 4
N_HEADS = 4
HEAD_DIM = 128
GROUP_W = N_HEADS * HEAD_DIM
DIFF_QK = HEAD_DIM // 2
ROT_DIM = DIFF_QK // 4
ROPE_THETA = 500000.0
N_IN_SLICES = 15
EPS = 1e-6
LOG2E = 1.4426950408889634
SB_SCALE_LOG2 = LOG2E / math.sqrt(HEAD_DIM)
DIFF_SCALE = 1.0 / math.sqrt(DIFF_QK)
PAD_T = 8
SB_EXIT_LOG2 = -104.0 * LOG2E
VMEM_LIMIT = 56 * 1024 * 1024
DECODE_PREFETCH_STEPS = 2
SB_LOOKAHEAD = 2

(S_CONF_A, S_CONF_B, S_CONF_G, S_SB_Q, S_SB_K, S_SB_V, S_SB_G, S_DF_Q, S_DF_K, S_DF_V, S_DF_G,
 S_SC_H, S_SC_B, S_SC_C, S_SC_G) = range(N_IN_SLICES)


def _silu(x):
    return x * (1.0 / (1.0 + jnp.exp(-x)))


def _cparams(sem):
    return pltpu.CompilerParams(dimension_semantics=sem, vmem_limit_bytes=VMEM_LIMIT)


def _log2(n):
    assert n & (n - 1) == 0
    return n.bit_length() - 1


def _head_cols(h):
    return slice(h * HEAD_DIM, (h + 1) * HEAD_DIM)


def _lane_chunks(x):
    return [x[:, c * HEAD_DIM:(c + 1) * HEAD_DIM] for c in range(x.shape[1] // HEAD_DIM)]


def _tree(op, xs):
    while len(xs) > 1:
        xs = [op(xs[n], xs[n + 1]) if n + 1 < len(xs) else xs[n] for n in range(0, len(xs), 2)]
    return xs[0]


def _qk(q, k):
    return lax.dot_general(q, k.astype(BF16), (((1,), (1,)), ((), ())), preferred_element_type=F32)


def _with_ones(v):
    v = v.astype(BF16)
    return jnp.concatenate([v, jnp.ones_like(v)], axis=1)


def _store_head_major(dst_ref, x):
    rows = x.shape[0]
    for h in range(N_HEADS):
        dst_ref[pl.ds(h, rows, stride=N_HEADS), :] = x[:, _head_cols(h)]


def _store_layer_head_major(dst_ref, x, layer, whole):
    if not whole:
        _store_head_major(dst_ref, x)
        return
    for other in range(dst_ref.shape[0]):
        if other != layer:
            dst_ref[other] = jnp.zeros(dst_ref.shape[1:], dst_ref.dtype)
    _store_head_major(dst_ref.at[layer], x)


def _inproj_kernel(x_ref, g_ref, w_ref, *rest, slabs_per_step, direct, layer, whole):
    n_direct = len(direct)
    o_ref = rest[-n_direct - 2]
    direct_refs = rest[-n_direct - 1:-1]
    xn_ref = rest[-1]
    j = pl.program_id(1)

    @pl.when(j == 0)
    def _():
        x = x_ref[...]
        ms = jnp.mean(x * x, axis=-1, keepdims=True)
        xn_ref[...] = ((x * lax.rsqrt(ms + EPS)) * g_ref[...]).astype(BF16)

    res = jnp.dot(xn_ref[...], w_ref[...], preferred_element_type=F32)
    for k in range(slabs_per_step):
        o_ref[k] = res[:, k * GROUP_W:(k + 1) * GROUP_W]
    for slab, d_ref in zip(direct, direct_refs):
        @pl.when(j == slab // slabs_per_step)
        def _(slab=slab, d_ref=d_ref):
            k = slab % slabs_per_step
            _store_layer_head_major(d_ref, res[:, k * GROUP_W:(k + 1) * GROUP_W], layer, whole)


def _layer_rows_spec(rows, depth, layer, whole, grid_rank):
    if grid_rank == 1:
        if whole:
            return pl.BlockSpec((depth, rows, HEAD_DIM), lambda i: (0, i, 0))
        return pl.BlockSpec((None, rows, HEAD_DIM), lambda i: (layer, i, 0))
    if whole:
        return pl.BlockSpec((depth, rows, HEAD_DIM), lambda i, j: (0, i, 0))
    return pl.BlockSpec((None, rows, HEAD_DIM), lambda i, j: (layer, i, 0))


def in_proj(x2d, g, w_bf16, tm, slabs_per_step, direct=(), direct_bufs=None, layer=0, depth=1):
    m, d = x2d.shape
    n = w_bf16.shape[2]
    ns = n // GROUP_W
    sps = slabs_per_step
    n_direct = len(direct)
    alias_in = list(direct_bufs) if direct_bufs is not None else []
    whole = not alias_in
    direct_shape = jax.ShapeDtypeStruct((depth, m * N_HEADS, HEAD_DIM), F32)
    outs = pl.pallas_call(
        functools.partial(_inproj_kernel, slabs_per_step=sps, direct=tuple(direct), layer=layer,
                          whole=whole),
        grid=(m // tm, ns // sps),
        in_specs=[pl.BlockSpec((tm, d), lambda i, j: (i, 0)),
                  pl.BlockSpec((1, d), lambda i, j: (0, 0)),
                  pl.BlockSpec((None, d, sps * GROUP_W), lambda i, j: (layer, 0, j))]
                 + [pl.BlockSpec(memory_space=pl.ANY)] * len(alias_in),
        out_specs=[pl.BlockSpec((sps, tm, GROUP_W), lambda i, j: (j, i, 0))]
                  + [_layer_rows_spec(tm * N_HEADS, depth, layer, whole, 2)] * n_direct,
        out_shape=[jax.ShapeDtypeStruct((ns, m, GROUP_W), F32)] + [direct_shape] * n_direct,
        input_output_aliases={3 + n: 1 + n for n in range(len(alias_in))},
        scratch_shapes=[pltpu.VMEM((tm, d), BF16)],
        compiler_params=_cparams(("parallel", "arbitrary")),
        name="in_proj",
    )(x2d, g.reshape(1, d), w_bf16, *alias_in)
    return outs[0], tuple(outs[1:])


def _outproj_kernel(ya_ref, yb_ref, yc_ref, yd_ref, w_ref, x_ref, fg_ref, o_ref, *, final):
    acc = x_ref[...]
    for m, y_ref in enumerate((ya_ref, yb_ref, yc_ref, yd_ref)):
        acc = acc + jnp.dot(y_ref[...], w_ref[m * GROUP_W:(m + 1) * GROUP_W, :],
                            preferred_element_type=F32)
    if final:
        ms = jnp.mean(acc * acc, axis=-1, keepdims=True)
        acc = (acc * lax.rsqrt(ms + EPS)) * fg_ref[...]
    o_ref[...] = acc


def out_proj(ys, w_bf16, layer, x2d, final_g, final, tm):
    m, d = x2d.shape
    yspec = pl.BlockSpec((tm, GROUP_W), lambda i: (i, 0))
    return pl.pallas_call(
        functools.partial(_outproj_kernel, final=final),
        grid=(m // tm,),
        in_specs=[yspec, yspec, yspec, yspec,
                  pl.BlockSpec((None, N_MIXERS * GROUP_W, d), lambda i: (layer, 0, 0)),
                  pl.BlockSpec((tm, d), lambda i: (i, 0)),
                  pl.BlockSpec((1, d), lambda i: (0, 0))],
        out_specs=pl.BlockSpec((tm, d), lambda i: (i, 0)),
        out_shape=jax.ShapeDtypeStruct((m, d), F32),
        compiler_params=_cparams(("parallel",)),
        name="out_proj",
    )(*ys, w_bf16, x2d, final_g.reshape(1, d))


def _sb_terms(z):
    neg = -z
    t = jnp.log2(1.0 + jnp.exp2(jnp.minimum(z, neg)))
    return jnp.minimum(z, 0.0) - t, jnp.minimum(neg, 0.0) - t


def _suffix_sums(lk, u):
    hi = lk.astype(BF16)
    lo = (lk - hi.astype(F32)).astype(BF16)
    return (jnp.dot(hi, u, preferred_element_type=F32) + jnp.dot(lo, u, preferred_element_type=F32))


def _strict_lower(n):
    row = lax.broadcasted_iota(jnp.int32, (n, n), 0)
    col = lax.broadcasted_iota(jnp.int32, (n, n), 1)
    return row, col


def _sb_prompt_kernel(q_ref, k_ref, v_ref, g_ref, o_ref, *scratch, tq, tk, hps):
    i = pl.program_id(1)
    acc_refs, carry_refs = scratch[:hps], scratch[hps:]
    n_diag = tq // tk
    row, col = _strict_lower(tk)
    u = jnp.where(row > col, 1.0, 0.0).astype(BF16)
    qs = [q_ref[:, _head_cols(h)].astype(BF16) for h in range(hps)]

    def scores(h, jb):
        start = pl.multiple_of(jb * tk, tk)
        z = _qk(qs[h], k_ref[pl.ds(start, tk), _head_cols(h)]) * SB_SCALE_LOG2
        return z, v_ref[pl.ds(start, tk), _head_cols(h)].astype(BF16)

    def sweep(jb, visible, first):
        for h in range(hps):
            z, v = scores(h, jb)
            ls, lk = _sb_terms(z)
            if visible is not None:
                lk = jnp.where(visible, lk, 0.0)
            e = ls + _suffix_sums(lk, u)
            if not first:
                e = e + carry_refs[h][...]
            a = jnp.exp2(e)
            if visible is not None:
                a = jnp.where(visible, a, 0.0)
            pv = jnp.dot(a.astype(BF16), v, preferred_element_type=F32)
            dc = jnp.sum(lk, axis=1, keepdims=True)
            if first:
                acc_refs[h][...] = pv
                carry_refs[h][...] = dc
            else:
                acc_refs[h][...] += pv
                carry_refs[h][...] += dc

    qrow = lax.broadcasted_iota(jnp.int32, (tq, tk), 0)
    kcol = lax.broadcasted_iota(jnp.int32, (tq, tk), 1)
    for d in reversed(range(n_diag)):
        sweep(i * n_diag + d, kcol + d * tk < qrow, d == n_diag - 1)

    def live():
        return _tree(jnp.maximum, [jnp.max(c[...]) for c in carry_refs]) > SB_EXIT_LOG2

    def cond(state):
        jb, go = state
        return jnp.logical_and(jb >= 0, go)

    def body(state):
        jb, _ = state
        sweep(jb, None, False)
        return jb - 1, live()

    lax.while_loop(cond, body, (i * n_diag - 1, live()))
    for h in range(hps):
        o_ref[:, _head_cols(h)] = (acc_refs[h][...] * _silu(g_ref[:, _head_cols(h)])).astype(o_ref.dtype)


def sb_prompt(proj, tq, tk, heads_per_step):
    _, t, _ = proj.shape
    hps = heads_per_step
    width = hps * HEAD_DIM

    def spec(rows, s, per_q):
        if per_q:
            return pl.BlockSpec((None, rows, width), lambda h, i: (s, i, h))
        return pl.BlockSpec((None, rows, width), lambda h, i: (s, 0, h))

    return pl.pallas_call(
        functools.partial(_sb_prompt_kernel, tq=tq, tk=tk, hps=hps),
        grid=(N_HEADS // hps, t // tq),
        in_specs=[spec(tq, S_SB_Q, True), spec(t, S_SB_K, False), spec(t, S_SB_V, False),
                  spec(tq, S_SB_G, True)],
        out_specs=pl.BlockSpec((tq, width), lambda h, i: (i, h)),
        out_shape=jax.ShapeDtypeStruct((t, GROUP_W), BF16),
        scratch_shapes=[pltpu.VMEM((tq, HEAD_DIM), F32)] * hps + [pltpu.VMEM((tq, 1), F32)] * hps,
        compiler_params=_cparams(("parallel", "arbitrary")),
        name="sb_prompt",
    )(proj, proj, proj, proj)


def _page_masks(rows, rows_per_head, page_rows):
    row = lax.broadcasted_iota(jnp.int32, (rows, page_rows), 0)
    col = lax.broadcasted_iota(jnp.int32, (rows, page_rows), 1)
    own = (col & (N_HEADS - 1)) == (row >> _log2(rows_per_head))
    return own, col >> _log2(N_HEADS), row & (PAD_T - 1)


def _sb_decode_kernel(pt_ref, q_ref, kn_ref, vn_ref, g_ref, u_ref, ck_ref, cv_ref, o_ref,
                      kbuf, vbuf, newk, newv, sem, acc_ref, carry_ref, *, layer, n_pages, n_tok):
    b = pl.program_id(0)
    rows = N_HEADS * PAD_T
    page_rows = newk.shape[0]

    ring = kbuf.shape[1]
    par = b % 2
    n_prime = min(SB_LOOKAHEAD, n_pages)

    def page_copies(bb, v, pr):
        pg = pt_ref[bb, n_pages - 1 - v]
        sl = v % ring
        return (pltpu.make_async_copy(ck_ref.at[layer, pg], kbuf.at[pr, sl], sem.at[0, pr, sl]),
                pltpu.make_async_copy(cv_ref.at[layer, pg], vbuf.at[pr, sl], sem.at[1, pr, sl]))

    def start_fetch(bb, v, pr):
        for c in page_copies(bb, v, pr):
            c.start()

    def wait_fetch(v):
        for c in page_copies(b, v, par):
            c.wait()

    @pl.when(b == 0)
    def _():
        for v in range(n_prime):
            start_fetch(b, v, par)

    @pl.when(b + 1 < pl.num_programs(0))
    def _():
        for v in range(n_prime):
            start_fetch(b + 1, v, 1 - par)

    qall = jnp.concatenate([q_ref[h] for h in range(N_HEADS)], axis=0).astype(BF16)
    own, key, tok = _page_masks(rows, PAD_T, page_rows)
    real_row = (lax.broadcasted_iota(jnp.int32, (rows, 1), 0) & (PAD_T - 1)) < n_tok
    u = u_ref[...]

    def attend(k, v, visible, carry):
        ls, lk = _sb_terms(_qk(qall, k) * SB_SCALE_LOG2)
        lk = jnp.where(visible, lk, 0.0)
        e = ls + _suffix_sums(lk, u)
        if carry is not None:
            e = e + carry
        a = jnp.where(visible, jnp.exp2(e), 0.0).astype(BF16)
        return (jnp.dot(a, v.astype(BF16), preferred_element_type=F32),
                jnp.sum(lk, axis=1, keepdims=True))

    def live(carry):
        return jnp.max(jnp.where(real_row, carry, -jnp.inf)) > SB_EXIT_LOG2

    newk[...] = jnp.zeros_like(newk)
    newv[...] = jnp.zeros_like(newv)
    newk[0:kn_ref.shape[0], :] = kn_ref[...]
    newv[0:vn_ref.shape[0], :] = vn_ref[...]
    pv, dc = attend(newk[...], newv[...], jnp.logical_and(own, key < tok), None)
    acc_ref[...] = pv
    carry_ref[...] = dc

    def cond(state):
        v, go = state
        return jnp.logical_and(v < n_pages, go)

    def body(state):
        v, _ = state
        wait_fetch(v)

        @pl.when(v + SB_LOOKAHEAD < n_pages)
        def _():
            start_fetch(b, v + SB_LOOKAHEAD, par)

        slot = v % ring
        carry = carry_ref[...]
        pv, dc = attend(kbuf[par, slot], vbuf[par, slot], own, carry)
        acc_ref[...] += pv
        carry = carry + dc
        carry_ref[...] = carry
        return v + 1, live(carry)

    v_end, _ = lax.while_loop(cond, body, (jnp.int32(0), live(carry_ref[...])))

    for k in range(SB_LOOKAHEAD):
        @pl.when(v_end + k < n_pages)
        def _(k=k):
            wait_fetch(v_end + k)

    acc = acc_ref[...]
    o_ref[...] = jnp.concatenate([acc[h * PAD_T:(h + 1) * PAD_T] * _silu(g_ref[h])
                                  for h in range(N_HEADS)], axis=1)


def sb_decode(q8, kn, vn, g8, cache_k, cache_v, page_table, layer, n_tok):
    bsz = page_table.shape[0]
    n_pages = page_table.shape[1]
    page_rows = cache_k.shape[2]
    tok_spec = pl.BlockSpec((None, PAD_T, GROUP_W), lambda b, pt: (b, 0, 0))
    head_spec = pl.BlockSpec((N_HEADS, PAD_T, HEAD_DIM), lambda b, pt: (0, b, 0))
    new_spec = pl.BlockSpec((None, kn.shape[1], HEAD_DIM), lambda b, pt: (b, 0, 0))
    any_spec = pl.BlockSpec(memory_space=pl.ANY)
    rows = N_HEADS * PAD_T
    row, col = _strict_lower(page_rows)
    u = jnp.where(row > col, 1.0, 0.0).astype(BF16)
    page_buf = pltpu.VMEM((2, SB_LOOKAHEAD + 1, page_rows, HEAD_DIM), F32)
    new_buf = pltpu.VMEM((page_rows, HEAD_DIM), F32)
    return pl.pallas_call(
        functools.partial(_sb_decode_kernel, layer=layer, n_pages=n_pages, n_tok=n_tok),
        grid_spec=pltpu.PrefetchScalarGridSpec(
            num_scalar_prefetch=1,
            grid=(bsz,),
            in_specs=[head_spec, new_spec, new_spec, head_spec,
                      pl.BlockSpec((page_rows, page_rows), lambda b, pt: (0, 0)), any_spec, any_spec],
            out_specs=tok_spec,
            scratch_shapes=[page_buf, page_buf, new_buf, new_buf,
                            pltpu.SemaphoreType.DMA((2, 2, SB_LOOKAHEAD + 1)),
                            pltpu.VMEM((rows, HEAD_DIM), F32), pltpu.VMEM((rows, 1), F32)]),
        out_shape=jax.ShapeDtypeStruct((bsz, PAD_T, GROUP_W), F32),
        compiler_params=_cparams(("arbitrary",)),
        name="sb_decode",
    )(page_table, q8, kn, vn, g8, u, cache_k, cache_v)


def _lambda(lq1_ref, lk1_ref, lq2_ref, lk2_ref, lam_init):
    s1 = jnp.sum(lq1_ref[...] * lk1_ref[...], axis=-1, keepdims=True)
    s2 = jnp.sum(lq2_ref[...] * lk2_ref[...], axis=-1, keepdims=True)
    return jnp.exp(s1) - jnp.exp(s2) + lam_init


def _diff_finish(o1, o2, lam, sg, gate, lam_init):
    w = o1 - lam * o2
    ms = jnp.mean(w * w, axis=-1, keepdims=True)
    y = (w * lax.rsqrt(ms + EPS)) * sg
    return (y * (1.0 - lam_init)) * _silu(gate)


def _split_components(q):
    lane = lax.broadcasted_iota(jnp.int32, q.shape, 1)
    zero = jnp.zeros_like(q)
    return jnp.concatenate([jnp.where(lane < DIFF_QK, q, zero), jnp.where(lane >= DIFF_QK, q, zero)],
                           axis=0)


def _diff_prompt_kernel(q_ref, k_ref, v_ref, g_ref, lq1_ref, lk1_ref, lq2_ref, lk2_ref, sg_ref,
                        o_ref, s0_ref, s1_ref, mt0_ref, mt1_ref, m_ref, acc_ref, *, tq, tk, lam_init):
    i = pl.program_id(1)
    qq = _split_components(q_ref[...])
    m_ref[...] = jnp.full_like(m_ref, -jnp.inf)
    acc_ref[...] = jnp.zeros_like(acc_ref)
    bufs = ((s0_ref, mt0_ref), (s1_ref, mt1_ref))

    def scores(jt, diag=None):
        s = _qk(qq, k_ref[pl.ds(pl.multiple_of(jt * tk, tk), tk), :])
        if diag is not None:
            row = lax.broadcasted_iota(jnp.int32, s.shape, 0) & (tq - 1)
            col = lax.broadcasted_iota(jnp.int32, s.shape, 1) + diag * tk
            s = jnp.where(col <= row, s, -jnp.inf)
        return s

    def score_stage(slot, s):
        s_ref, mt_ref = bufs[slot]
        s_ref[...] = s
        mt_ref[...] = jnp.broadcast_to(jnp.max(s, axis=1, keepdims=True), mt_ref.shape)

    def value_stage(jt, slot):
        s_ref, mt_ref = bufs[slot]
        m_old = m_ref[...]
        m_new = jnp.maximum(m_old, mt_ref[...])
        alpha = jnp.exp2(m_old - m_new)
        p = jnp.concatenate([jnp.exp2(c - m_new) for c in _lane_chunks(s_ref[...])], axis=1)
        v = v_ref[pl.ds(pl.multiple_of(jt * tk, tk), tk), :]
        pv = jnp.dot(p.astype(BF16), _with_ones(v), preferred_element_type=F32)
        acc_ref[...] = jnp.concatenate([alpha, alpha], axis=1) * acc_ref[...] + pv
        m_ref[...] = m_new

    d0 = 2 * i

    @pl.when(i == 0)
    def _():
        score_stage(0, scores(0, 0))
        score_stage(1, scores(1, 1))
        value_stage(0, 0)
        value_stage(1, 1)

    @pl.when(i > 0)
    def _():
        score_stage(0, scores(0))

        def body(g, c):
            score_stage(1, scores(2 * g + 1))
            value_stage(2 * g, 0)
            score_stage(0, scores(2 * g + 2))
            value_stage(2 * g + 1, 1)
            return c

        lax.fori_loop(0, i - 1, body, 0)
        score_stage(1, scores(d0 - 1))
        value_stage(d0 - 2, 0)
        score_stage(0, scores(d0, 0))
        value_stage(d0 - 1, 1)
        score_stage(1, scores(d0 + 1, 1))
        value_stage(d0, 0)
        value_stage(d0 + 1, 1)

    acc = acc_ref[...]
    o = acc[:, :HEAD_DIM] / acc[:, HEAD_DIM:]
    lam = _lambda(lq1_ref, lk1_ref, lq2_ref, lk2_ref, lam_init)
    o_ref[...] = _diff_finish(o[:tq], o[tq:], lam, sg_ref[...], g_ref[...], lam_init).astype(o_ref.dtype)


def diff_prompt(qb, kb, vb, proj, lam_params, subln_g, lam_init, tk):
    t = qb.shape[0]
    tq = 2 * tk
    small = pl.BlockSpec((1, DIFF_QK), lambda h, i: (0, 0))
    return pl.pallas_call(
        functools.partial(_diff_prompt_kernel, tq=tq, tk=tk, lam_init=lam_init),
        grid=(N_HEADS, t // tq),
        in_specs=[pl.BlockSpec((tq, HEAD_DIM), lambda h, i: (i, h)),
                  pl.BlockSpec((t, HEAD_DIM), lambda h, i: (0, h)),
                  pl.BlockSpec((t, HEAD_DIM), lambda h, i: (0, h)),
                  pl.BlockSpec((None, tq, HEAD_DIM), lambda h, i: (S_DF_G, i, h)),
                  small, small, small, small,
                  pl.BlockSpec((1, HEAD_DIM), lambda h, i: (0, 0))],
        out_specs=pl.BlockSpec((tq, HEAD_DIM), lambda h, i: (i, h)),
        out_shape=jax.ShapeDtypeStruct((t, GROUP_W), BF16),
        scratch_shapes=[pltpu.VMEM((2 * tq, tk), F32), pltpu.VMEM((2 * tq, tk), F32),
                        pltpu.VMEM((2 * tq, HEAD_DIM), F32), pltpu.VMEM((2 * tq, HEAD_DIM), F32),
                        pltpu.VMEM((2 * tq, HEAD_DIM), F32), pltpu.VMEM((2 * tq, 2 * HEAD_DIM), F32)],
        compiler_params=_cparams(("parallel", "arbitrary")),
        name="diff_prompt",
    )(qb, kb, vb, proj, *[p.reshape(1, DIFF_QK) for p in lam_params], subln_g.reshape(1, HEAD_DIM))


def _diff_decode_kernel(pt_ref, q_ref, kn_ref, vn_ref, g_ref, lq1_ref, lk1_ref, lq2_ref, lk2_ref,
                        sg_ref, ck_ref, cv_ref, o_ref, kbuf, vbuf, sem, m_ref, acc_ref, newk, newv, *,
                        layer, pages_per_step, n_chunks, group, lam_init):
    b = pl.program_id(0)
    c = pl.program_id(1)
    hrows = 2 * PAD_T
    rows = N_HEADS * hrows
    page_rows = newk.shape[0]

    n_slots = kbuf.shape[0]
    n_steps = pl.num_programs(0) * n_chunks
    step = b * n_chunks + c
    slot = step % n_slots

    def step_copies(t):
        bb, cc, sl = t // n_chunks, t % n_chunks, t % n_slots
        copies = []
        for n in range(pages_per_step):
            pg = pt_ref[bb, cc * pages_per_step + n]
            copies.append(pltpu.make_async_copy(ck_ref.at[layer, pg], kbuf.at[sl, n], sem.at[sl, 0, n]))
            copies.append(pltpu.make_async_copy(cv_ref.at[layer, pg], vbuf.at[sl, n], sem.at[sl, 1, n]))
        return copies

    def start_step(t):
        for n, cp in enumerate(step_copies(t)):
            cp.start(priority=n % 2)

    @pl.when(step == 0)
    def _():
        for t in range(DECODE_PREFETCH_STEPS):
            @pl.when(t < n_steps)
            def _(t=t):
                start_step(t)

    @pl.when(step + DECODE_PREFETCH_STEPS < n_steps)
    def _():
        start_step(step + DECODE_PREFETCH_STEPS)

    for cp in step_copies(step):
        cp.wait()

    qall = jnp.concatenate([_split_components(q_ref[h] * (DIFF_SCALE * LOG2E))
                            for h in range(N_HEADS)], axis=0).astype(BF16)
    zq = jnp.zeros_like(qall)
    qpair = jnp.concatenate([jnp.concatenate([qall, zq], axis=1),
                             jnp.concatenate([zq, qall], axis=1)], axis=0)
    own, key, tok = _page_masks(rows, hrows, page_rows)

    @pl.when(c == 0)
    def _():
        m_ref[...] = jnp.full_like(m_ref, -jnp.inf)
        acc_ref[...] = jnp.zeros_like(acc_ref)

    def page_scores(ks):
        out = []
        for n in range(0, len(ks) - 1, 2):
            kk = jnp.concatenate([ks[n].astype(BF16), ks[n + 1].astype(BF16)], axis=1)
            s2 = _qk(qpair, kk)
            out += [s2[:rows], s2[rows:]]
        if len(ks) % 2:
            out.append(_qk(qall, ks[-1]))
        return out

    def update(ks, vs, visible):
        s = jnp.concatenate([jnp.where(visible, z, -jnp.inf) for z in page_scores(ks)], axis=1)
        m_prev = m_ref[...]
        m_new = jnp.maximum(m_prev, jnp.max(s, axis=1, keepdims=True))
        p = jnp.exp2(s - m_new).astype(BF16)
        pv = _tree(jnp.add, [jnp.dot(p[:, n * page_rows:(n + 1) * page_rows], _with_ones(v),
                                     preferred_element_type=F32) for n, v in enumerate(vs)])
        return m_new, jnp.exp2(m_prev - m_new) * acc_ref[...] + pv

    for g0 in range(0, pages_per_step, group):
        m_new, acc_new = update([kbuf[slot, n] for n in range(g0, g0 + group)],
                                [vbuf[slot, n] for n in range(g0, g0 + group)], own)
        m_ref[...] = m_new
        acc_ref[...] = acc_new

    @pl.when(c == pl.num_programs(1) - 1)
    def _():
        newk[...] = jnp.zeros_like(newk)
        newv[...] = jnp.zeros_like(newv)
        newk[0:kn_ref.shape[0], :] = kn_ref[...]
        newv[0:vn_ref.shape[0], :] = vn_ref[...]
        _, acc = update([newk[...]], [newv[...]], jnp.logical_and(own, key <= tok))
        o = acc[:, :HEAD_DIM] / acc[:, HEAD_DIM:]
        lam = _lambda(lq1_ref, lk1_ref, lq2_ref, lk2_ref, lam_init)
        sg = sg_ref[...]
        outs = []
        for h in range(N_HEADS):
            o1 = o[h * hrows:h * hrows + PAD_T]
            o2 = o[h * hrows + PAD_T:(h + 1) * hrows]
            outs.append(_diff_finish(o1, o2, lam, sg, g_ref[h], lam_init))
        o_ref[...] = jnp.concatenate(outs, axis=1)


def diff_decode(q8, kn, vn, g8, cache_k, cache_v, page_table, lam_params, subln_g, lam_init,
                layer, pages_per_step, group):
    bsz = page_table.shape[0]
    n_pages = page_table.shape[1]
    page_rows = cache_k.shape[2]
    pps = pages_per_step
    tok_spec = pl.BlockSpec((None, PAD_T, GROUP_W), lambda b, c, pt: (b, 0, 0))
    head_spec = pl.BlockSpec((N_HEADS, PAD_T, HEAD_DIM), lambda b, c, pt: (0, b, 0))
    new_spec = pl.BlockSpec((None, kn.shape[1], HEAD_DIM), lambda b, c, pt: (b, 0, 0))
    small = pl.BlockSpec((1, DIFF_QK), lambda b, c, pt: (0, 0))

    rows = N_HEADS * 2 * PAD_T
    any_spec = pl.BlockSpec(memory_space=pl.ANY)
    n_slots = DECODE_PREFETCH_STEPS + 1
    page_bufs = pltpu.VMEM((n_slots, pps, page_rows, HEAD_DIM), F32)
    new_buf = pltpu.VMEM((page_rows, HEAD_DIM), F32)
    return pl.pallas_call(
        functools.partial(_diff_decode_kernel, layer=layer, pages_per_step=pps,
                          n_chunks=n_pages // pps, group=group, lam_init=lam_init),
        grid_spec=pltpu.PrefetchScalarGridSpec(
            num_scalar_prefetch=1,
            grid=(bsz, n_pages // pps),
            in_specs=[head_spec, new_spec, new_spec, head_spec, small, small, small, small,
                      pl.BlockSpec((1, HEAD_DIM), lambda b, c, pt: (0, 0)), any_spec, any_spec],
            out_specs=tok_spec,
            scratch_shapes=[page_bufs, page_bufs, pltpu.SemaphoreType.DMA((n_slots, 2, pps)),
                            pltpu.VMEM((rows, 1), F32), pltpu.VMEM((rows, 2 * HEAD_DIM), F32),
                            new_buf, new_buf]),
        out_shape=jax.ShapeDtypeStruct((bsz, PAD_T, GROUP_W), F32),
        compiler_params=_cparams(("arbitrary", "arbitrary")),
        name="diff_decode",
    )(page_table, q8, kn, vn, g8, *[p.reshape(1, DIFF_QK) for p in lam_params],
      subln_g.reshape(1, HEAD_DIM), cache_k, cache_v)


CONF_HALO = 32
SC_HALO = 8
SUBLANES = 8
MIX_ROWS = 32


def _rope_tables(pos):
    half = ROT_DIM // 2
    inv = ROPE_THETA ** (-2.0 * jnp.arange(half, dtype=F32) / ROT_DIM)
    ang = pos.astype(F32)[:, None] * inv[None, :]
    cos, sin = jnp.cos(ang), jnp.sin(ang)
    j = jnp.arange(HEAD_DIM) % DIFF_QK
    f = j % half
    c = jnp.where(j < ROT_DIM, cos[:, f], 1.0)
    sa = jnp.where((j >= half) & (j < ROT_DIM), sin[:, f], 0.0)
    sb = jnp.where(j < half, -sin[:, f], 0.0)
    return c, sa, sb


def _rope_rows(x, c, sa, sb):
    half = ROT_DIM // 2
    out = []
    for xh in _lane_chunks(x):
        out.append(xh * c + pltpu.roll(xh, half, 1) * sa + pltpu.roll(xh, HEAD_DIM - half, 1) * sb)
    return jnp.concatenate(out, axis=1)


def _prompt_mix_kernel(a_ref, b_ref, g_ref, dq_ref, dk_ref, dv_ref, sh_ref, sb_ref, sc_ref, sg_ref,
                       cw_ref, cb_ref, lg_ref, lb_ref, sw_ref, rc_ref, rsa_ref, rsb_ref, *rest,
                       tb, conf_w, sc_w, layer, whole):
    ya_ref, yd_ref, dqb_ref, dk_out_ref, dkb_ref, dvb_ref, nconf_ref, nsc_ref, hb, hs, ub = rest[-11:]
    i = pl.program_id(0)

    @pl.when(i == 0)
    def _():
        hb[0:CONF_HALO, :] = jnp.zeros((CONF_HALO, GROUP_W), F32)
        ub[0:SC_HALO, :] = jnp.zeros((SC_HALO, GROUP_W), F32)

    @pl.when(i > 0)
    def _():
        hb[0:CONF_HALO, :] = hb[tb:tb + CONF_HALO, :]
        ub[0:SC_HALO, :] = ub[tb:tb + SC_HALO, :]

    hb[CONF_HALO:CONF_HALO + tb, :] = a_ref[...] * (1.0 / (1.0 + jnp.exp(-b_ref[...])))
    ub[SC_HALO:SC_HALO + tb, :] = sc_ref[...] * sh_ref[...]

    off = CONF_HALO - (conf_w - 1)
    span = hs.shape[1]
    for k in range(1, SUBLANES):
        hs[k - 1, :, :] = hb[k:k + span, :]

    def tap(w_ref, w):
        return jnp.concatenate([w_ref[w]] * (MIX_ROWS // SUBLANES), axis=0)

    def window(r):
        k, base = r % SUBLANES, r - r % SUBLANES
        if k == 0:
            return hb[base:base + MIX_ROWS, :]
        return hs[k - 1, base:base + MIX_ROWS, :]

    for r0 in range(0, tb, MIX_ROWS):
        rows = slice(r0, r0 + MIX_ROWS)
        acc = jnp.zeros((MIX_ROWS, GROUP_W), F32) + cb_ref[...]
        for w in range(conf_w):
            acc = acc + tap(cw_ref, w) * window(r0 + off + w)
        mu = jnp.mean(acc, axis=-1, keepdims=True)
        xc = acc - mu
        y = (xc * lax.rsqrt(jnp.mean(xc * xc, axis=-1, keepdims=True) + EPS)) * lg_ref[...] + lb_ref[...]
        ya_ref[rows, :] = (_silu(y) * _silu(g_ref[rows, :])).astype(ya_ref.dtype)
        u0 = r0 + SC_HALO - (sc_w - 1)
        conv = jnp.zeros((MIX_ROWS, GROUP_W), F32)
        for w in range(sc_w):
            conv = conv + tap(sw_ref, w) * ub[u0 + w:u0 + w + MIX_ROWS, :]
        yd_ref[rows, :] = ((sb_ref[rows, :] * conv) * _silu(sg_ref[rows, :])).astype(yd_ref.dtype)

    c, sa, sb = rc_ref[...], rsa_ref[...], rsb_ref[...]
    dqb_ref[...] = (_rope_rows(dq_ref[...], c, sa, sb) * (DIFF_SCALE * LOG2E)).astype(dqb_ref.dtype)
    dk = _rope_rows(dk_ref[...], c, sa, sb)
    _store_layer_head_major(dk_out_ref, dk, layer, whole)
    dkb_ref[...] = dk.astype(dkb_ref.dtype)
    dvb_ref[...] = dv_ref[...].astype(dvb_ref.dtype)

    @pl.when(i == pl.num_programs(0) - 1)
    def _():
        nconf_ref[...] = hb[CONF_HALO + tb - (conf_w - 1):CONF_HALO + tb, :]
        nsc_ref[...] = ub[SC_HALO + tb - (sc_w - 1):SC_HALO + tb, :]


def prompt_mix(proj, conf_w, conf_b, ln_g, ln_b, sc_w, rope_tabs, tb, k_buf=None, layer=0, depth=1):
    _, t, _ = proj.shape
    cw, scw = conf_w.shape[0], sc_w.shape[0]

    def slab(s):
        return pl.BlockSpec((None, tb, GROUP_W), lambda i: (s, i, 0))

    def full(r):
        return pl.BlockSpec((r, GROUP_W), lambda i: (0, 0))

    def taps(n):
        return pl.BlockSpec((n, SUBLANES, GROUP_W), lambda i: (0, 0, 0))

    def replicate(w):
        return jnp.broadcast_to(w[:, None, :], (w.shape[0], SUBLANES, w.shape[1]))

    row_blk = pl.BlockSpec((tb, GROUP_W), lambda i: (i, 0))
    tab_blk = pl.BlockSpec((tb, HEAD_DIM), lambda i: (i, 0))
    slabs = (S_CONF_A, S_CONF_B, S_CONF_G, S_DF_Q, S_DF_K, S_DF_V, S_SC_H, S_SC_B, S_SC_C, S_SC_G)
    act = jax.ShapeDtypeStruct((t, GROUP_W), BF16)
    alias_in = [] if k_buf is None else [k_buf]
    n_in = len(slabs) + 8
    return pl.pallas_call(
        functools.partial(_prompt_mix_kernel, tb=tb, conf_w=cw, sc_w=scw, layer=layer,
                          whole=not alias_in),
        grid=(t // tb,),
        in_specs=[slab(s) for s in slabs]
                 + [taps(cw), full(1), full(1), full(1), taps(scw), tab_blk, tab_blk, tab_blk]
                 + [pl.BlockSpec(memory_space=pl.ANY)] * len(alias_in),
        out_specs=[row_blk, row_blk, row_blk,
                   _layer_rows_spec(tb * N_HEADS, depth, layer, not alias_in, 1),
                   row_blk, row_blk, full(cw - 1), full(scw - 1)],
        out_shape=[act, act, act, jax.ShapeDtypeStruct((depth, t * N_HEADS, HEAD_DIM), F32), act, act,
                   jax.ShapeDtypeStruct((cw - 1, GROUP_W), F32),
                   jax.ShapeDtypeStruct((scw - 1, GROUP_W), F32)],
        scratch_shapes=[pltpu.VMEM((CONF_HALO + tb, GROUP_W), F32),
                        pltpu.VMEM((SUBLANES - 1, CONF_HALO + tb - SUBLANES, GROUP_W), F32),
                        pltpu.VMEM((SC_HALO + tb, GROUP_W), F32)],
        input_output_aliases={n_in: 3} if alias_in else {},
        compiler_params=_cparams(("arbitrary",)),
        name="prompt_mix",
    )(*([proj] * len(slabs)), replicate(conf_w), conf_b.reshape(1, -1), ln_g.reshape(1, -1),
      ln_b.reshape(1, -1), replicate(sc_w), *rope_tabs, *alias_in)


def _decode_mix_kernel(proj_ref, sconf_ref, ssc_ref, cw_ref, cb_ref, lg_ref, lb_ref, sw_ref,
                       rc_ref, rsa_ref, rsb_ref,
                       ya_ref, yd_ref, sbq_ref, sbg_ref, dfq_ref, dfg_ref, sbk_ref, sbv_ref, dfk_ref,
                       dfv_ref, nconf_ref, nsc_ref, *, bsz, n_tok, conf_w, sc_w):
    def tok(slab, t):
        return proj_ref[slab, t * bsz:(t + 1) * bsz, :]

    def put_tok(ref, t, x):
        ref[t * bsz:(t + 1) * bsz, :] = x.astype(ref.dtype)

    def put_pad(ref, t, x):
        for h in range(N_HEADS):
            ref[h, pl.ds(t, bsz, stride=PAD_T), :] = x[:, _head_cols(h)]

    def put_heads(ref, t, x):
        for h in range(N_HEADS):
            ref[pl.ds(t * N_HEADS + h, bsz, stride=n_tok * N_HEADS), :] = x[:, _head_cols(h)]

    def tap(w_ref, w):
        if bsz % SUBLANES:
            return jnp.broadcast_to(w_ref[w, 0:1, :], (bsz, GROUP_W))
        return jnp.concatenate([w_ref[w]] * (bsz // SUBLANES), axis=0)

    h = [tok(S_CONF_A, t) * (1.0 / (1.0 + jnp.exp(-tok(S_CONF_B, t)))) for t in range(n_tok)]

    def hist(r):
        return sconf_ref[r] if r < conf_w - 1 else h[r - (conf_w - 1)]

    for t in range(n_tok):
        acc = jnp.zeros((bsz, GROUP_W), F32) + cb_ref[...]
        for w in range(conf_w):
            acc = acc + tap(cw_ref, w) * hist(t + w)
        mu = jnp.mean(acc, axis=-1, keepdims=True)
        xc = acc - mu
        y = (xc * lax.rsqrt(jnp.mean(xc * xc, axis=-1, keepdims=True) + EPS)) * lg_ref[...] + lb_ref[...]
        put_tok(ya_ref, t, _silu(y) * _silu(tok(S_CONF_G, t)))
    for r in range(conf_w - 1):
        nconf_ref[r] = hist(r + n_tok)

    u = [tok(S_SC_C, t) * tok(S_SC_H, t) for t in range(n_tok)]

    def uhist(r):
        return ssc_ref[r] if r < sc_w - 1 else u[r - (sc_w - 1)]

    for t in range(n_tok):
        conv = jnp.zeros((bsz, GROUP_W), F32)
        for w in range(sc_w):
            conv = conv + tap(sw_ref, w) * uhist(t + w)
        put_tok(yd_ref, t, (tok(S_SC_B, t) * conv) * _silu(tok(S_SC_G, t)))
    for r in range(sc_w - 1):
        nsc_ref[r] = uhist(r + n_tok)

    for ref in (sbq_ref, sbg_ref, dfq_ref, dfg_ref):
        ref[...] = jnp.zeros_like(ref)
    for t in range(n_tok):
        tabs = [jnp.broadcast_to(r[t:t + 1, :], (bsz, HEAD_DIM)) for r in (rc_ref, rsa_ref, rsb_ref)]
        put_pad(sbq_ref, t, tok(S_SB_Q, t))
        put_pad(sbg_ref, t, tok(S_SB_G, t))
        put_pad(dfq_ref, t, _rope_rows(tok(S_DF_Q, t), *tabs))
        put_pad(dfg_ref, t, tok(S_DF_G, t))
        put_heads(sbk_ref, t, tok(S_SB_K, t))
        put_heads(sbv_ref, t, tok(S_SB_V, t))
        put_heads(dfk_ref, t, _rope_rows(tok(S_DF_K, t), *tabs))
        put_heads(dfv_ref, t, tok(S_DF_V, t))


def decode_mix(proj, state_conf, state_sc, conf_w, conf_b, ln_g, ln_b, sc_w, rope_tabs, bsz, n_tok):
    cw, scw = conf_w.shape[0], sc_w.shape[0]

    def replicate(w):
        return jnp.broadcast_to(w[:, None, :], (w.shape[0], SUBLANES, w.shape[1]))

    act = jax.ShapeDtypeStruct((bsz * n_tok, GROUP_W), BF16)
    pad = jax.ShapeDtypeStruct((N_HEADS, bsz * PAD_T, HEAD_DIM), F32)
    hm = jax.ShapeDtypeStruct((bsz * n_tok * N_HEADS, HEAD_DIM), F32)
    return pl.pallas_call(
        functools.partial(_decode_mix_kernel, bsz=bsz, n_tok=n_tok, conf_w=cw, sc_w=scw),
        out_shape=[act, act, pad, pad, pad, pad, hm, hm, hm, hm,
                   jax.ShapeDtypeStruct(state_conf.shape, F32), jax.ShapeDtypeStruct(state_sc.shape, F32)],
        compiler_params=pltpu.CompilerParams(vmem_limit_bytes=VMEM_LIMIT),
        name="decode_mix",
    )(proj, state_conf, state_sc, replicate(conf_w), conf_b.reshape(1, -1), ln_g.reshape(1, -1),
      ln_b.reshape(1, -1), replicate(sc_w), *rope_tabs)


def _pick_tile(n, pref):
    t = min(n, pref)
    while n % t:
        t //= 2
    return t


def kernel(x_prompt, x_sample, cache_sb_k, cache_sb_v, cache_diff_k, cache_diff_v, state_conf_conv,
           state_short_conv, page_table, norm_g, w_in, conf_dw_w, conf_dw_b, conf_ln_g, conf_ln_b,
           diff_lq1, diff_lk1, diff_lq2, diff_lk2, diff_subln_g, sc_conv_w, w_out, final_norm_g):
    bp, tp, d = x_prompt.shape
    bs, ts, _ = x_sample.shape
    depth = w_in.shape[0]
    n_pages = page_table.shape[1]
    page = cache_sb_k.shape[2]
    n_past = n_pages * page
    pos_p = jnp.arange(tp, dtype=jnp.int32)
    pos_s = n_past + jnp.arange(ts, dtype=jnp.int32)
    assert bp == 1 and ts <= PAD_T

    caches = [c.reshape(c.shape[0], c.shape[1], page * N_HEADS, HEAD_DIM)
              for c in (cache_sb_k, cache_sb_v, cache_diff_k, cache_diff_v)]
    w_in_b = w_in.astype(BF16)
    w_out_b = w_out.astype(BF16)

    hp = x_prompt.reshape(bp * tp, d)
    hs = jnp.swapaxes(x_sample, 0, 1).reshape(ts * bs, d)
    conf_states = jnp.swapaxes(state_conf_conv, 1, 2)
    sc_states = jnp.swapaxes(state_short_conv, 1, 2)
    tm_p = _pick_tile(bp * tp, 512)
    slabs_per_step = 5
    assert N_IN_SLICES % slabs_per_step == 0
    tm_s = _pick_tile(bs * ts, 128)
    tq_sb = _pick_tile(tp, 512)
    tk_sb = _pick_tile(tq_sb, 256)
    tk_df = _pick_tile(tp // 2, 512)
    tb_mix = _pick_tile(tp, 256)
    assert tb_mix >= CONF_HALO >= conf_dw_w.shape[1] - 1 and SC_HALO >= sc_conv_w.shape[1] - 1
    rope_tabs_p = _rope_tables(pos_p)
    rope_tabs_s = _rope_tables(pos_s)
    pps = _pick_tile(n_pages, 16)
    pgroup = _pick_tile(pps, 2)

    p_conv = [[], []]
    s_new = [[] for _ in range(6)]
    kv_bufs = None
    dk_buf = None
    for l in range(depth):
        lam_init = 0.8 - 0.6 * math.exp(-0.3 * l)
        lam_params = (diff_lq1[l], diff_lk1[l], diff_lq2[l], diff_lk2[l])
        final = l == depth - 1

        proj, kv_bufs = in_proj(hp, norm_g[l], w_in_b, tm_p, slabs_per_step,
                                direct=(S_SB_K, S_SB_V, S_DF_V), direct_bufs=kv_bufs, layer=l,
                                depth=depth)
        ya, yd, dqb, dk_buf, dkb, dvb, new_conf, new_sc = prompt_mix(
            proj, conf_dw_w[l], conf_dw_b[l], conf_ln_g[l], conf_ln_b[l], sc_conv_w[l], rope_tabs_p,
            tb_mix, k_buf=dk_buf, layer=l, depth=depth)
        yb = sb_prompt(proj, tq_sb, tk_sb, 2)
        yc = diff_prompt(dqb, dkb, dvb, proj, lam_params, diff_subln_g[l], lam_init, tk_df)
        hp = out_proj((ya, yb, yc, yd), w_out_b, l, hp, final_norm_g, final,
                      _pick_tile(bp * tp, 512))
        p_conv[0].append(new_conf[None])
        p_conv[1].append(new_sc[None])
        heads = lambda a, bb, tt: a.reshape(bb, tt, N_HEADS, HEAD_DIM)

        proj, _ = in_proj(hs, norm_g[l], w_in_b, tm_s, slabs_per_step, layer=l)
        ya, yd, sbq, sbg, dfq, dfg, sbk, sbv, dfk, dfv, new_conf, new_sc = decode_mix(
            proj, conf_states[l], sc_states[l], conf_dw_w[l], conf_dw_b[l], conf_ln_g[l],
            conf_ln_b[l], sc_conv_w[l], rope_tabs_s, bs, ts)
        new_rows = lambda a: a.reshape(bs, ts * N_HEADS, HEAD_DIM)
        yb = sb_decode(sbq, new_rows(sbk), new_rows(sbv), sbg, caches[0], caches[1], page_table, l, ts)
        yc = diff_decode(dfq, new_rows(dfk), new_rows(dfv), dfg, caches[2], caches[3], page_table,
                         lam_params, diff_subln_g[l], lam_init, l, pps, pgroup)
        flat = lambda a: jnp.swapaxes(a[:, :ts], 0, 1).reshape(ts * bs, GROUP_W).astype(BF16)
        hs = out_proj((ya, flat(yb), flat(yc), yd), w_out_b, l, hs, final_norm_g, final, tm_s)
        for n, a in enumerate((sbk, sbv, dfk, dfv)):
            s_new[n].append(a.reshape(bs, ts, N_HEADS, HEAD_DIM))
        s_new[4].append(new_conf)
        s_new[5].append(new_sc)

    y_prompt = hp.reshape(bp, tp, d)
    y_sample = jnp.swapaxes(hs.reshape(ts, bs, d), 0, 1)
    p_kv = [a.reshape(depth, bp, tp, N_HEADS, HEAD_DIM)
            for a in (kv_bufs[0], kv_bufs[1], dk_buf, kv_bufs[2])]
    s_out = [jnp.stack(a, 0) for a in s_new]
    s_out[4:] = [jnp.swapaxes(a, 1, 2) for a in s_out[4:]]
    return (y_prompt, y_sample, *p_kv, *[jnp.stack(a, 0) for a in p_conv], *s_out)
```

```python
import functools
import math

import jax
import jax.numpy as jnp
from jax import lax
from jax.experimental import pallas as pl
from jax.experimental.pallas import tpu as pltpu

F32 = jnp.float32
BF16 = jnp.bfloat16

N_MIXERS = 4
N_HEADS = 4
HEAD_DIM = 128
GROUP_W = N_HEADS * HEAD_DIM
DIFF_QK = HEAD_DIM // 2
ROT_DIM = DIFF_QK // 4
ROPE_THETA = 500000.0
N_IN_SLICES = 15
EPS = 1e-6
LOG2E = 1.4426950408889634
SB_SCALE_LOG2 = LOG2E / math.sqrt(HEAD_DIM)
DIFF_SCALE = 1.0 / math.sqrt(DIFF_QK)
PAD_T = 8
SB_EXIT_LOG2 = -104.0 * LOG2E
VMEM_LIMIT = 56 * 1024 * 1024
DECODE_PREFETCH_STEPS = 2
SB_LOOKAHEAD = 2

(S_CONF_A, S_CONF_B, S_CONF_G, S_SB_Q, S_SB_K, S_SB_V, S_SB_G, S_DF_Q, S_DF_K, S_DF_V, S_DF_G,
 S_SC_H, S_SC_B, S_SC_C, S_SC_G) = range(N_IN_SLICES)


def _silu(x):
    return x * (1.0 / (1.0 + jnp.exp(-x)))


def _cparams(sem):
    return pltpu.CompilerParams(dimension_semantics=sem, vmem_limit_bytes=VMEM_LIMIT)


def _log2(n):
    assert n & (n - 1) == 0
    return n.bit_length() - 1


def _head_cols(h):
    return slice(h * HEAD_DIM, (h + 1) * HEAD_DIM)


def _lane_chunks(x):
    return [x[:, c * HEAD_DIM:(c + 1) * HEAD_DIM] for c in range(x.shape[1] // HEAD_DIM)]


def _tree(op, xs):
    while len(xs) > 1:
        xs = [op(xs[n], xs[n + 1]) if n + 1 < len(xs) else xs[n] for n in range(0, len(xs), 2)]
    return xs[0]


def _qk(q, k):
    return lax.dot_general(q, k.astype(BF16), (((1,), (1,)), ((), ())), preferred_element_type=F32)


def _with_ones(v):
    v = v.astype(BF16)
    return jnp.concatenate([v, jnp.ones_like(v)], axis=1)


def _store_head_major(dst_ref, x):
    rows = x.shape[0]
    for h in range(N_HEADS):
        dst_ref[pl.ds(h, rows, stride=N_HEADS), :] = x[:, _head_cols(h)]


def _store_layer_head_major(dst_ref, x, layer, whole):
    if not whole:
        _store_head_major(dst_ref, x)
        return
    for other in range(dst_ref.shape[0]):
        if other != layer:
            dst_ref[other] = jnp.zeros(dst_ref.shape[1:], dst_ref.dtype)
    _store_head_major(dst_ref.at[layer], x)


def _inproj_kernel(x_ref, g_ref, w_ref, o_ref, xn_ref, *, slabs_per_step):
    @pl.when(pl.program_id(1) == 0)
    def _():
        x = x_ref[...]
        ms = jnp.mean(x * x, axis=-1, keepdims=True)
        xn_ref[...] = ((x * lax.rsqrt(ms + EPS)) * g_ref[...]).astype(BF16)

    res = jnp.dot(xn_ref[...], w_ref[...], preferred_element_type=F32)
    for k in range(slabs_per_step):
        o_ref[k] = res[:, k * GROUP_W:(k + 1) * GROUP_W]


def _layer_rows_spec(rows, depth, layer, whole):
    if whole:
        return pl.BlockSpec((depth, rows, HEAD_DIM), lambda i: (0, i, 0))
    return pl.BlockSpec((None, rows, HEAD_DIM), lambda i: (layer, i, 0))


def in_proj(x2d, g, w_bf16, layer, tm, slabs_per_step):
    m, d = x2d.shape
    ns = w_bf16.shape[2] // GROUP_W
    sps = slabs_per_step
    return pl.pallas_call(
        functools.partial(_inproj_kernel, slabs_per_step=sps),
        grid=(m // tm, ns // sps),
        in_specs=[pl.BlockSpec((tm, d), lambda i, j: (i, 0)),
                  pl.BlockSpec((1, d), lambda i, j: (0, 0)),
                  pl.BlockSpec((None, d, sps * GROUP_W), lambda i, j: (layer, 0, j))],
        out_specs=pl.BlockSpec((sps, tm, GROUP_W), lambda i, j: (j, i, 0)),
        out_shape=jax.ShapeDtypeStruct((ns, m, GROUP_W), F32),
        scratch_shapes=[pltpu.VMEM((tm, d), BF16)],
        compiler_params=_cparams(("parallel", "arbitrary")),
        name="in_proj",
    )(x2d, g.reshape(1, d), w_bf16)


def _outproj_kernel(ya_ref, yb_ref, yc_ref, yd_ref, w_ref, x_ref, fg_ref, o_ref, *, final):
    acc = x_ref[...]
    for m, y_ref in enumerate((ya_ref, yb_ref, yc_ref, yd_ref)):
        acc = acc + jnp.dot(y_ref[...], w_ref[m * GROUP_W:(m + 1) * GROUP_W, :],
                            preferred_element_type=F32)
    if final:
        ms = jnp.mean(acc * acc, axis=-1, keepdims=True)
        acc = (acc * lax.rsqrt(ms + EPS)) * fg_ref[...]
    o_ref[...] = acc


def out_proj(ys, w_bf16, layer, x2d, final_g, final, tm):
    m, d = x2d.shape
    yspec = pl.BlockSpec((tm, GROUP_W), lambda i: (i, 0))
    return pl.pallas_call(
        functools.partial(_outproj_kernel, final=final),
        grid=(m // tm,),
        in_specs=[yspec, yspec, yspec, yspec,
                  pl.BlockSpec((None, N_MIXERS * GROUP_W, d), lambda i: (layer, 0, 0)),
                  pl.BlockSpec((tm, d), lambda i: (i, 0)),
                  pl.BlockSpec((1, d), lambda i: (0, 0))],
        out_specs=pl.BlockSpec((tm, d), lambda i: (i, 0)),
        out_shape=jax.ShapeDtypeStruct((m, d), F32),
        compiler_params=_cparams(("parallel",)),
        name="out_proj",
    )(*ys, w_bf16, x2d, final_g.reshape(1, d))


def _sb_terms(z):
    neg = -z
    t = jnp.log2(1.0 + jnp.exp2(jnp.minimum(z, neg)))
    return jnp.minimum(z, 0.0) - t, jnp.minimum(neg, 0.0) - t


def _suffix_sums(lk, u):
    hi = lk.astype(BF16)
    lo = (lk - hi.astype(F32)).astype(BF16)
    return (jnp.dot(hi, u, preferred_element_type=F32) + jnp.dot(lo, u, preferred_element_type=F32))


def _strict_lower(n):
    row = lax.broadcasted_iota(jnp.int32, (n, n), 0)
    col = lax.broadcasted_iota(jnp.int32, (n, n), 1)
    return row, col


def _sb_prompt_kernel(q_ref, k_ref, v_ref, g_ref, o_ref, *scratch, tq, tk, hps):
    i = pl.program_id(1)
    acc_refs, carry_refs = scratch[:hps], scratch[hps:]
    n_diag = tq // tk
    row, col = _strict_lower(tk)
    u = jnp.where(row > col, 1.0, 0.0).astype(BF16)
    qs = [q_ref[:, _head_cols(h)].astype(BF16) for h in range(hps)]

    def scores(h, jb):
        start = pl.multiple_of(jb * tk, tk)
        z = _qk(qs[h], k_ref[pl.ds(start, tk), _head_cols(h)]) * SB_SCALE_LOG2
        return z, v_ref[pl.ds(start, tk), _head_cols(h)].astype(BF16)

    def sweep(jb, visible, first):
        for h in range(hps):
            z, v = scores(h, jb)
            ls, lk = _sb_terms(z)
            if visible is not None:
                lk = jnp.where(visible, lk, 0.0)
            e = ls + _suffix_sums(lk, u)
            if not first:
                e = e + carry_refs[h][...]
            a = jnp.exp2(e)
            if visible is not None:
                a = jnp.where(visible, a, 0.0)
            pv = jnp.dot(a.astype(BF16), v, preferred_element_type=F32)
            dc = jnp.sum(lk, axis=1, keepdims=True)
            if first:
                acc_refs[h][...] = pv
                carry_refs[h][...] = dc
            else:
                acc_refs[h][...] += pv
                carry_refs[h][...] += dc

    qrow = lax.broadcasted_iota(jnp.int32, (tq, tk), 0)
    kcol = lax.broadcasted_iota(jnp.int32, (tq, tk), 1)
    for d in reversed(range(n_diag)):
        sweep(i * n_diag + d, kcol + d * tk < qrow, d == n_diag - 1)

    def live():
        return _tree(jnp.maximum, [jnp.max(c[...]) for c in carry_refs]) > SB_EXIT_LOG2

    def cond(state):
        jb, go = state
        return jnp.logical_and(jb >= 0, go)

    def body(state):
        jb, _ = state
        sweep(jb, None, False)
        return jb - 1, live()

    lax.while_loop(cond, body, (i * n_diag - 1, live()))
    for h in range(hps):
        o_ref[:, _head_cols(h)] = (acc_refs[h][...] * _silu(g_ref[:, _head_cols(h)])).astype(o_ref.dtype)


def sb_prompt(proj, tq, tk, heads_per_step):
    _, t, _ = proj.shape
    hps = heads_per_step
    width = hps * HEAD_DIM

    def spec(rows, s, per_q):
        if per_q:
            return pl.BlockSpec((None, rows, width), lambda h, i: (s, i, h))
        return pl.BlockSpec((None, rows, width), lambda h, i: (s, 0, h))

    return pl.pallas_call(
        functools.partial(_sb_prompt_kernel, tq=tq, tk=tk, hps=hps),
        grid=(N_HEADS // hps, t // tq),
        in_specs=[spec(tq, S_SB_Q, True), spec(t, S_SB_K, False), spec(t, S_SB_V, False),
                  spec(tq, S_SB_G, True)],
        out_specs=pl.BlockSpec((tq, width), lambda h, i: (i, h)),
        out_shape=jax.ShapeDtypeStruct((t, GROUP_W), BF16),
        scratch_shapes=[pltpu.VMEM((tq, HEAD_DIM), F32)] * hps + [pltpu.VMEM((tq, 1), F32)] * hps,
        compiler_params=_cparams(("parallel", "arbitrary")),
        name="sb_prompt",
    )(proj, proj, proj, proj)


def _page_masks(rows, rows_per_head, page_rows):
    row = lax.broadcasted_iota(jnp.int32, (rows, page_rows), 0)
    col = lax.broadcasted_iota(jnp.int32, (rows, page_rows), 1)
    own = (col & (N_HEADS - 1)) == (row >> _log2(rows_per_head))
    return own, col >> _log2(N_HEADS), row & (PAD_T - 1)


def _sb_decode_kernel(pt_ref, q_ref, kn_ref, vn_ref, g_ref, u_ref, ck_ref, cv_ref, o_ref,
                      kbuf, vbuf, newk, newv, sem, acc_ref, carry_ref, *, layer, n_pages, n_tok):
    b = pl.program_id(0)
    rows = N_HEADS * PAD_T
    page_rows = newk.shape[0]

    ring = kbuf.shape[1]
    par = b % 2
    n_prime = min(SB_LOOKAHEAD, n_pages)

    def page_copies(bb, v, pr):
        pg = pt_ref[bb, n_pages - 1 - v]
        sl = v % ring
        return (pltpu.make_async_copy(ck_ref.at[layer, pg], kbuf.at[pr, sl], sem.at[0, pr, sl]),
                pltpu.make_async_copy(cv_ref.at[layer, pg], vbuf.at[pr, sl], sem.at[1, pr, sl]))

    def start_fetch(bb, v, pr):
        for c in page_copies(bb, v, pr):
            c.start()

    def wait_fetch(v):
        for c in page_copies(b, v, par):
            c.wait()

    @pl.when(b == 0)
    def _():
        for v in range(n_prime):
            start_fetch(b, v, par)

    @pl.when(b + 1 < pl.num_programs(0))
    def _():
        for v in range(n_prime):
            start_fetch(b + 1, v, 1 - par)

    qall = jnp.concatenate([q_ref[h] for h in range(N_HEADS)], axis=0).astype(BF16)
    own, key, tok = _page_masks(rows, PAD_T, page_rows)
    real_row = (lax.broadcasted_iota(jnp.int32, (rows, 1), 0) & (PAD_T - 1)) < n_tok
    u = u_ref[...]

    def attend(k, v, visible, carry):
        ls, lk = _sb_terms(_qk(qall, k) * SB_SCALE_LOG2)
        lk = jnp.where(visible, lk, 0.0)
        e = ls + _suffix_sums(lk, u)
        if carry is not None:
            e = e + carry
        a = jnp.where(visible, jnp.exp2(e), 0.0).astype(BF16)
        return (jnp.dot(a, v.astype(BF16), preferred_element_type=F32),
                jnp.sum(lk, axis=1, keepdims=True))

    def live(carry):
        return jnp.max(jnp.where(real_row, carry, -jnp.inf)) > SB_EXIT_LOG2

    newk[...] = jnp.zeros_like(newk)
    newv[...] = jnp.zeros_like(newv)
    newk[0:kn_ref.shape[0], :] = kn_ref[...]
    newv[0:vn_ref.shape[0], :] = vn_ref[...]
    pv, dc = attend(newk[...], newv[...], jnp.logical_and(own, key < tok), None)
    acc_ref[...] = pv
    carry_ref[...] = dc

    def cond(state):
        v, go = state
        return jnp.logical_and(v < n_pages, go)

    def body(state):
        v, _ = state
        wait_fetch(v)

        @pl.when(v + SB_LOOKAHEAD < n_pages)
        def _():
            start_fetch(b, v + SB_LOOKAHEAD, par)

        slot = v % ring
        carry = carry_ref[...]
        pv, dc = attend(kbuf[par, slot], vbuf[par, slot], own, carry)
        acc_ref[...] += pv
        carry = carry + dc
        carry_ref[...] = carry
        return v + 1, live(carry)

    v_end, _ = lax.while_loop(cond, body, (jnp.int32(0), live(carry_ref[...])))

    for k in range(SB_LOOKAHEAD):
        @pl.when(v_end + k < n_pages)
        def _(k=k):
            wait_fetch(v_end + k)

    acc = acc_ref[...]
    o_ref[...] = jnp.concatenate([acc[h * PAD_T:(h + 1) * PAD_T] * _silu(g_ref[h])
                                  for h in range(N_HEADS)], axis=1)


def sb_decode(q8, kn, vn, g8, cache_k, cache_v, page_table, layer, n_tok):
    bsz = page_table.shape[0]
    n_pages = page_table.shape[1]
    page_rows = cache_k.shape[2]
    tok_spec = pl.BlockSpec((None, PAD_T, GROUP_W), lambda b, pt: (b, 0, 0))
    head_spec = pl.BlockSpec((N_HEADS, PAD_T, HEAD_DIM), lambda b, pt: (0, b, 0))
    new_spec = pl.BlockSpec((None, kn.shape[1], HEAD_DIM), lambda b, pt: (b, 0, 0))
    any_spec = pl.BlockSpec(memory_space=pl.ANY)
    rows = N_HEADS * PAD_T
    row, col = _strict_lower(page_rows)
    u = jnp.where(row > col, 1.0, 0.0).astype(BF16)
    page_buf = pltpu.VMEM((2, SB_LOOKAHEAD + 1, page_rows, HEAD_DIM), F32)
    new_buf = pltpu.VMEM((page_rows, HEAD_DIM), F32)
    return pl.pallas_call(
        functools.partial(_sb_decode_kernel, layer=layer, n_pages=n_pages, n_tok=n_tok),
        grid_spec=pltpu.PrefetchScalarGridSpec(
            num_scalar_prefetch=1,
            grid=(bsz,),
            in_specs=[head_spec, new_spec, new_spec, head_spec,
                      pl.BlockSpec((page_rows, page_rows), lambda b, pt: (0, 0)), any_spec, any_spec],
            out_specs=tok_spec,
            scratch_shapes=[page_buf, page_buf, new_buf, new_buf,
                            pltpu.SemaphoreType.DMA((2, 2, SB_LOOKAHEAD + 1)),
                            pltpu.VMEM((rows, HEAD_DIM), F32), pltpu.VMEM((rows, 1), F32)]),
        out_shape=jax.ShapeDtypeStruct((bsz, PAD_T, GROUP_W), F32),
        compiler_params=_cparams(("arbitrary",)),
        name="sb_decode",
    )(page_table, q8, kn, vn, g8, u, cache_k, cache_v)


def _lambda(lq1_ref, lk1_ref, lq2_ref, lk2_ref, lam_init):
    s1 = jnp.sum(lq1_ref[...] * lk1_ref[...], axis=-1, keepdims=True)
    s2 = jnp.sum(lq2_ref[...] * lk2_ref[...], axis=-1, keepdims=True)
    return jnp.exp(s1) - jnp.exp(s2) + lam_init


def _diff_finish(o1, o2, lam, sg, gate, lam_init):
    w = o1 - lam * o2
    ms = jnp.mean(w * w, axis=-1, keepdims=True)
    y = (w * lax.rsqrt(ms + EPS)) * sg
    return (y * (1.0 - lam_init)) * _silu(gate)


def _split_components(q):
    lane = lax.broadcasted_iota(jnp.int32, q.shape, 1)
    zero = jnp.zeros_like(q)
    return jnp.concatenate([jnp.where(lane < DIFF_QK, q, zero), jnp.where(lane >= DIFF_QK, q, zero)],
                           axis=0)


def _diff_prompt_kernel(q_ref, k_ref, v_ref, g_ref, lq1_ref, lk1_ref, lq2_ref, lk2_ref, sg_ref,
                        o_ref, s0_ref, s1_ref, mt0_ref, mt1_ref, m_ref, acc_ref, *, tq, tk, lam_init):
    i = pl.program_id(1)
    qq = _split_components(q_ref[...])
    m_ref[...] = jnp.full_like(m_ref, -jnp.inf)
    acc_ref[...] = jnp.zeros_like(acc_ref)
    bufs = ((s0_ref, mt0_ref), (s1_ref, mt1_ref))

    def scores(jt, diag=None):
        s = _qk(qq, k_ref[pl.ds(pl.multiple_of(jt * tk, tk), tk), :])
        if diag is not None:
            row = lax.broadcasted_iota(jnp.int32, s.shape, 0) & (tq - 1)
            col = lax.broadcasted_iota(jnp.int32, s.shape, 1) + diag * tk
            s = jnp.where(col <= row, s, -jnp.inf)
        return s

    def score_stage(slot, s):
        s_ref, mt_ref = bufs[slot]
        s_ref[...] = s
        mt_ref[...] = jnp.broadcast_to(jnp.max(s, axis=1, keepdims=True), mt_ref.shape)

    def value_stage(jt, slot):
        s_ref, mt_ref = bufs[slot]
        m_old = m_ref[...]
        m_new = jnp.maximum(m_old, mt_ref[...])
        alpha = jnp.exp2(m_old - m_new)
        p = jnp.concatenate([jnp.exp2(c - m_new) for c in _lane_chunks(s_ref[...])], axis=1)
        v = v_ref[pl.ds(pl.multiple_of(jt * tk, tk), tk), :]
        pv = jnp.dot(p.astype(BF16), _with_ones(v), preferred_element_type=F32)
        acc_ref[...] = jnp.concatenate([alpha, alpha], axis=1) * acc_ref[...] + pv
        m_ref[...] = m_new

    d0 = 2 * i

    @pl.when(i == 0)
    def _():
        score_stage(0, scores(0, 0))
        score_stage(1, scores(1, 1))
        value_stage(0, 0)
        value_stage(1, 1)

    @pl.when(i > 0)
    def _():
        score_stage(0, scores(0))

        def body(g, c):
            score_stage(1, scores(2 * g + 1))
            value_stage(2 * g, 0)
            score_stage(0, scores(2 * g + 2))
            value_stage(2 * g + 1, 1)
            return c

        lax.fori_loop(0, i - 1, body, 0)
        score_stage(1, scores(d0 - 1))
        value_stage(d0 - 2, 0)
        score_stage(0, scores(d0, 0))
        value_stage(d0 - 1, 1)
        score_stage(1, scores(d0 + 1, 1))
        value_stage(d0, 0)
        value_stage(d0 + 1, 1)

    acc = acc_ref[...]
    o = acc[:, :HEAD_DIM] / acc[:, HEAD_DIM:]
    lam = _lambda(lq1_ref, lk1_ref, lq2_ref, lk2_ref, lam_init)
    o_ref[...] = _diff_finish(o[:tq], o[tq:], lam, sg_ref[...], g_ref[...], lam_init).astype(o_ref.dtype)


def diff_prompt(qb, kb, vb, proj, lam_params, subln_g, lam_init, tk):
    t = qb.shape[0]
    tq = 2 * tk
    small = pl.BlockSpec((1, DIFF_QK), lambda h, i: (0, 0))
    return pl.pallas_call(
        functools.partial(_diff_prompt_kernel, tq=tq, tk=tk, lam_init=lam_init),
        grid=(N_HEADS, t // tq),
        in_specs=[pl.BlockSpec((tq, HEAD_DIM), lambda h, i: (i, h)),
                  pl.BlockSpec((t, HEAD_DIM), lambda h, i: (0, h)),
                  pl.BlockSpec((t, HEAD_DIM), lambda h, i: (0, h)),
                  pl.BlockSpec((None, tq, HEAD_DIM), lambda h, i: (S_DF_G, i, h)),
                  small, small, small, small,
                  pl.BlockSpec((1, HEAD_DIM), lambda h, i: (0, 0))],
        out_specs=pl.BlockSpec((tq, HEAD_DIM), lambda h, i: (i, h)),
        out_shape=jax.ShapeDtypeStruct((t, GROUP_W), BF16),
        scratch_shapes=[pltpu.VMEM((2 * tq, tk), F32), pltpu.VMEM((2 * tq, tk), F32),
                        pltpu.VMEM((2 * tq, HEAD_DIM), F32), pltpu.VMEM((2 * tq, HEAD_DIM), F32),
                        pltpu.VMEM((2 * tq, HEAD_DIM), F32), pltpu.VMEM((2 * tq, 2 * HEAD_DIM), F32)],
        compiler_params=_cparams(("parallel", "arbitrary")),
        name="diff_prompt",
    )(qb, kb, vb, proj, *[p.reshape(1, DIFF_QK) for p in lam_params], subln_g.reshape(1, HEAD_DIM))


def _diff_decode_kernel(pt_ref, q_ref, kn_ref, vn_ref, g_ref, lq1_ref, lk1_ref, lq2_ref, lk2_ref,
                        sg_ref, ck_ref, cv_ref, o_ref, kbuf, vbuf, sem, m_ref, acc_ref, newk, newv, *,
                        layer, pages_per_step, n_chunks, group, lam_init):
    b = pl.program_id(0)
    c = pl.program_id(1)
    hrows = 2 * PAD_T
    rows = N_HEADS * hrows
    page_rows = newk.shape[0]

    n_slots = kbuf.shape[0]
    n_steps = pl.num_programs(0) * n_chunks
    step = b * n_chunks + c
    slot = step % n_slots

    def step_copies(t):
        bb, cc, sl = t // n_chunks, t % n_chunks, t % n_slots
        copies = []
        for n in range(pages_per_step):
            pg = pt_ref[bb, cc * pages_per_step + n]
            copies.append(pltpu.make_async_copy(ck_ref.at[layer, pg], kbuf.at[sl, n], sem.at[sl, 0, n]))
            copies.append(pltpu.make_async_copy(cv_ref.at[layer, pg], vbuf.at[sl, n], sem.at[sl, 1, n]))
        return copies

    def start_step(t):
        for n, cp in enumerate(step_copies(t)):
            cp.start(priority=n % 2)

    @pl.when(step == 0)
    def _():
        for t in range(DECODE_PREFETCH_STEPS):
            @pl.when(t < n_steps)
            def _(t=t):
                start_step(t)

    @pl.when(step + DECODE_PREFETCH_STEPS < n_steps)
    def _():
        start_step(step + DECODE_PREFETCH_STEPS)

    for cp in step_copies(step):
        cp.wait()

    qall = jnp.concatenate([_split_components(q_ref[h] * (DIFF_SCALE * LOG2E))
                            for h in range(N_HEADS)], axis=0).astype(BF16)
    zq = jnp.zeros_like(qall)
    qpair = jnp.concatenate([jnp.concatenate([qall, zq], axis=1),
                             jnp.concatenate([zq, qall], axis=1)], axis=0)
    own, key, tok = _page_masks(rows, hrows, page_rows)

    @pl.when(c == 0)
    def _():
        m_ref[...] = jnp.full_like(m_ref, -jnp.inf)
        acc_ref[...] = jnp.zeros_like(acc_ref)

    def page_scores(ks):
        out = []
        for n in range(0, len(ks) - 1, 2):
            kk = jnp.concatenate([ks[n].astype(BF16), ks[n + 1].astype(BF16)], axis=1)
            s2 = _qk(qpair, kk)
            out += [s2[:rows], s2[rows:]]
        if len(ks) % 2:
            out.append(_qk(qall, ks[-1]))
        return out

    def update(ks, vs, visible):
        s = jnp.concatenate([jnp.where(visible, z, -jnp.inf) for z in page_scores(ks)], axis=1)
        m_prev = m_ref[...]
        m_new = jnp.maximum(m_prev, jnp.max(s, axis=1, keepdims=True))
        p = jnp.exp2(s - m_new).astype(BF16)
        pv = _tree(jnp.add, [jnp.dot(p[:, n * page_rows:(n + 1) * page_rows], _with_ones(v),
                                     preferred_element_type=F32) for n, v in enumerate(vs)])
        return m_new, jnp.exp2(m_prev - m_new) * acc_ref[...] + pv

    for g0 in range(0, pages_per_step, group):
        m_new, acc_new = update([kbuf[slot, n] for n in range(g0, g0 + group)],
                                [vbuf[slot, n] for n in range(g0, g0 + group)], own)
        m_ref[...] = m_new
        acc_ref[...] = acc_new

    @pl.when(c == pl.num_programs(1) - 1)
    def _():
        newk[...] = jnp.zeros_like(newk)
        newv[...] = jnp.zeros_like(newv)
        newk[0:kn_ref.shape[0], :] = kn_ref[...]
        newv[0:vn_ref.shape[0], :] = vn_ref[...]
        _, acc = update([newk[...]], [newv[...]], jnp.logical_and(own, key <= tok))
        o = acc[:, :HEAD_DIM] / acc[:, HEAD_DIM:]
        lam = _lambda(lq1_ref, lk1_ref, lq2_ref, lk2_ref, lam_init)
        sg = sg_ref[...]
        outs = []
        for h in range(N_HEADS):
            o1 = o[h * hrows:h * hrows + PAD_T]
            o2 = o[h * hrows + PAD_T:(h + 1) * hrows]
            outs.append(_diff_finish(o1, o2, lam, sg, g_ref[h], lam_init))
        o_ref[...] = jnp.concatenate(outs, axis=1)


def diff_decode(q8, kn, vn, g8, cache_k, cache_v, page_table, lam_params, subln_g, lam_init,
                layer, pages_per_step, group):
    bsz = page_table.shape[0]
    n_pages = page_table.shape[1]
    page_rows = cache_k.shape[2]
    pps = pages_per_step
    tok_spec = pl.BlockSpec((None, PAD_T, GROUP_W), lambda b, c, pt: (b, 0, 0))
    head_spec = pl.BlockSpec((N_HEADS, PAD_T, HEAD_DIM), lambda b, c, pt: (0, b, 0))
    new_spec = pl.BlockSpec((None, kn.shape[1], HEAD_DIM), lambda b, c, pt: (b, 0, 0))
    small = pl.BlockSpec((1, DIFF_QK), lambda b, c, pt: (0, 0))

    rows = N_HEADS * 2 * PAD_T
    any_spec = pl.BlockSpec(memory_space=pl.ANY)
    n_slots = DECODE_PREFETCH_STEPS + 1
    page_bufs = pltpu.VMEM((n_slots, pps, page_rows, HEAD_DIM), F32)
    new_buf = pltpu.VMEM((page_rows, HEAD_DIM), F32)
    return pl.pallas_call(
        functools.partial(_diff_decode_kernel, layer=layer, pages_per_step=pps,
                          n_chunks=n_pages // pps, group=group, lam_init=lam_init),
        grid_spec=pltpu.PrefetchScalarGridSpec(
            num_scalar_prefetch=1,
            grid=(bsz, n_pages // pps),
            in_specs=[head_spec, new_spec, new_spec, head_spec, small, small, small, small,
                      pl.BlockSpec((1, HEAD_DIM), lambda b, c, pt: (0, 0)), any_spec, any_spec],
            out_specs=tok_spec,
            scratch_shapes=[page_bufs, page_bufs, pltpu.SemaphoreType.DMA((n_slots, 2, pps)),
                            pltpu.VMEM((rows, 1), F32), pltpu.VMEM((rows, 2 * HEAD_DIM), F32),
                            new_buf, new_buf]),
        out_shape=jax.ShapeDtypeStruct((bsz, PAD_T, GROUP_W), F32),
        compiler_params=_cparams(("arbitrary", "arbitrary")),
        name="diff_decode",
    )(page_table, q8, kn, vn, g8, *[p.reshape(1, DIFF_QK) for p in lam_params],
      subln_g.reshape(1, HEAD_DIM), cache_k, cache_v)


CONF_HALO = 32
SC_HALO = 8
SUBLANES = 8
MIX_ROWS = 32


def _rope_tables(pos):
    half = ROT_DIM // 2
    inv = ROPE_THETA ** (-2.0 * jnp.arange(half, dtype=F32) / ROT_DIM)
    ang = pos.astype(F32)[:, None] * inv[None, :]
    cos, sin = jnp.cos(ang), jnp.sin(ang)
    j = jnp.arange(HEAD_DIM) % DIFF_QK
    f = j % half
    c = jnp.where(j < ROT_DIM, cos[:, f], 1.0)
    sa = jnp.where((j >= half) & (j < ROT_DIM), sin[:, f], 0.0)
    sb = jnp.where(j < half, -sin[:, f], 0.0)
    return c, sa, sb


def _rope_rows(x, c, sa, sb):
    half = ROT_DIM // 2
    out = []
    for xh in _lane_chunks(x):
        out.append(xh * c + pltpu.roll(xh, half, 1) * sa + pltpu.roll(xh, HEAD_DIM - half, 1) * sb)
    return jnp.concatenate(out, axis=1)


def _prompt_mix_kernel(a_ref, b_ref, g_ref, dq_ref, dk_ref, dv_ref, sh_ref, sb_ref, sc_ref, sg_ref,
                       sk_ref, sv_ref, cw_ref, cb_ref, lg_ref, lb_ref, sw_ref, rc_ref, rsa_ref, rsb_ref,
                       *rest, tb, conf_w, sc_w, layer, whole):
    (ya_ref, yd_ref, dqb_ref, sk_out_ref, sv_out_ref, dk_out_ref, dv_out_ref, dkb_ref, dvb_ref,
     nconf_ref, nsc_ref, hb, hs, ub) = rest[-14:]
    i = pl.program_id(0)

    @pl.when(i == 0)
    def _():
        hb[0:CONF_HALO, :] = jnp.zeros((CONF_HALO, GROUP_W), F32)
        ub[0:SC_HALO, :] = jnp.zeros((SC_HALO, GROUP_W), F32)

    @pl.when(i > 0)
    def _():
        hb[0:CONF_HALO, :] = hb[tb:tb + CONF_HALO, :]
        ub[0:SC_HALO, :] = ub[tb:tb + SC_HALO, :]

    hb[CONF_HALO:CONF_HALO + tb, :] = a_ref[...] * (1.0 / (1.0 + jnp.exp(-b_ref[...])))
    ub[SC_HALO:SC_HALO + tb, :] = sc_ref[...] * sh_ref[...]

    off = CONF_HALO - (conf_w - 1)
    span = hs.shape[1]
    for k in range(1, SUBLANES):
        hs[k - 1, :, :] = hb[k:k + span, :]

    def tap(w_ref, w):
        return jnp.concatenate([w_ref[w]] * (MIX_ROWS // SUBLANES), axis=0)

    def window(r):
        k, base = r % SUBLANES, r - r % SUBLANES
        if k == 0:
            return hb[base:base + MIX_ROWS, :]
        return hs[k - 1, base:base + MIX_ROWS, :]

    for r0 in range(0, tb, MIX_ROWS):
        rows = slice(r0, r0 + MIX_ROWS)
        acc = jnp.zeros((MIX_ROWS, GROUP_W), F32) + cb_ref[...]
        for w in range(conf_w):
            acc = acc + tap(cw_ref, w) * window(r0 + off + w)
        mu = jnp.mean(acc, axis=-1, keepdims=True)
        xc = acc - mu
        y = (xc * lax.rsqrt(jnp.mean(xc * xc, axis=-1, keepdims=True) + EPS)) * lg_ref[...] + lb_ref[...]
        ya_ref[rows, :] = (_silu(y) * _silu(g_ref[rows, :])).astype(ya_ref.dtype)
        u0 = r0 + SC_HALO - (sc_w - 1)
        conv = jnp.zeros((MIX_ROWS, GROUP_W), F32)
        for w in range(sc_w):
            conv = conv + tap(sw_ref, w) * ub[u0 + w:u0 + w + MIX_ROWS, :]
        yd_ref[rows, :] = ((sb_ref[rows, :] * conv) * _silu(sg_ref[rows, :])).astype(yd_ref.dtype)

    c, sa, sb = rc_ref[...], rsa_ref[...], rsb_ref[...]
    dqb_ref[...] = (_rope_rows(dq_ref[...], c, sa, sb) * (DIFF_SCALE * LOG2E)).astype(dqb_ref.dtype)
    dk = _rope_rows(dk_ref[...], c, sa, sb)
    _store_layer_head_major(dk_out_ref, dk, layer, whole)
    dkb_ref[...] = dk.astype(dkb_ref.dtype)
    dv = dv_ref[...]
    dvb_ref[...] = dv.astype(dvb_ref.dtype)
    _store_layer_head_major(dv_out_ref, dv, layer, whole)
    _store_layer_head_major(sk_out_ref, sk_ref[...], layer, whole)
    _store_layer_head_major(sv_out_ref, sv_ref[...], layer, whole)

    @pl.when(i == pl.num_programs(0) - 1)
    def _():
        nconf_ref[...] = hb[CONF_HALO + tb - (conf_w - 1):CONF_HALO + tb, :]
        nsc_ref[...] = ub[SC_HALO + tb - (sc_w - 1):SC_HALO + tb, :]


def prompt_mix(proj, conf_w, conf_b, ln_g, ln_b, sc_w, rope_tabs, tb, kv_bufs=None, layer=0, depth=1):
    _, t, _ = proj.shape
    cw, scw = conf_w.shape[0], sc_w.shape[0]

    def slab(s):
        return pl.BlockSpec((None, tb, GROUP_W), lambda i: (s, i, 0))

    def full(r):
        return pl.BlockSpec((r, GROUP_W), lambda i: (0, 0))

    def taps(n):
        return pl.BlockSpec((n, SUBLANES, GROUP_W), lambda i: (0, 0, 0))

    def replicate(w):
        return jnp.broadcast_to(w[:, None, :], (w.shape[0], SUBLANES, w.shape[1]))

    row_blk = pl.BlockSpec((tb, GROUP_W), lambda i: (i, 0))
    tab_blk = pl.BlockSpec((tb, HEAD_DIM), lambda i: (i, 0))
    slabs = (S_CONF_A, S_CONF_B, S_CONF_G, S_DF_Q, S_DF_K, S_DF_V, S_SC_H, S_SC_B, S_SC_C, S_SC_G,
             S_SB_K, S_SB_V)
    act = jax.ShapeDtypeStruct((t, GROUP_W), BF16)
    per_layer = jax.ShapeDtypeStruct((depth, t * N_HEADS, HEAD_DIM), F32)
    n_kv = 4
    alias_in = [] if kv_bufs is None else list(kv_bufs)
    kv_spec = _layer_rows_spec(tb * N_HEADS, depth, layer, not alias_in)
    n_in = len(slabs) + 8
    return pl.pallas_call(
        functools.partial(_prompt_mix_kernel, tb=tb, conf_w=cw, sc_w=scw, layer=layer,
                          whole=not alias_in),
        grid=(t // tb,),
        in_specs=[slab(s) for s in slabs]
                 + [taps(cw), full(1), full(1), full(1), taps(scw), tab_blk, tab_blk, tab_blk]
                 + [pl.BlockSpec(memory_space=pl.ANY)] * len(alias_in),
        out_specs=[row_blk, row_blk, row_blk] + [kv_spec] * n_kv
                  + [row_blk, row_blk, full(cw - 1), full(scw - 1)],
        out_shape=[act, act, act] + [per_layer] * n_kv + [act, act,
                   jax.ShapeDtypeStruct((cw - 1, GROUP_W), F32),
                   jax.ShapeDtypeStruct((scw - 1, GROUP_W), F32)],
        scratch_shapes=[pltpu.VMEM((CONF_HALO + tb, GROUP_W), F32),
                        pltpu.VMEM((SUBLANES - 1, CONF_HALO + tb - SUBLANES, GROUP_W), F32),
                        pltpu.VMEM((SC_HALO + tb, GROUP_W), F32)],
        input_output_aliases={n_in + n: 3 + n for n in range(len(alias_in))},
        compiler_params=_cparams(("arbitrary",)),
        name="prompt_mix",
    )(*([proj] * len(slabs)), replicate(conf_w), conf_b.reshape(1, -1), ln_g.reshape(1, -1),
      ln_b.reshape(1, -1), replicate(sc_w), *rope_tabs, *alias_in)


def _decode_mix_kernel(proj_ref, sconf_ref, ssc_ref, cw_ref, cb_ref, lg_ref, lb_ref, sw_ref,
                       rc_ref, rsa_ref, rsb_ref,
                       ya_ref, yd_ref, sbq_ref, sbg_ref, dfq_ref, dfg_ref, sbk_ref, sbv_ref, dfk_ref,
                       dfv_ref, nconf_ref, nsc_ref, *, bsz, n_tok, conf_w, sc_w):
    def tok(slab, t):
        return proj_ref[slab, t * bsz:(t + 1) * bsz, :]

    def put_tok(ref, t, x):
        ref[t * bsz:(t + 1) * bsz, :] = x.astype(ref.dtype)

    def put_pad(ref, t, x):
        for h in range(N_HEADS):
            ref[h, pl.ds(t, bsz, stride=PAD_T), :] = x[:, _head_cols(h)]

    def put_heads(ref, t, x):
        for h in range(N_HEADS):
            ref[pl.ds(t * N_HEADS + h, bsz, stride=n_tok * N_HEADS), :] = x[:, _head_cols(h)]

    def tap(w_ref, w):
        if bsz % SUBLANES:
            return jnp.broadcast_to(w_ref[w, 0:1, :], (bsz, GROUP_W))
        return jnp.concatenate([w_ref[w]] * (bsz // SUBLANES), axis=0)

    h = [tok(S_CONF_A, t) * (1.0 / (1.0 + jnp.exp(-tok(S_CONF_B, t)))) for t in range(n_tok)]

    def hist(r):
        return sconf_ref[r] if r < conf_w - 1 else h[r - (conf_w - 1)]

    for t in range(n_tok):
        acc = jnp.zeros((bsz, GROUP_W), F32) + cb_ref[...]
        for w in range(conf_w):
            acc = acc + tap(cw_ref, w) * hist(t + w)
        mu = jnp.mean(acc, axis=-1, keepdims=True)
        xc = acc - mu
        y = (xc * lax.rsqrt(jnp.mean(xc * xc, axis=-1, keepdims=True) + EPS)) * lg_ref[...] + lb_ref[...]
        put_tok(ya_ref, t, _silu(y) * _silu(tok(S_CONF_G, t)))
    for r in range(conf_w - 1):
        nconf_ref[r] = hist(r + n_tok)

    u = [tok(S_SC_C, t) * tok(S_SC_H, t) for t in range(n_tok)]

    def uhist(r):
        return ssc_ref[r] if r < sc_w - 1 else u[r - (sc_w - 1)]

    for t in range(n_tok):
        conv = jnp.zeros((bsz, GROUP_W), F32)
        for w in range(sc_w):
            conv = conv + tap(sw_ref, w) * uhist(t + w)
        put_tok(yd_ref, t, (tok(S_SC_B, t) * conv) * _silu(tok(S_SC_G, t)))
    for r in range(sc_w - 1):
        nsc_ref[r] = uhist(r + n_tok)

    for ref in (sbq_ref, sbg_ref, dfq_ref, dfg_ref):
        ref[...] = jnp.zeros_like(ref)
    for t in range(n_tok):
        tabs = [jnp.broadcast_to(r[t:t + 1, :], (bsz, HEAD_DIM)) for r in (rc_ref, rsa_ref, rsb_ref)]
        put_pad(sbq_ref, t, tok(S_SB_Q, t))
        put_pad(sbg_ref, t, tok(S_SB_G, t))
        put_pad(dfq_ref, t, _rope_rows(tok(S_DF_Q, t), *tabs))
        put_pad(dfg_ref, t, tok(S_DF_G, t))
        put_heads(sbk_ref, t, tok(S_SB_K, t))
        put_heads(sbv_ref, t, tok(S_SB_V, t))
        put_heads(dfk_ref, t, _rope_rows(tok(S_DF_K, t), *tabs))
        put_heads(dfv_ref, t, tok(S_DF_V, t))


def decode_mix(proj, state_conf, state_sc, conf_w, conf_b, ln_g, ln_b, sc_w, rope_tabs, bsz, n_tok):
    cw, scw = conf_w.shape[0], sc_w.shape[0]

    def replicate(w):
        return jnp.broadcast_to(w[:, None, :], (w.shape[0], SUBLANES, w.shape[1]))

    act = jax.ShapeDtypeStruct((bsz * n_tok, GROUP_W), BF16)
    pad = jax.ShapeDtypeStruct((N_HEADS, bsz * PAD_T, HEAD_DIM), F32)
    hm = jax.ShapeDtypeStruct((bsz * n_tok * N_HEADS, HEAD_DIM), F32)
    return pl.pallas_call(
        functools.partial(_decode_mix_kernel, bsz=bsz, n_tok=n_tok, conf_w=cw, sc_w=scw),
        out_shape=[act, act, pad, pad, pad, pad, hm, hm, hm, hm,
                   jax.ShapeDtypeStruct(state_conf.shape, F32), jax.ShapeDtypeStruct(state_sc.shape, F32)],
        compiler_params=pltpu.CompilerParams(vmem_limit_bytes=VMEM_LIMIT),
        name="decode_mix",
    )(proj, state_conf, state_sc, replicate(conf_w), conf_b.reshape(1, -1), ln_g.reshape(1, -1),
      ln_b.reshape(1, -1), replicate(sc_w), *rope_tabs)


def _pick_tile(n, pref):
    t = min(n, pref)
    while n % t:
        t //= 2
    return t


def kernel(x_prompt, x_sample, cache_sb_k, cache_sb_v, cache_diff_k, cache_diff_v, state_conf_conv,
           state_short_conv, page_table, norm_g, w_in, conf_dw_w, conf_dw_b, conf_ln_g, conf_ln_b,
           diff_lq1, diff_lk1, diff_lq2, diff_lk2, diff_subln_g, sc_conv_w, w_out, final_norm_g):
    bp, tp, d = x_prompt.shape
    bs, ts, _ = x_sample.shape
    depth = w_in.shape[0]
    n_pages = page_table.shape[1]
    page = cache_sb_k.shape[2]
    n_past = n_pages * page
    pos_p = jnp.arange(tp, dtype=jnp.int32)
    pos_s = n_past + jnp.arange(ts, dtype=jnp.int32)
    assert bp == 1 and ts <= PAD_T

    caches = [c.reshape(c.shape[0], c.shape[1], page * N_HEADS, HEAD_DIM)
              for c in (cache_sb_k, cache_sb_v, cache_diff_k, cache_diff_v)]
    w_in_b = w_in.astype(BF16)
    w_out_b = w_out.astype(BF16)

    hp = x_prompt.reshape(bp * tp, d)
    hs = jnp.swapaxes(x_sample, 0, 1).reshape(ts * bs, d)
    conf_states = jnp.swapaxes(state_conf_conv, 1, 2)
    sc_states = jnp.swapaxes(state_short_conv, 1, 2)
    tm_p = _pick_tile(bp * tp, 1024)
    slabs_per_step_p = 3
    slabs_per_step = 5
    assert N_IN_SLICES % slabs_per_step == 0 and N_IN_SLICES % slabs_per_step_p == 0
    tm_s = _pick_tile(bs * ts, 128)
    tq_sb = _pick_tile(tp, 512)
    tk_sb = _pick_tile(tq_sb, 256)
    tk_df = _pick_tile(tp // 2, 512)
    tb_mix = _pick_tile(tp, 256)
    assert tb_mix >= CONF_HALO >= conf_dw_w.shape[1] - 1 and SC_HALO >= sc_conv_w.shape[1] - 1
    rope_tabs_p = _rope_tables(pos_p)
    rope_tabs_s = _rope_tables(pos_s)
    pps = _pick_tile(n_pages, 16)
    pgroup = _pick_tile(pps, 2)

    p_conv = [[], []]
    s_new = [[] for _ in range(6)]
    kv_bufs = None
    for l in range(depth):
        lam_init = 0.8 - 0.6 * math.exp(-0.3 * l)
        lam_params = (diff_lq1[l], diff_lk1[l], diff_lq2[l], diff_lk2[l])
        final = l == depth - 1

        proj = in_proj(hp, norm_g[l], w_in_b, l, tm_p, slabs_per_step_p)
        ya, yd, dqb, *kv_bufs, dkb, dvb, new_conf, new_sc = prompt_mix(
            proj, conf_dw_w[l], conf_dw_b[l], conf_ln_g[l], conf_ln_b[l], sc_conv_w[l], rope_tabs_p,
            tb_mix, kv_bufs=kv_bufs, layer=l, depth=depth)
        yb = sb_prompt(proj, tq_sb, tk_sb, 2)
        yc = diff_prompt(dqb, dkb, dvb, proj, lam_params, diff_subln_g[l], lam_init, tk_df)
        hp = out_proj((ya, yb, yc, yd), w_out_b, l, hp, final_norm_g, final,
                      _pick_tile(bp * tp, 512))
        p_conv[0].append(new_conf[None])
        p_conv[1].append(new_sc[None])
        heads = lambda a, bb, tt: a.reshape(bb, tt, N_HEADS, HEAD_DIM)

        proj = in_proj(hs, norm_g[l], w_in_b, l, tm_s, slabs_per_step)
        ya, yd, sbq, sbg, dfq, dfg, sbk, sbv, dfk, dfv, new_conf, new_sc = decode_mix(
            proj, conf_states[l], sc_states[l], conf_dw_w[l], conf_dw_b[l], conf_ln_g[l],
            conf_ln_b[l], sc_conv_w[l], rope_tabs_s, bs, ts)
        new_rows = lambda a: a.reshape(bs, ts * N_HEADS, HEAD_DIM)
        yb = sb_decode(sbq, new_rows(sbk), new_rows(sbv), sbg, caches[0], caches[1], page_table, l, ts)
        yc = diff_decode(dfq, new_rows(dfk), new_rows(dfv), dfg, caches[2], caches[3], page_table,
                         lam_params, diff_subln_g[l], lam_init, l, pps, pgroup)
        flat = lambda a: jnp.swapaxes(a[:, :ts], 0, 1).reshape(ts * bs, GROUP_W).astype(BF16)
        hs = out_proj((ya, flat(yb), flat(yc), yd), w_out_b, l, hs, final_norm_g, final, tm_s)
        for n, a in enumerate((sbk, sbv, dfk, dfv)):
            s_new[n].append(a.reshape(bs, ts, N_HEADS, HEAD_DIM))
        s_new[4].append(new_conf)
        s_new[5].append(new_sc)

    y_prompt = hp.reshape(bp, tp, d)
    y_sample = jnp.swapaxes(hs.reshape(ts, bs, d), 0, 1)
    p_kv = [a.reshape(depth, bp, tp, N_HEADS, HEAD_DIM)
            for a in kv_bufs]
    s_out = [jnp.stack(a, 0) for a in s_new]
    s_out[4:] = [jnp.swapaxes(a, 1, 2) for a in s_out[4:]]
    return (y_prompt, y_sample, *p_kv, *[jnp.stack(a, 0) for a in p_conv], *s_out)
```

```python
import functools
import math

import jax
import jax.numpy as jnp
from jax import lax
from jax.experimental import pallas as pl
from jax.experimental.pallas import tpu as pltpu

F32 = jnp.float32
BF16 = jnp.bfloat16

N_MIXERS = 4
N_HEADS = 4
HEAD_DIM = 128
GROUP_W = N_HEADS * HEAD_DIM
DIFF_QK = HEAD_DIM // 2
ROT_DIM = DIFF_QK // 4
ROPE_THETA = 500000.0
N_IN_SLICES = 15
EPS = 1e-6
LOG2E = 1.4426950408889634
SB_SCALE_LOG2 = LOG2E / math.sqrt(HEAD_DIM)
DIFF_SCALE = 1.0 / math.sqrt(DIFF_QK)
PAD_T = 8
SB_EXIT_LOG2 = -104.0 * LOG2E
VMEM_LIMIT = 56 * 1024 * 1024
DECODE_PREFETCH_STEPS = 3
SB_LOOKAHEAD = 2

(S_CONF_A, S_CONF_B, S_CONF_G, S_SB_Q, S_SB_K, S_SB_V, S_SB_G, S_DF_Q, S_DF_K, S_DF_V, S_DF_G,
 S_SC_H, S_SC_B, S_SC_C, S_SC_G) = range(N_IN_SLICES)


def _silu(x):
    return x * (1.0 / (1.0 + jnp.exp(-x)))


def _cparams(sem):
    return pltpu.CompilerParams(dimension_semantics=sem, vmem_limit_bytes=VMEM_LIMIT)


def _log2(n):
    assert n & (n - 1) == 0
    return n.bit_length() - 1


def _head_cols(h):
    return slice(h * HEAD_DIM, (h + 1) * HEAD_DIM)


def _lane_chunks(x):
    return [x[:, c * HEAD_DIM:(c + 1) * HEAD_DIM] for c in range(x.shape[1] // HEAD_DIM)]


def _tree(op, xs):
    while len(xs) > 1:
        xs = [op(xs[n], xs[n + 1]) if n + 1 < len(xs) else xs[n] for n in range(0, len(xs), 2)]
    return xs[0]


def _qk(q, k):
    return lax.dot_general(q, k.astype(BF16), (((1,), (1,)), ((), ())), preferred_element_type=F32)


def _with_ones(v):
    v = v.astype(BF16)
    return jnp.concatenate([v, jnp.ones_like(v)], axis=1)


def _store_head_major(dst_ref, x):
    rows = x.shape[0]
    for h in range(N_HEADS):
        dst_ref[pl.ds(h, rows, stride=N_HEADS), :] = x[:, _head_cols(h)]


def _store_layer_head_major(dst_ref, x, layer, whole):
    if not whole:
        _store_head_major(dst_ref, x)
        return
    for other in range(dst_ref.shape[0]):
        if other != layer:
            dst_ref[other] = jnp.zeros(dst_ref.shape[1:], dst_ref.dtype)
    _store_head_major(dst_ref.at[layer], x)


def _inproj_kernel(x_ref, g_ref, w_ref, o_ref, xn_ref, *, slabs_per_step):
    @pl.when(pl.program_id(1) == 0)
    def _():
        x = x_ref[...]
        ms = jnp.mean(x * x, axis=-1, keepdims=True)
        xn_ref[...] = ((x * lax.rsqrt(ms + EPS)) * g_ref[...]).astype(BF16)

    res = jnp.dot(xn_ref[...], w_ref[...], preferred_element_type=F32)
    for k in range(slabs_per_step):
        o_ref[k] = res[:, k * GROUP_W:(k + 1) * GROUP_W]


def _layer_rows_spec(rows, depth, layer, whole):
    if whole:
        return pl.BlockSpec((depth, rows, HEAD_DIM), lambda i: (0, i, 0))
    return pl.BlockSpec((None, rows, HEAD_DIM), lambda i: (layer, i, 0))


def in_proj(x2d, g, w_bf16, layer, tm, slabs_per_step):
    m, d = x2d.shape
    ns = w_bf16.shape[2] // GROUP_W
    sps = slabs_per_step
    return pl.pallas_call(
        functools.partial(_inproj_kernel, slabs_per_step=sps),
        grid=(m // tm, ns // sps),
        in_specs=[pl.BlockSpec((tm, d), lambda i, j: (i, 0)),
                  pl.BlockSpec((1, d), lambda i, j: (0, 0)),
                  pl.BlockSpec((None, d, sps * GROUP_W), lambda i, j: (layer, 0, j))],
        out_specs=pl.BlockSpec((sps, tm, GROUP_W), lambda i, j: (j, i, 0)),
        out_shape=jax.ShapeDtypeStruct((ns, m, GROUP_W), F32),
        scratch_shapes=[pltpu.VMEM((tm, d), BF16)],
        compiler_params=_cparams(("parallel", "arbitrary")),
        name="in_proj",
    )(x2d, g.reshape(1, d), w_bf16)


def _outproj_kernel(ya_ref, yb_ref, yc_ref, yd_ref, w_ref, x_ref, fg_ref, o_ref, *, final):
    acc = x_ref[...]
    for m, y_ref in enumerate((ya_ref, yb_ref, yc_ref, yd_ref)):
        acc = acc + jnp.dot(y_ref[...], w_ref[m * GROUP_W:(m + 1) * GROUP_W, :],
                            preferred_element_type=F32)
    if final:
        ms = jnp.mean(acc * acc, axis=-1, keepdims=True)
        acc = (acc * lax.rsqrt(ms + EPS)) * fg_ref[...]
    o_ref[...] = acc


def out_proj(ys, w_bf16, layer, x2d, final_g, final, tm):
    m, d = x2d.shape
    yspec = pl.BlockSpec((tm, GROUP_W), lambda i: (i, 0))
    return pl.pallas_call(
        functools.partial(_outproj_kernel, final=final),
        grid=(m // tm,),
        in_specs=[yspec, yspec, yspec, yspec,
                  pl.BlockSpec((None, N_MIXERS * GROUP_W, d), lambda i: (layer, 0, 0)),
                  pl.BlockSpec((tm, d), lambda i: (i, 0)),
                  pl.BlockSpec((1, d), lambda i: (0, 0))],
        out_specs=pl.BlockSpec((tm, d), lambda i: (i, 0)),
        out_shape=jax.ShapeDtypeStruct((m, d), F32),
        compiler_params=_cparams(("parallel",)),
        name="out_proj",
    )(*ys, w_bf16, x2d, final_g.reshape(1, d))


def _sb_terms(z):
    neg = -z
    t = jnp.log2(1.0 + jnp.exp2(jnp.minimum(z, neg)))
    return jnp.minimum(z, 0.0) - t, jnp.minimum(neg, 0.0) - t


def _suffix_sums(lk, u):
    hi = lk.astype(BF16)
    lo = (lk - hi.astype(F32)).astype(BF16)
    return (jnp.dot(hi, u, preferred_element_type=F32) + jnp.dot(lo, u, preferred_element_type=F32))


def _strict_lower(n):
    row = lax.broadcasted_iota(jnp.int32, (n, n), 0)
    col = lax.broadcasted_iota(jnp.int32, (n, n), 1)
    return row, col


def _sb_prompt_kernel(q_ref, k_ref, v_ref, g_ref, o_ref, *scratch, tq, tk, hps):
    i = pl.program_id(1)
    acc_refs, carry_refs = scratch[:hps], scratch[hps:]
    n_diag = tq // tk
    row, col = _strict_lower(tk)
    u = jnp.where(row > col, 1.0, 0.0).astype(BF16)
    qs = [q_ref[:, _head_cols(h)].astype(BF16) for h in range(hps)]

    def scores(h, jb):
        start = pl.multiple_of(jb * tk, tk)
        z = _qk(qs[h], k_ref[pl.ds(start, tk), _head_cols(h)]) * SB_SCALE_LOG2
        return z, v_ref[pl.ds(start, tk), _head_cols(h)].astype(BF16)

    def sweep(jb, visible, first):
        for h in range(hps):
            z, v = scores(h, jb)
            ls, lk = _sb_terms(z)
            if visible is not None:
                lk = jnp.where(visible, lk, 0.0)
            e = ls + _suffix_sums(lk, u)
            if not first:
                e = e + carry_refs[h][...]
            a = jnp.exp2(e)
            if visible is not None:
                a = jnp.where(visible, a, 0.0)
            pv = jnp.dot(a.astype(BF16), v, preferred_element_type=F32)
            dc = jnp.sum(lk, axis=1, keepdims=True)
            if first:
                acc_refs[h][...] = pv
                carry_refs[h][...] = dc
            else:
                acc_refs[h][...] += pv
                carry_refs[h][...] += dc

    qrow = lax.broadcasted_iota(jnp.int32, (tq, tk), 0)
    kcol = lax.broadcasted_iota(jnp.int32, (tq, tk), 1)
    for d in reversed(range(n_diag)):
        sweep(i * n_diag + d, kcol + d * tk < qrow, d == n_diag - 1)

    def live():
        return _tree(jnp.maximum, [jnp.max(c[...]) for c in carry_refs]) > SB_EXIT_LOG2

    def cond(state):
        jb, go = state
        return jnp.logical_and(jb >= 0, go)

    def body(state):
        jb, _ = state
        sweep(jb, None, False)
        return jb - 1, live()

    lax.while_loop(cond, body, (i * n_diag - 1, live()))
    for h in range(hps):
        o_ref[:, _head_cols(h)] = (acc_refs[h][...] * _silu(g_ref[:, _head_cols(h)])).astype(o_ref.dtype)


def sb_prompt(proj, tq, tk, heads_per_step):
    _, t, _ = proj.shape
    hps = heads_per_step
    width = hps * HEAD_DIM

    def spec(rows, s, per_q):
        if per_q:
            return pl.BlockSpec((None, rows, width), lambda h, i: (s, i, h))
        return pl.BlockSpec((None, rows, width), lambda h, i: (s, 0, h))

    return pl.pallas_call(
        functools.partial(_sb_prompt_kernel, tq=tq, tk=tk, hps=hps),
        grid=(N_HEADS // hps, t // tq),
        in_specs=[spec(tq, S_SB_Q, True), spec(t, S_SB_K, False), spec(t, S_SB_V, False),
                  spec(tq, S_SB_G, True)],
        out_specs=pl.BlockSpec((tq, width), lambda h, i: (i, h)),
        out_shape=jax.ShapeDtypeStruct((t, GROUP_W), BF16),
        scratch_shapes=[pltpu.VMEM((tq, HEAD_DIM), F32)] * hps + [pltpu.VMEM((tq, 1), F32)] * hps,
        compiler_params=_cparams(("parallel", "arbitrary")),
        name="sb_prompt",
    )(proj, proj, proj, proj)


def _page_masks(rows, rows_per_head, page_rows):
    row = lax.broadcasted_iota(jnp.int32, (rows, page_rows), 0)
    col = lax.broadcasted_iota(jnp.int32, (rows, page_rows), 1)
    own = (col & (N_HEADS - 1)) == (row >> _log2(rows_per_head))
    return own, col >> _log2(N_HEADS), row & (PAD_T - 1)


def _sb_decode_kernel(pt_ref, q_ref, kn_ref, vn_ref, g_ref, u_ref, ck_ref, cv_ref, o_ref,
                      kbuf, vbuf, newk, newv, sem, acc_ref, carry_ref, *, layer, n_pages, n_tok):
    b = pl.program_id(0)
    rows = N_HEADS * PAD_T
    page_rows = newk.shape[0]

    ring = kbuf.shape[1]
    par = b % 2
    n_prime = min(SB_LOOKAHEAD, n_pages)

    def page_copies(bb, v, pr):
        pg = pt_ref[bb, n_pages - 1 - v]
        sl = v % ring
        return (pltpu.make_async_copy(ck_ref.at[layer, pg], kbuf.at[pr, sl], sem.at[0, pr, sl]),
                pltpu.make_async_copy(cv_ref.at[layer, pg], vbuf.at[pr, sl], sem.at[1, pr, sl]))

    def start_fetch(bb, v, pr):
        for c in page_copies(bb, v, pr):
            c.start()

    def wait_fetch(v):
        for c in page_copies(b, v, par):
            c.wait()

    @pl.when(b == 0)
    def _():
        for v in range(n_prime):
            start_fetch(b, v, par)

    @pl.when(b + 1 < pl.num_programs(0))
    def _():
        for v in range(n_prime):
            start_fetch(b + 1, v, 1 - par)

    qall = jnp.concatenate([q_ref[h] for h in range(N_HEADS)], axis=0).astype(BF16)
    own, key, tok = _page_masks(rows, PAD_T, page_rows)
    real_row = (lax.broadcasted_iota(jnp.int32, (rows, 1), 0) & (PAD_T - 1)) < n_tok
    u = u_ref[...]

    def attend(k, v, visible, carry):
        ls, lk = _sb_terms(_qk(qall, k) * SB_SCALE_LOG2)
        lk = jnp.where(visible, lk, 0.0)
        e = ls + _suffix_sums(lk, u)
        if carry is not None:
            e = e + carry
        a = jnp.where(visible, jnp.exp2(e), 0.0).astype(BF16)
        return (jnp.dot(a, v.astype(BF16), preferred_element_type=F32),
                jnp.sum(lk, axis=1, keepdims=True))

    def live(carry):
        return jnp.max(jnp.where(real_row, carry, -jnp.inf)) > SB_EXIT_LOG2

    newk[...] = jnp.zeros_like(newk)
    newv[...] = jnp.zeros_like(newv)
    newk[0:kn_ref.shape[0], :] = kn_ref[...]
    newv[0:vn_ref.shape[0], :] = vn_ref[...]
    pv, dc = attend(newk[...], newv[...], jnp.logical_and(own, key < tok), None)
    acc_ref[...] = pv
    carry_ref[...] = dc

    def cond(state):
        v, go = state
        return jnp.logical_and(v < n_pages, go)

    def body(state):
        v, _ = state
        wait_fetch(v)

        @pl.when(v + SB_LOOKAHEAD < n_pages)
        def _():
            start_fetch(b, v + SB_LOOKAHEAD, par)

        slot = v % ring
        carry = carry_ref[...]
        pv, dc = attend(kbuf[par, slot], vbuf[par, slot], own, carry)
        acc_ref[...] += pv
        carry = carry + dc
        carry_ref[...] = carry
        return v + 1, live(carry)

    v_end, _ = lax.while_loop(cond, body, (jnp.int32(0), live(carry_ref[...])))

    for k in range(SB_LOOKAHEAD):
        @pl.when(v_end + k < n_pages)
        def _(k=k):
            wait_fetch(v_end + k)

    acc = acc_ref[...]
    o_ref[...] = jnp.concatenate([acc[h * PAD_T:(h + 1) * PAD_T] * _silu(g_ref[h])
                                  for h in range(N_HEADS)], axis=1)


def sb_decode(q8, kn, vn, g8, cache_k, cache_v, page_table, layer, n_tok):
    bsz = page_table.shape[0]
    n_pages = page_table.shape[1]
    page_rows = cache_k.shape[2]
    tok_spec = pl.BlockSpec((None, PAD_T, GROUP_W), lambda b, pt: (b, 0, 0))
    head_spec = pl.BlockSpec((N_HEADS, PAD_T, HEAD_DIM), lambda b, pt: (0, b, 0))
    new_spec = pl.BlockSpec((None, kn.shape[1], HEAD_DIM), lambda b, pt: (b, 0, 0))
    any_spec = pl.BlockSpec(memory_space=pl.ANY)
    rows = N_HEADS * PAD_T
    row, col = _strict_lower(page_rows)
    u = jnp.where(row > col, 1.0, 0.0).astype(BF16)
    page_buf = pltpu.VMEM((2, SB_LOOKAHEAD + 1, page_rows, HEAD_DIM), F32)
    new_buf = pltpu.VMEM((page_rows, HEAD_DIM), F32)
    return pl.pallas_call(
        functools.partial(_sb_decode_kernel, layer=layer, n_pages=n_pages, n_tok=n_tok),
        grid_spec=pltpu.PrefetchScalarGridSpec(
            num_scalar_prefetch=1,
            grid=(bsz,),
            in_specs=[head_spec, new_spec, new_spec, head_spec,
                      pl.BlockSpec((page_rows, page_rows), lambda b, pt: (0, 0)), any_spec, any_spec],
            out_specs=tok_spec,
            scratch_shapes=[page_buf, page_buf, new_buf, new_buf,
                            pltpu.SemaphoreType.DMA((2, 2, SB_LOOKAHEAD + 1)),
                            pltpu.VMEM((rows, HEAD_DIM), F32), pltpu.VMEM((rows, 1), F32)]),
        out_shape=jax.ShapeDtypeStruct((bsz, PAD_T, GROUP_W), F32),
        compiler_params=_cparams(("arbitrary",)),
        name="sb_decode",
    )(page_table, q8, kn, vn, g8, u, cache_k, cache_v)


def _lambda(lq1_ref, lk1_ref, lq2_ref, lk2_ref, lam_init):
    s1 = jnp.sum(lq1_ref[...] * lk1_ref[...], axis=-1, keepdims=True)
    s2 = jnp.sum(lq2_ref[...] * lk2_ref[...], axis=-1, keepdims=True)
    return jnp.exp(s1) - jnp.exp(s2) + lam_init


def _diff_finish(o1, o2, lam, sg, gate, lam_init):
    w = o1 - lam * o2
    ms = jnp.mean(w * w, axis=-1, keepdims=True)
    y = (w * lax.rsqrt(ms + EPS)) * sg
    return (y * (1.0 - lam_init)) * _silu(gate)


def _split_components(q):
    lane = lax.broadcasted_iota(jnp.int32, q.shape, 1)
    zero = jnp.zeros_like(q)
    return jnp.concatenate([jnp.where(lane < DIFF_QK, q, zero), jnp.where(lane >= DIFF_QK, q, zero)],
                           axis=0)


def _diff_prompt_kernel(q_ref, k_ref, v_ref, g_ref, lq1_ref, lk1_ref, lq2_ref, lk2_ref, sg_ref,
                        o_ref, s0_ref, s1_ref, mt0_ref, mt1_ref, m_ref, acc_ref, *, tq, tk, lam_init):
    i = pl.program_id(1)
    qq = _split_components(q_ref[...])
    m_ref[...] = jnp.full_like(m_ref, -jnp.inf)
    acc_ref[...] = jnp.zeros_like(acc_ref)
    bufs = ((s0_ref, mt0_ref), (s1_ref, mt1_ref))

    def scores(jt, diag=None):
        s = _qk(qq, k_ref[pl.ds(pl.multiple_of(jt * tk, tk), tk), :])
        if diag is not None:
            row = lax.broadcasted_iota(jnp.int32, s.shape, 0) & (tq - 1)
            col = lax.broadcasted_iota(jnp.int32, s.shape, 1) + diag * tk
            s = jnp.where(col <= row, s, -jnp.inf)
        return s

    def score_stage(slot, s):
        s_ref, mt_ref = bufs[slot]
        s_ref[...] = s
        mt_ref[...] = jnp.broadcast_to(jnp.max(s, axis=1, keepdims=True), mt_ref.shape)

    def value_stage(jt, slot):
        s_ref, mt_ref = bufs[slot]
        m_old = m_ref[...]
        m_new = jnp.maximum(m_old, mt_ref[...])
        alpha = jnp.exp2(m_old - m_new)
        p = jnp.concatenate([jnp.exp2(c - m_new) for c in _lane_chunks(s_ref[...])], axis=1)
        v = v_ref[pl.ds(pl.multiple_of(jt * tk, tk), tk), :]
        pv = jnp.dot(p.astype(BF16), _with_ones(v), preferred_element_type=F32)
        acc_ref[...] = jnp.concatenate([alpha, alpha], axis=1) * acc_ref[...] + pv
        m_ref[...] = m_new

    d0 = 2 * i

    @pl.when(i == 0)
    def _():
        score_stage(0, scores(0, 0))
        score_stage(1, scores(1, 1))
        value_stage(0, 0)
        value_stage(1, 1)

    @pl.when(i > 0)
    def _():
        score_stage(0, scores(0))

        def body(g, c):
            score_stage(1, scores(2 * g + 1))
            value_stage(2 * g, 0)
            score_stage(0, scores(2 * g + 2))
            value_stage(2 * g + 1, 1)
            return c

        lax.fori_loop(0, i - 1, body, 0)
        score_stage(1, scores(d0 - 1))
        value_stage(d0 - 2, 0)
        score_stage(0, scores(d0, 0))
        value_stage(d0 - 1, 1)
        score_stage(1, scores(d0 + 1, 1))
        value_stage(d0, 0)
        value_stage(d0 + 1, 1)

    acc = acc_ref[...]
    o = acc[:, :HEAD_DIM] / acc[:, HEAD_DIM:]
    lam = _lambda(lq1_ref, lk1_ref, lq2_ref, lk2_ref, lam_init)
    o_ref[...] = _diff_finish(o[:tq], o[tq:], lam, sg_ref[...], g_ref[...], lam_init).astype(o_ref.dtype)


def diff_prompt(qb, kb, vb, proj, lam_params, subln_g, lam_init, tk):
    t = qb.shape[0]
    tq = 2 * tk
    small = pl.BlockSpec((1, DIFF_QK), lambda h, i: (0, 0))
    return pl.pallas_call(
        functools.partial(_diff_prompt_kernel, tq=tq, tk=tk, lam_init=lam_init),
        grid=(N_HEADS, t // tq),
        in_specs=[pl.BlockSpec((tq, HEAD_DIM), lambda h, i: (i, h)),
                  pl.BlockSpec((t, HEAD_DIM), lambda h, i: (0, h)),
                  pl.BlockSpec((t, HEAD_DIM), lambda h, i: (0, h)),
                  pl.BlockSpec((None, tq, HEAD_DIM), lambda h, i: (S_DF_G, i, h)),
                  small, small, small, small,
                  pl.BlockSpec((1, HEAD_DIM), lambda h, i: (0, 0))],
        out_specs=pl.BlockSpec((tq, HEAD_DIM), lambda h, i: (i, h)),
        out_shape=jax.ShapeDtypeStruct((t, GROUP_W), BF16),
        scratch_shapes=[pltpu.VMEM((2 * tq, tk), F32), pltpu.VMEM((2 * tq, tk), F32),
                        pltpu.VMEM((2 * tq, HEAD_DIM), F32), pltpu.VMEM((2 * tq, HEAD_DIM), F32),
                        pltpu.VMEM((2 * tq, HEAD_DIM), F32), pltpu.VMEM((2 * tq, 2 * HEAD_DIM), F32)],
        compiler_params=_cparams(("parallel", "arbitrary")),
        name="diff_prompt",
    )(qb, kb, vb, proj, *[p.reshape(1, DIFF_QK) for p in lam_params], subln_g.reshape(1, HEAD_DIM))


def _diff_decode_kernel(pt_ref, q_ref, kn_ref, vn_ref, g_ref, lq1_ref, lk1_ref, lq2_ref, lk2_ref,
                        sg_ref, ck_ref, cv_ref, o_ref, kbuf, vbuf, sem, m_ref, acc_ref, newk, newv, *,
                        layer, pages_per_step, n_chunks, group, lam_init):
    b = pl.program_id(0)
    c = pl.program_id(1)
    hrows = 2 * PAD_T
    rows = N_HEADS * hrows
    page_rows = newk.shape[0]

    n_slots = kbuf.shape[0]
    n_steps = pl.num_programs(0) * n_chunks
    step = b * n_chunks + c
    slot = step % n_slots

    def step_copies(t):
        bb, cc, sl = t // n_chunks, t % n_chunks, t % n_slots
        copies = []
        for n in range(pages_per_step):
            pg = pt_ref[bb, cc * pages_per_step + n]
            copies.append(pltpu.make_async_copy(ck_ref.at[layer, pg], kbuf.at[sl, n], sem.at[sl, 0, n]))
            copies.append(pltpu.make_async_copy(cv_ref.at[layer, pg], vbuf.at[sl, n], sem.at[sl, 1, n]))
        return copies

    def start_step(t):
        for n, cp in enumerate(step_copies(t)):
            cp.start(priority=n % 2)

    @pl.when(step == 0)
    def _():
        for t in range(DECODE_PREFETCH_STEPS):
            @pl.when(t < n_steps)
            def _(t=t):
                start_step(t)

    @pl.when(step + DECODE_PREFETCH_STEPS < n_steps)
    def _():
        start_step(step + DECODE_PREFETCH_STEPS)

    for cp in step_copies(step):
        cp.wait()

    qall = jnp.concatenate([_split_components(q_ref[h] * (DIFF_SCALE * LOG2E))
                            for h in range(N_HEADS)], axis=0).astype(BF16)
    zq = jnp.zeros_like(qall)
    qpair = jnp.concatenate([jnp.concatenate([qall, zq], axis=1),
                             jnp.concatenate([zq, qall], axis=1)], axis=0)
    own, key, tok = _page_masks(rows, hrows, page_rows)

    @pl.when(c == 0)
    def _():
        m_ref[...] = jnp.full_like(m_ref, -jnp.inf)
        acc_ref[...] = jnp.zeros_like(acc_ref)

    def page_scores(ks):
        out = []
        for n in range(0, len(ks) - 1, 2):
            kk = jnp.concatenate([ks[n].astype(BF16), ks[n + 1].astype(BF16)], axis=1)
            s2 = _qk(qpair, kk)
            out += [s2[:rows], s2[rows:]]
        if len(ks) % 2:
            out.append(_qk(qall, ks[-1]))
        return out

    def update(ks, vs, visible):
        s = jnp.concatenate([jnp.where(visible, z, -jnp.inf) for z in page_scores(ks)], axis=1)
        m_prev = m_ref[...]
        m_new = jnp.maximum(m_prev, jnp.max(s, axis=1, keepdims=True))
        p = jnp.exp2(s - m_new).astype(BF16)
        pv = _tree(jnp.add, [jnp.dot(p[:, n * page_rows:(n + 1) * page_rows], _with_ones(v),
                                     preferred_element_type=F32) for n, v in enumerate(vs)])
        return m_new, jnp.exp2(m_prev - m_new) * acc_ref[...] + pv

    for g0 in range(0, pages_per_step, group):
        m_new, acc_new = update([kbuf[slot, n] for n in range(g0, g0 + group)],
                                [vbuf[slot, n] for n in range(g0, g0 + group)], own)
        m_ref[...] = m_new
        acc_ref[...] = acc_new

    @pl.when(c == pl.num_programs(1) - 1)
    def _():
        newk[...] = jnp.zeros_like(newk)
        newv[...] = jnp.zeros_like(newv)
        newk[0:kn_ref.shape[0], :] = kn_ref[...]
        newv[0:vn_ref.shape[0], :] = vn_ref[...]
        _, acc = update([newk[...]], [newv[...]], jnp.logical_and(own, key <= tok))
        o = acc[:, :HEAD_DIM] / acc[:, HEAD_DIM:]
        lam = _lambda(lq1_ref, lk1_ref, lq2_ref, lk2_ref, lam_init)
        sg = sg_ref[...]
        outs = []
        for h in range(N_HEADS):
            o1 = o[h * hrows:h * hrows + PAD_T]
            o2 = o[h * hrows + PAD_T:(h + 1) * hrows]
            outs.append(_diff_finish(o1, o2, lam, sg, g_ref[h], lam_init))
        o_ref[...] = jnp.concatenate(outs, axis=1)


def diff_decode(q8, kn, vn, g8, cache_k, cache_v, page_table, lam_params, subln_g, lam_init,
                layer, pages_per_step, group):
    bsz = page_table.shape[0]
    n_pages = page_table.shape[1]
    page_rows = cache_k.shape[2]
    pps = pages_per_step
    tok_spec = pl.BlockSpec((None, PAD_T, GROUP_W), lambda b, c, pt: (b, 0, 0))
    head_spec = pl.BlockSpec((N_HEADS, PAD_T, HEAD_DIM), lambda b, c, pt: (0, b, 0))
    new_spec = pl.BlockSpec((None, kn.shape[1], HEAD_DIM), lambda b, c, pt: (b, 0, 0))
    small = pl.BlockSpec((1, DIFF_QK), lambda b, c, pt: (0, 0))

    rows = N_HEADS * 2 * PAD_T
    any_spec = pl.BlockSpec(memory_space=pl.ANY)
    n_slots = DECODE_PREFETCH_STEPS + 1
    page_bufs = pltpu.VMEM((n_slots, pps, page_rows, HEAD_DIM), F32)
    new_buf = pltpu.VMEM((page_rows, HEAD_DIM), F32)
    return pl.pallas_call(
        functools.partial(_diff_decode_kernel, layer=layer, pages_per_step=pps,
                          n_chunks=n_pages // pps, group=group, lam_init=lam_init),
        grid_spec=pltpu.PrefetchScalarGridSpec(
            num_scalar_prefetch=1,
            grid=(bsz, n_pages // pps),
            in_specs=[head_spec, new_spec, new_spec, head_spec, small, small, small, small,
                      pl.BlockSpec((1, HEAD_DIM), lambda b, c, pt: (0, 0)), any_spec, any_spec],
            out_specs=tok_spec,
            scratch_shapes=[page_bufs, page_bufs, pltpu.SemaphoreType.DMA((n_slots, 2, pps)),
                            pltpu.VMEM((rows, 1), F32), pltpu.VMEM((rows, 2 * HEAD_DIM), F32),
                            new_buf, new_buf]),
        out_shape=jax.ShapeDtypeStruct((bsz, PAD_T, GROUP_W), F32),
        compiler_params=_cparams(("arbitrary", "arbitrary")),
        name="diff_decode",
    )(page_table, q8, kn, vn, g8, *[p.reshape(1, DIFF_QK) for p in lam_params],
      subln_g.reshape(1, HEAD_DIM), cache_k, cache_v)


CONF_HALO = 32
SC_HALO = 8
SUBLANES = 8
MIX_ROWS = 32


def _rope_tables(pos):
    half = ROT_DIM // 2
    inv = ROPE_THETA ** (-2.0 * jnp.arange(half, dtype=F32) / ROT_DIM)
    ang = pos.astype(F32)[:, None] * inv[None, :]
    cos, sin = jnp.cos(ang), jnp.sin(ang)
    j = jnp.arange(HEAD_DIM) % DIFF_QK
    f = j % half
    c = jnp.where(j < ROT_DIM, cos[:, f], 1.0)
    sa = jnp.where((j >= half) & (j < ROT_DIM), sin[:, f], 0.0)
    sb = jnp.where(j < half, -sin[:, f], 0.0)
    return c, sa, sb


def _rope_rows(x, c, sa, sb):
    half = ROT_DIM // 2
    out = []
    for xh in _lane_chunks(x):
        out.append(xh * c + pltpu.roll(xh, half, 1) * sa + pltpu.roll(xh, HEAD_DIM - half, 1) * sb)
    return jnp.concatenate(out, axis=1)


def _prompt_mix_kernel(a_ref, b_ref, g_ref, dq_ref, dk_ref, dv_ref, sh_ref, sb_ref, sc_ref, sg_ref,
                       sk_ref, sv_ref, cw_ref, cb_ref, lg_ref, lb_ref, sw_ref, rc_ref, rsa_ref, rsb_ref,
                       *rest, tb, conf_w, sc_w, layer, whole):
    (ya_ref, yd_ref, dqb_ref, sk_out_ref, sv_out_ref, dk_out_ref, dv_out_ref, dkb_ref, dvb_ref,
     nconf_ref, nsc_ref, hb, hs, ub) = rest[-14:]
    i = pl.program_id(0)

    @pl.when(i == 0)
    def _():
        hb[0:CONF_HALO, :] = jnp.zeros((CONF_HALO, GROUP_W), F32)
        ub[0:SC_HALO, :] = jnp.zeros((SC_HALO, GROUP_W), F32)

    @pl.when(i > 0)
    def _():
        hb[0:CONF_HALO, :] = hb[tb:tb + CONF_HALO, :]
        ub[0:SC_HALO, :] = ub[tb:tb + SC_HALO, :]

    hb[CONF_HALO:CONF_HALO + tb, :] = a_ref[...] * (1.0 / (1.0 + jnp.exp(-b_ref[...])))
    ub[SC_HALO:SC_HALO + tb, :] = sc_ref[...] * sh_ref[...]

    off = CONF_HALO - (conf_w - 1)
    span = hs.shape[1]
    for k in range(1, SUBLANES):
        hs[k - 1, :, :] = hb[k:k + span, :]

    def tap(w_ref, w):
        return jnp.concatenate([w_ref[w]] * (MIX_ROWS // SUBLANES), axis=0)

    def window(r):
        k, base = r % SUBLANES, r - r % SUBLANES
        if k == 0:
            return hb[base:base + MIX_ROWS, :]
        return hs[k - 1, base:base + MIX_ROWS, :]

    for r0 in range(0, tb, MIX_ROWS):
        rows = slice(r0, r0 + MIX_ROWS)
        acc = jnp.zeros((MIX_ROWS, GROUP_W), F32) + cb_ref[...]
        for w in range(conf_w):
            acc = acc + tap(cw_ref, w) * window(r0 + off + w)
        mu = jnp.mean(acc, axis=-1, keepdims=True)
        xc = acc - mu
        y = (xc * lax.rsqrt(jnp.mean(xc * xc, axis=-1, keepdims=True) + EPS)) * lg_ref[...] + lb_ref[...]
        ya_ref[rows, :] = (_silu(y) * _silu(g_ref[rows, :])).astype(ya_ref.dtype)
        u0 = r0 + SC_HALO - (sc_w - 1)
        conv = jnp.zeros((MIX_ROWS, GROUP_W), F32)
        for w in range(sc_w):
            conv = conv + tap(sw_ref, w) * ub[u0 + w:u0 + w + MIX_ROWS, :]
        yd_ref[rows, :] = ((sb_ref[rows, :] * conv) * _silu(sg_ref[rows, :])).astype(yd_ref.dtype)

    c, sa, sb = rc_ref[...], rsa_ref[...], rsb_ref[...]
    dqb_ref[...] = (_rope_rows(dq_ref[...], c, sa, sb) * (DIFF_SCALE * LOG2E)).astype(dqb_ref.dtype)
    dk = _rope_rows(dk_ref[...], c, sa, sb)
    _store_layer_head_major(dk_out_ref, dk, layer, whole)
    dkb_ref[...] = dk.astype(dkb_ref.dtype)
    dv = dv_ref[...]
    dvb_ref[...] = dv.astype(dvb_ref.dtype)
    _store_layer_head_major(dv_out_ref, dv, layer, whole)
    _store_layer_head_major(sk_out_ref, sk_ref[...], layer, whole)
    _store_layer_head_major(sv_out_ref, sv_ref[...], layer, whole)

    @pl.when(i == pl.num_programs(0) - 1)
    def _():
        nconf_ref[...] = hb[CONF_HALO + tb - (conf_w - 1):CONF_HALO + tb, :]
        nsc_ref[...] = ub[SC_HALO + tb - (sc_w - 1):SC_HALO + tb, :]


def prompt_mix(proj, conf_w, conf_b, ln_g, ln_b, sc_w, rope_tabs, tb, kv_bufs=None, layer=0, depth=1):
    _, t, _ = proj.shape
    cw, scw = conf_w.shape[0], sc_w.shape[0]

    def slab(s):
        return pl.BlockSpec((None, tb, GROUP_W), lambda i: (s, i, 0))

    def full(r):
        return pl.BlockSpec((r, GROUP_W), lambda i: (0, 0))

    def taps(n):
        return pl.BlockSpec((n, SUBLANES, GROUP_W), lambda i: (0, 0, 0))

    def replicate(w):
        return jnp.broadcast_to(w[:, None, :], (w.shape[0], SUBLANES, w.shape[1]))

    row_blk = pl.BlockSpec((tb, GROUP_W), lambda i: (i, 0))
    tab_blk = pl.BlockSpec((tb, HEAD_DIM), lambda i: (i, 0))
    slabs = (S_CONF_A, S_CONF_B, S_CONF_G, S_DF_Q, S_DF_K, S_DF_V, S_SC_H, S_SC_B, S_SC_C, S_SC_G,
             S_SB_K, S_SB_V)
    act = jax.ShapeDtypeStruct((t, GROUP_W), BF16)
    per_layer = jax.ShapeDtypeStruct((depth, t * N_HEADS, HEAD_DIM), F32)
    n_kv = 4
    alias_in = [] if kv_bufs is None else list(kv_bufs)
    kv_spec = _layer_rows_spec(tb * N_HEADS, depth, layer, not alias_in)
    n_in = len(slabs) + 8
    return pl.pallas_call(
        functools.partial(_prompt_mix_kernel, tb=tb, conf_w=cw, sc_w=scw, layer=layer,
                          whole=not alias_in),
        grid=(t // tb,),
        in_specs=[slab(s) for s in slabs]
                 + [taps(cw), full(1), full(1), full(1), taps(scw), tab_blk, tab_blk, tab_blk]
                 + [pl.BlockSpec(memory_space=pl.ANY)] * len(alias_in),
        out_specs=[row_blk, row_blk, row_blk] + [kv_spec] * n_kv
                  + [row_blk, row_blk, full(cw - 1), full(scw - 1)],
        out_shape=[act, act, act] + [per_layer] * n_kv + [act, act,
                   jax.ShapeDtypeStruct((cw - 1, GROUP_W), F32),
                   jax.ShapeDtypeStruct((scw - 1, GROUP_W), F32)],
        scratch_shapes=[pltpu.VMEM((CONF_HALO + tb, GROUP_W), F32),
                        pltpu.VMEM((SUBLANES - 1, CONF_HALO + tb - SUBLANES, GROUP_W), F32),
                        pltpu.VMEM((SC_HALO + tb, GROUP_W), F32)],
        input_output_aliases={n_in + n: 3 + n for n in range(len(alias_in))},
        compiler_params=_cparams(("arbitrary",)),
        name="prompt_mix",
    )(*([proj] * len(slabs)), replicate(conf_w), conf_b.reshape(1, -1), ln_g.reshape(1, -1),
      ln_b.reshape(1, -1), replicate(sc_w), *rope_tabs, *alias_in)


def _decode_mix_kernel(proj_ref, sconf_ref, ssc_ref, cw_ref, cb_ref, lg_ref, lb_ref, sw_ref,
                       rc_ref, rsa_ref, rsb_ref,
                       ya_ref, yd_ref, sbq_ref, sbg_ref, dfq_ref, dfg_ref, sbk_ref, sbv_ref, dfk_ref,
                       dfv_ref, nconf_ref, nsc_ref, *, bsz, n_tok, conf_w, sc_w):
    def tok(slab, t):
        return proj_ref[slab, t * bsz:(t + 1) * bsz, :]

    def put_tok(ref, t, x):
        ref[t * bsz:(t + 1) * bsz, :] = x.astype(ref.dtype)

    def put_pad(ref, t, x):
        for h in range(N_HEADS):
            ref[h, pl.ds(t, bsz, stride=PAD_T), :] = x[:, _head_cols(h)]

    def put_heads(ref, t, x):
        for h in range(N_HEADS):
            ref[pl.ds(t * N_HEADS + h, bsz, stride=n_tok * N_HEADS), :] = x[:, _head_cols(h)]

    def tap(w_ref, w):
        if bsz % SUBLANES:
            return jnp.broadcast_to(w_ref[w, 0:1, :], (bsz, GROUP_W))
        return jnp.concatenate([w_ref[w]] * (bsz // SUBLANES), axis=0)

    h = [tok(S_CONF_A, t) * (1.0 / (1.0 + jnp.exp(-tok(S_CONF_B, t)))) for t in range(n_tok)]

    def hist(r):
        return sconf_ref[r] if r < conf_w - 1 else h[r - (conf_w - 1)]

    for t in range(n_tok):
        acc = jnp.zeros((bsz, GROUP_W), F32) + cb_ref[...]
        for w in range(conf_w):
            acc = acc + tap(cw_ref, w) * hist(t + w)
        mu = jnp.mean(acc, axis=-1, keepdims=True)
        xc = acc - mu
        y = (xc * lax.rsqrt(jnp.mean(xc * xc, axis=-1, keepdims=True) + EPS)) * lg_ref[...] + lb_ref[...]
        put_tok(ya_ref, t, _silu(y) * _silu(tok(S_CONF_G, t)))
    for r in range(conf_w - 1):
        nconf_ref[r] = hist(r + n_tok)

    u = [tok(S_SC_C, t) * tok(S_SC_H, t) for t in range(n_tok)]

    def uhist(r):
        return ssc_ref[r] if r < sc_w - 1 else u[r - (sc_w - 1)]

    for t in range(n_tok):
        conv = jnp.zeros((bsz, GROUP_W), F32)
        for w in range(sc_w):
            conv = conv + tap(sw_ref, w) * uhist(t + w)
        put_tok(yd_ref, t, (tok(S_SC_B, t) * conv) * _silu(tok(S_SC_G, t)))
    for r in range(sc_w - 1):
        nsc_ref[r] = uhist(r + n_tok)

    for ref in (sbq_ref, sbg_ref, dfq_ref, dfg_ref):
        ref[...] = jnp.zeros_like(ref)
    for t in range(n_tok):
        tabs = [jnp.broadcast_to(r[t:t + 1, :], (bsz, HEAD_DIM)) for r in (rc_ref, rsa_ref, rsb_ref)]
        put_pad(sbq_ref, t, tok(S_SB_Q, t))
        put_pad(sbg_ref, t, tok(S_SB_G, t))
        put_pad(dfq_ref, t, _rope_rows(tok(S_DF_Q, t), *tabs))
        put_pad(dfg_ref, t, tok(S_DF_G, t))
        put_heads(sbk_ref, t, tok(S_SB_K, t))
        put_heads(sbv_ref, t, tok(S_SB_V, t))
        put_heads(dfk_ref, t, _rope_rows(tok(S_DF_K, t), *tabs))
        put_heads(dfv_ref, t, tok(S_DF_V, t))


def decode_mix(proj, state_conf, state_sc, conf_w, conf_b, ln_g, ln_b, sc_w, rope_tabs, bsz, n_tok):
    cw, scw = conf_w.shape[0], sc_w.shape[0]

    def replicate(w):
        return jnp.broadcast_to(w[:, None, :], (w.shape[0], SUBLANES, w.shape[1]))

    act = jax.ShapeDtypeStruct((bsz * n_tok, GROUP_W), BF16)
    pad = jax.ShapeDtypeStruct((N_HEADS, bsz * PAD_T, HEAD_DIM), F32)
    hm = jax.ShapeDtypeStruct((bsz * n_tok * N_HEADS, HEAD_DIM), F32)
    return pl.pallas_call(
        functools.partial(_decode_mix_kernel, bsz=bsz, n_tok=n_tok, conf_w=cw, sc_w=scw),
        out_shape=[act, act, pad, pad, pad, pad, hm, hm, hm, hm,
                   jax.ShapeDtypeStruct(state_conf.shape, F32), jax.ShapeDtypeStruct(state_sc.shape, F32)],
        compiler_params=pltpu.CompilerParams(vmem_limit_bytes=VMEM_LIMIT),
        name="decode_mix",
    )(proj, state_conf, state_sc, replicate(conf_w), conf_b.reshape(1, -1), ln_g.reshape(1, -1),
      ln_b.reshape(1, -1), replicate(sc_w), *rope_tabs)


def _pick_tile(n, pref):
    t = min(n, pref)
    while n % t:
        t //= 2
    return t


def kernel(x_prompt, x_sample, cache_sb_k, cache_sb_v, cache_diff_k, cache_diff_v, state_conf_conv,
           state_short_conv, page_table, norm_g, w_in, conf_dw_w, conf_dw_b, conf_ln_g, conf_ln_b,
           diff_lq1, diff_lk1, diff_lq2, diff_lk2, diff_subln_g, sc_conv_w, w_out, final_norm_g):
    bp, tp, d = x_prompt.shape
    bs, ts, _ = x_sample.shape
    depth = w_in.shape[0]
    n_pages = page_table.shape[1]
    page = cache_sb_k.shape[2]
    n_past = n_pages * page
    pos_p = jnp.arange(tp, dtype=jnp.int32)
    pos_s = n_past + jnp.arange(ts, dtype=jnp.int32)
    assert bp == 1 and ts <= PAD_T

    caches = [c.reshape(c.shape[0], c.shape[1], page * N_HEADS, HEAD_DIM)
              for c in (cache_sb_k, cache_sb_v, cache_diff_k, cache_diff_v)]
    w_in_b = w_in.astype(BF16)
    w_out_b = w_out.astype(BF16)

    hp = x_prompt.reshape(bp * tp, d)
    hs = jnp.swapaxes(x_sample, 0, 1).reshape(ts * bs, d)
    conf_states = jnp.swapaxes(state_conf_conv, 1, 2)
    sc_states = jnp.swapaxes(state_short_conv, 1, 2)
    tm_p = _pick_tile(bp * tp, 1024)
    slabs_per_step_p = 3
    slabs_per_step = 5
    assert N_IN_SLICES % slabs_per_step == 0 and N_IN_SLICES % slabs_per_step_p == 0
    tm_s = _pick_tile(bs * ts, 128)
    tq_sb = _pick_tile(tp, 512)
    tk_sb = _pick_tile(tq_sb, 256)
    tk_df = _pick_tile(tp // 2, 512)
    tb_mix = _pick_tile(tp, 256)
    assert tb_mix >= CONF_HALO >= conf_dw_w.shape[1] - 1 and SC_HALO >= sc_conv_w.shape[1] - 1
    rope_tabs_p = _rope_tables(pos_p)
    rope_tabs_s = _rope_tables(pos_s)
    pps = _pick_tile(n_pages, 16)
    pgroup = _pick_tile(pps, 2)

    p_conv = [[], []]
    s_new = [[] for _ in range(6)]
    kv_bufs = None
    for l in range(depth):
        lam_init = 0.8 - 0.6 * math.exp(-0.3 * l)
        lam_params = (diff_lq1[l], diff_lk1[l], diff_lq2[l], diff_lk2[l])
        final = l == depth - 1

        proj = in_proj(hp, norm_g[l], w_in_b, l, tm_p, slabs_per_step_p)
        ya, yd, dqb, *kv_bufs, dkb, dvb, new_conf, new_sc = prompt_mix(
            proj, conf_dw_w[l], conf_dw_b[l], conf_ln_g[l], conf_ln_b[l], sc_conv_w[l], rope_tabs_p,
            tb_mix, kv_bufs=kv_bufs, layer=l, depth=depth)
        yb = sb_prompt(proj, tq_sb, tk_sb, 2)
        yc = diff_prompt(dqb, dkb, dvb, proj, lam_params, diff_subln_g[l], lam_init, tk_df)
        hp = out_proj((ya, yb, yc, yd), w_out_b, l, hp, final_norm_g, final,
                      _pick_tile(bp * tp, 512))
        p_conv[0].append(new_conf[None])
        p_conv[1].append(new_sc[None])
        heads = lambda a, bb, tt: a.reshape(bb, tt, N_HEADS, HEAD_DIM)

        proj = in_proj(hs, norm_g[l], w_in_b, l, tm_s, slabs_per_step)
        ya, yd, sbq, sbg, dfq, dfg, sbk, sbv, dfk, dfv, new_conf, new_sc = decode_mix(
            proj, conf_states[l], sc_states[l], conf_dw_w[l], conf_dw_b[l], conf_ln_g[l],
            conf_ln_b[l], sc_conv_w[l], rope_tabs_s, bs, ts)
        new_rows = lambda a: a.reshape(bs, ts * N_HEADS, HEAD_DIM)
        yb = sb_decode(sbq, new_rows(sbk), new_rows(sbv), sbg, caches[0], caches[1], page_table, l, ts)
        yc = diff_decode(dfq, new_rows(dfk), new_rows(dfv), dfg, caches[2], caches[3], page_table,
                         lam_params, diff_subln_g[l], lam_init, l, pps, pgroup)
        flat = lambda a: jnp.swapaxes(a[:, :ts], 0, 1).reshape(ts * bs, GROUP_W).astype(BF16)
        hs = out_proj((ya, flat(yb), flat(yc), yd), w_out_b, l, hs, final_norm_g, final, tm_s)
        for n, a in enumerate((sbk, sbv, dfk, dfv)):
            s_new[n].append(a.reshape(bs, ts, N_HEADS, HEAD_DIM))
        s_new[4].append(new_conf)
        s_new[5].append(new_sc)

    y_prompt = hp.reshape(bp, tp, d)
    y_sample = jnp.swapaxes(hs.reshape(ts, bs, d), 0, 1)
    p_kv = [a.reshape(depth, bp, tp, N_HEADS, HEAD_DIM)
            for a in kv_bufs]
    s_out = [jnp.stack(a, 0) for a in s_new]
    s_out[4:] = [jnp.swapaxes(a, 1, 2) for a in s_out[4:]]
    return (y_prompt, y_sample, *p_kv, *[jnp.stack(a, 0) for a in p_conv], *s_out)
```
